```python
import numpy as np
import jax
import jax.numpy as jnp
from jax import lax

D_MODEL = 1024
BATCH = 16
SEQ = 2048
DEPTH = 1
DEC_BATCH = 32
DEC_SEQ = 32
PAST_LEN = 4096

CHUNK = 64
MIX_WIDTH = D_MODEL
GLA_HEADS = 4
GLA_VALUE_WIDTH = MIX_WIDTH // 2
GLA_KEY_WIDTH = GLA_VALUE_WIDTH // 2
GLA_HEAD_K = GLA_KEY_WIDTH // GLA_HEADS
GLA_HEAD_V = GLA_VALUE_WIDTH // GLA_HEADS
GLA_LOW_RANK = 16
GLA_GATE_NORMALIZER = 16.0
FOX_WIDTH = MIX_WIDTH - GLA_VALUE_WIDTH
FOX_HEAD_DIM = 64
FOX_HEADS = FOX_WIDTH // FOX_HEAD_DIM
FOX_Q_BLOCK = 128
IN_WIDTH = 2 * GLA_KEY_WIDTH + 2 * GLA_VALUE_WIDTH + GLA_LOW_RANK + 3 * FOX_WIDTH + FOX_HEADS
N_EXPERTS = 64
N_GROUPS = 8
TOPK_GROUPS = 4
TOP_K = 8
D_EXPERT = 256
D_SHARED = 256
ROUTED_SCALE = 2.5
EXPERT_BLOCK = 256
NORM_EPS = 1e-6

kernel_name = 'gla_fox_moe_adaln_stream_step'


def rms_norm(x, g):
    xf = x.astype(jnp.float32)
    y = xf * lax.rsqrt(jnp.mean(xf * xf, axis=-1, keepdims=True) + NORM_EPS)
    return (y * g.astype(jnp.float32)).astype(x.dtype)


def ada_modulation(c, w_ada, b_ada):
    m = (jax.nn.silu(c) @ w_ada + b_ada)[:, None, :]
    return jnp.split(m, 6, axis=-1)


def gla_chunked(q, k, v, log_a, s0, block):
    bsz, length, heads, dk = q.shape
    n_blocks = length // block

    def to_blocks(t):
        t = t.astype(jnp.float32).reshape(bsz, n_blocks, block, heads, t.shape[-1])
        return t.transpose(1, 0, 3, 2, 4)

    qb = to_blocks(q) * (dk ** -0.5)
    kb, vb, gb = to_blocks(k), to_blocks(v), to_blocks(log_a)
    causal = jnp.tril(jnp.ones((block, block), dtype=bool))
    mid = block // 2

    def step(s, xs):
        qc, kc, vc, gc = xs
        b = jnp.cumsum(gc, axis=-2)
        ref = b[..., mid:mid + 1, :]
        b_last = b[..., -1:, :]
        a = jnp.einsum('bhtk,bhsk->bhts', qc * jnp.exp(b - ref), kc * jnp.exp(ref - b))
        a = jnp.where(causal, a, 0.0)
        o = (jnp.einsum('bhtk,bhkv->bhtv', qc * jnp.exp(b), s)
             + jnp.einsum('bhts,bhsv->bhtv', a, vc))
        s_new = (s * jnp.swapaxes(jnp.exp(b_last), -1, -2)
                 + jnp.einsum('bhsk,bhsv->bhkv', kc * jnp.exp(b_last - b), vc))
        return s_new, o

    s_final, o = lax.scan(step, s0.astype(jnp.float32), (qb, kb, vb, gb))
    o = o.transpose(1, 0, 3, 2, 4).reshape(bsz, length, heads, v.shape[-1])
    return o, s_final


def fox_attention(q, cq, qpos, k, v, ck, kpos):
    bsz, lq, heads, d = q.shape
    qb = FOX_Q_BLOCK if lq % FOX_Q_BLOCK == 0 else lq
    n = lq // qb
    q_blocks = q.reshape(bsz, n, qb, heads, d).swapaxes(0, 1)
    c_blocks = cq.reshape(bsz, n, qb, heads).swapaxes(0, 1)
    p_blocks = qpos.reshape(n, qb)
    ck_t = ck.transpose(0, 2, 1)
    scale = d ** -0.5

    def one(args):
        qi, ci, pi = args
        s = jnp.einsum('bqhd,bkhd->bhqk', qi, k).astype(jnp.float32) * scale
        s = s + ci.transpose(0, 2, 1)[..., None] - ck_t[:, :, None, :]
        s = jnp.where(kpos[None, :] <= pi[:, None], s, -jnp.inf)
        p = jax.nn.softmax(s, axis=-1)
        return jnp.einsum('bhqk,bkhd->bqhd', p.astype(v.dtype), v)

    o = lax.map(one, (q_blocks, c_blocks, p_blocks))
    return o.swapaxes(0, 1).reshape(bsz, lq, heads, d)


def token_mixer(h, gla_s0, fox_past, pos0, gla_block, w_in, w_gla_gk2, b_gla_gk, g_gla_out,
                b_fox_f, g_fox_out, w_out):
    bsz, length, _ = h.shape
    z = h @ w_in
    sizes = [GLA_KEY_WIDTH, GLA_KEY_WIDTH, GLA_VALUE_WIDTH, GLA_VALUE_WIDTH, GLA_LOW_RANK,
             FOX_WIDTH, FOX_WIDTH, FOX_WIDTH, FOX_HEADS]
    splits = [int(i) for i in np.cumsum(sizes)[:-1]]
    gq, gk, gv, gg, glr, fq, fk, fv, ff = jnp.split(z, splits, axis=-1)

    def heads(t, nh):
        return t.reshape(bsz, length, nh, t.shape[-1] // nh)

    log_a = jax.nn.log_sigmoid((glr @ w_gla_gk2 + b_gla_gk).astype(jnp.float32)) / GLA_GATE_NORMALIZER
    o_gla, s_new = gla_chunked(heads(gq, GLA_HEADS), heads(gk, GLA_HEADS), heads(gv, GLA_HEADS),
                               heads(log_a, GLA_HEADS), gla_s0, gla_block)
    o_gla = rms_norm(o_gla.astype(h.dtype), g_gla_out) * jax.nn.silu(heads(gg, GLA_HEADS))

    q, k, v = heads(fq, FOX_HEADS), heads(fk, FOX_HEADS), heads(fv, FOX_HEADS)
    logf = jax.nn.log_sigmoid((ff + b_fox_f).astype(jnp.float32))
    c_new = jnp.cumsum(logf, axis=1)
    qpos = pos0 + jnp.arange(length)
    if fox_past is None:
        k_all, v_all, ck, kpos = k, v, c_new, qpos
    else:
        k_past, v_past, logf_past = fox_past
        logf_past = logf_past.astype(jnp.float32)
        suffix = lax.cumsum(logf_past, axis=1, reverse=True) - logf_past
        k_all = jnp.concatenate([k_past.astype(k.dtype), k], axis=1)
        v_all = jnp.concatenate([v_past.astype(v.dtype), v], axis=1)
        ck = jnp.concatenate([-suffix, c_new], axis=1)
        kpos = jnp.arange(k_all.shape[1])
    o_fox = fox_attention(q, c_new, qpos, k_all, v_all, ck, kpos)
    o_fox = rms_norm(o_fox.reshape(bsz, length, FOX_WIDTH), g_fox_out)

    merged = jnp.concatenate([o_gla.reshape(bsz, length, GLA_VALUE_WIDTH), o_fox], axis=-1)
    return merged @ w_out, s_new, k, v, logf


def route(t, w_router, b_router):
    n_tok = t.shape[0]
    scores = jax.nn.sigmoid((t @ w_router).astype(jnp.float32))
    choice = scores + b_router.astype(jnp.float32)
    grp = choice.reshape(n_tok, N_GROUPS, N_EXPERTS // N_GROUPS)
    grp_score = lax.top_k(grp, 2)[0].sum(-1)
    _, top_g = lax.top_k(grp_score, TOPK_GROUPS)
    gmask = jax.nn.one_hot(top_g, N_GROUPS, dtype=jnp.float32).sum(1) > 0
    emask = jnp.repeat(gmask, N_EXPERTS // N_GROUPS, axis=1)
    _, idx = lax.top_k(jnp.where(emask, choice, -jnp.inf), TOP_K)
    w = jnp.take_along_axis(scores, idx, axis=1)
    w = w / (w.sum(-1, keepdims=True) + 1e-20) * ROUTED_SCALE
    return idx, w


def routed_experts(t, idx, wts, w_gate, w_up, w_down):
    n_tok = t.shape[0]
    n = n_tok * TOP_K
    flat_e = idx.reshape(n)
    flat_t = jnp.repeat(jnp.arange(n_tok, dtype=jnp.int32), TOP_K)
    flat_w = wts.reshape(n)
    order = jnp.argsort(flat_e)
    e_sorted = flat_e[order]
    counts = jnp.bincount(flat_e, length=N_EXPERTS)
    padded = (counts + EXPERT_BLOCK - 1) // EXPERT_BLOCK * EXPERT_BLOCK
    pad_end = jnp.cumsum(padded)
    pad_start = pad_end - padded
    start = jnp.cumsum(counts) - counts
    dest = pad_start[e_sorted] + jnp.arange(n) - start[e_sorted]
    n_blocks = -(-n // EXPERT_BLOCK) + N_EXPERTS
    rows = n_blocks * EXPERT_BLOCK
    row_tok = jnp.zeros((rows,), jnp.int32).at[dest].set(flat_t[order])
    row_w = jnp.zeros((rows,), jnp.float32).at[dest].set(flat_w[order])
    blk_e = jnp.minimum(jnp.searchsorted(pad_end, jnp.arange(n_blocks) * EXPERT_BLOCK, side='right'),
                        N_EXPERTS - 1)

    def block(acc, xs):
        tok, w, e = xs
        xb = t[tok]
        yb = (jax.nn.silu(xb @ w_gate[e]) * (xb @ w_up[e])) @ w_down[e]
        return acc.at[tok].add((yb * w[:, None]).astype(acc.dtype)), None

    acc, _ = lax.scan(block, jnp.zeros_like(t),
                      (row_tok.reshape(n_blocks, EXPERT_BLOCK), row_w.reshape(n_blocks, EXPERT_BLOCK), blk_e))
    return acc


def moe(h, w_router, b_router, w_exp_gate, w_exp_up, w_exp_down, w_sh_gate, w_sh_up, w_sh_down):
    bsz, length, d = h.shape
    t = h.reshape(bsz * length, d)
    idx, wts = route(t, w_router, b_router)
    routed = routed_experts(t, idx, wts, w_exp_gate, w_exp_up, w_exp_down)
    shared = (jax.nn.silu(t @ w_sh_gate) * (t @ w_sh_up)) @ w_sh_down
    return (routed + shared).reshape(bsz, length, d)


def trunk_layer(x, c, gla_s0, fox_past, pos0, gla_block, lw):
    (w_ada, b_ada, g_mix, g_ffn, w_in, w_gk2, b_gk, g_gla, b_f, g_fox, w_out,
     w_r, b_r, w_eg, w_eu, w_ed, w_sg, w_su, w_sd) = lw
    sh1, sc1, gt1, sh2, sc2, gt2 = ada_modulation(c, w_ada, b_ada)
    h = rms_norm(x, g_mix) * (1.0 + sc1) + sh1
    mix, s_new, k, v, logf = token_mixer(h, gla_s0, fox_past, pos0, gla_block, w_in, w_gk2, b_gk,
                                         g_gla, b_f, g_fox, w_out)
    x = x + gt1 * mix
    h = rms_norm(x, g_ffn) * (1.0 + sc2) + sh2
    x = x + gt2 * moe(h, w_r, b_r, w_eg, w_eu, w_ed, w_sg, w_su, w_sd)
    return x, s_new, k, v, logf


def setup_inputs(seed: int = 0) -> dict:
    key = jax.random.key(seed)
    ks = jax.random.split(key, 32)

    def nrm(k, shape, s):
        return jax.random.normal(k, shape, jnp.float32) * s

    nl, d = DEPTH, D_MODEL
    return {
        'x_prompt': nrm(ks[0], (BATCH, SEQ, d), 1.0),
        'x_sample': nrm(ks[1], (DEC_BATCH, DEC_SEQ, d), 1.0),
        'state_gla': nrm(ks[2], (nl, DEC_BATCH, GLA_HEADS, GLA_HEAD_K, GLA_HEAD_V), 1.0),
        'cache_fox_k': nrm(ks[3], (nl, DEC_BATCH, PAST_LEN, FOX_HEADS, FOX_HEAD_DIM), 1.0),
        'cache_fox_v': nrm(ks[4], (nl, DEC_BATCH, PAST_LEN, FOX_HEADS, FOX_HEAD_DIM), 1.0),
        'cache_fox_logf': jax.nn.log_sigmoid(3.0 + nrm(ks[5], (nl, DEC_BATCH, PAST_LEN, FOX_HEADS), 1.0)),
        'c_prompt': nrm(ks[6], (BATCH, d), 1.0),
        'c_sample': nrm(ks[7], (DEC_BATCH, d), 1.0),
        'w_ada': nrm(ks[8], (nl, d, 6 * d), 0.5 * d ** -0.5),
        'b_ada': nrm(ks[9], (nl, 6 * d), 0.02),
        'g_norm_mix': 1.0 + nrm(ks[10], (nl, d), 0.05),
        'g_norm_ffn': 1.0 + nrm(ks[11], (nl, d), 0.05),
        'w_in': nrm(ks[12], (nl, d, IN_WIDTH), d ** -0.5),
        'w_gla_gk2': nrm(ks[13], (nl, GLA_LOW_RANK, GLA_KEY_WIDTH), GLA_LOW_RANK ** -0.5),
        'b_gla_gk': nrm(ks[14], (nl, GLA_KEY_WIDTH), 0.02),
        'g_gla_out': 1.0 + nrm(ks[15], (nl, GLA_HEAD_V), 0.05),
        'b_fox_f': jax.random.uniform(ks[16], (nl, FOX_HEADS), jnp.float32, 1.0, 5.0),
        'g_fox_out': 1.0 + nrm(ks[17], (nl, FOX_WIDTH), 0.05),
        'w_out': nrm(ks[18], (nl, MIX_WIDTH, d), MIX_WIDTH ** -0.5),
        'w_router': nrm(ks[19], (nl, d, N_EXPERTS), d ** -0.5),
        'b_router': nrm(ks[20], (nl, N_EXPERTS), 0.01),
        'w_exp_gate': nrm(ks[21], (nl, N_EXPERTS, d, D_EXPERT), d ** -0.5),
        'w_exp_up': nrm(ks[22], (nl, N_EXPERTS, d, D_EXPERT), d ** -0.5),
        'w_exp_down': nrm(ks[23], (nl, N_EXPERTS, D_EXPERT, d), D_EXPERT ** -0.5),
        'w_sh_gate': nrm(ks[24], (nl, d, D_SHARED), d ** -0.5),
        'w_sh_up': nrm(ks[25], (nl, d, D_SHARED), d ** -0.5),
        'w_sh_down': nrm(ks[26], (nl, D_SHARED, d), D_SHARED ** -0.5),
        'g_final': 1.0 + nrm(ks[27], (d,), 0.05),
    }


def reference(x_prompt, x_sample, state_gla, cache_fox_k, cache_fox_v, cache_fox_logf, c_prompt, c_sample,
              w_ada, b_ada, g_norm_mix, g_norm_ffn, w_in, w_gla_gk2, b_gla_gk, g_gla_out, b_fox_f, g_fox_out,
              w_out, w_router, b_router, w_exp_gate, w_exp_up, w_exp_down, w_sh_gate, w_sh_up, w_sh_down,
              g_final):
    xp, xs = x_prompt, x_sample
    gla_p, gla_s, kp_l, vp_l, fp_l, ks_l, vs_l, fs_l = [], [], [], [], [], [], [], []
    for l in range(DEPTH):
        lw = (w_ada[l], b_ada[l], g_norm_mix[l], g_norm_ffn[l], w_in[l], w_gla_gk2[l], b_gla_gk[l],
              g_gla_out[l], b_fox_f[l], g_fox_out[l], w_out[l], w_router[l], b_router[l],
              w_exp_gate[l], w_exp_up[l], w_exp_down[l], w_sh_gate[l], w_sh_up[l], w_sh_down[l])
        s0 = jnp.zeros((xp.shape[0], GLA_HEADS, GLA_HEAD_K, GLA_HEAD_V), jnp.float32)
        xp, sp, kp, vp, fp = trunk_layer(xp, c_prompt, s0, None, 0, CHUNK, lw)
        xs, ss, k_s, v_s, f_s = trunk_layer(xs, c_sample, state_gla[l],
                                            (cache_fox_k[l], cache_fox_v[l], cache_fox_logf[l]),
                                            cache_fox_k.shape[2], xs.shape[1], lw)
        gla_p.append(sp)
        gla_s.append(ss)
        kp_l.append(kp)
        vp_l.append(vp)
        fp_l.append(fp)
        ks_l.append(k_s)
        vs_l.append(v_s)
        fs_l.append(f_s)
    y_prompt = rms_norm(xp, g_final)
    y_sample = rms_norm(xs, g_final)
    return (y_prompt, y_sample, jnp.stack(gla_p), jnp.stack(gla_s), jnp.stack(kp_l), jnp.stack(vp_l),
            jnp.stack(fp_l), jnp.stack(ks_l), jnp.stack(vs_l), jnp.stack(fs_l))
```

```python
import functools

import jax
import jax.numpy as jnp
from jax import lax
from jax.experimental import pallas as pl
from jax.experimental.pallas import tpu as pltpu

f32, bf16, i32, u32 = jnp.float32, jnp.bfloat16, jnp.int32, jnp.uint32
HIGHEST = lax.Precision.HIGHEST

D_MODEL = 1024
GLA_HEADS, GLA_HEAD_K, GLA_HEAD_V = 4, 64, 128
GLA_KEY_WIDTH, GLA_VALUE_WIDTH, GLA_LOW_RANK = 256, 512, 16
GLA_GATE_NORMALIZER = 16.0
GLA_CHUNK = 64
FOX_HEADS, FOX_HEAD_DIM, FOX_WIDTH = 8, 64, 512
N_EXPERTS, N_GROUPS, TOPK_GROUPS, TOP_K = 64, 8, 4, 8
D_EXPERT = 256
ROUTED_SCALE = 2.5
NORM_EPS = 1e-6
HALF = D_MODEL // 2

EXPERT_ROWS = 512
VMEM_LIMIT = 56 * 1024 * 1024


def _cparams(sem):
    return pltpu.CompilerParams(dimension_semantics=sem, vmem_limit_bytes=VMEM_LIMIT)


def _log_sigmoid(x):
    return jnp.minimum(x, 0.0) - jnp.log1p(jnp.exp(-jnp.abs(x)))


def _silu(x):
    return x * jax.nn.sigmoid(x)


def _pack_halves(x):
    xb = x.astype(bf16).astype(f32)
    lo = lax.bitcast_convert_type(xb[:, :HALF], u32) >> 16
    hi = lax.bitcast_convert_type(xb[:, HALF:], u32) & jnp.uint32(0xFFFF0000)
    return lo | hi


def _unpack_halves(w):
    lo = lax.bitcast_convert_type(w << 16, f32)
    hi = lax.bitcast_convert_type(w & jnp.uint32(0xFFFF0000), f32)
    return lo, hi


def _ada_body(c_ref, w_ref, b_ref, o_ref):
    a = _silu(c_ref[...]).astype(bf16)
    o_ref[...] = jnp.dot(a, w_ref[...].astype(bf16), preferred_element_type=f32) + b_ref[...]


def _ada(c_all, w_ada, b_ada):
    nb, d = c_all.shape
    n = w_ada.shape[1]
    tn = 768
    return pl.pallas_call(
        _ada_body,
        grid=(n // tn,),
        in_specs=[pl.BlockSpec((nb, d), lambda j: (0, 0)),
                  pl.BlockSpec((d, tn), lambda j: (0, j)),
                  pl.BlockSpec((1, tn), lambda j: (0, j))],
        out_specs=pl.BlockSpec((nb, tn), lambda j: (0, j)),
        out_shape=jax.ShapeDtypeStruct((nb, n), f32),
        compiler_params=_cparams(("arbitrary",)),
        name="ada",
    )(c_all, w_ada, b_ada.reshape(1, n))


_C_GQ, _C_GK, _C_GV, _C_GG, _C_FQ, _C_FK, _C_FV, _C_SM, _C_END = 0, 256, 512, 1024, 1536, 2048, 2560, 3072, 3200


def _inproj_body(x_ref, sh_ref, sc_ref, g_ref, w_ref, wgk_ref, bgk_ref, bf_ref,
                 gq_ref, gk_ref, gv_ref, gg_ref, la_ref, fq_ref, fk_ref, fv_ref, fkb_ref, fvb_ref, lf_ref,
                 *, nb, rpb):
    tm = nb * rpb
    x = x_ref[...]
    y = x * lax.rsqrt(jnp.mean(x * x, axis=-1, keepdims=True) + NORM_EPS) * g_ref[...]
    h = (y.reshape(nb, rpb, D_MODEL) * (1.0 + sc_ref[...]) + sh_ref[...]).reshape(tm, D_MODEL).astype(bf16)

    def proj(a, b):
        return jnp.dot(h, w_ref[:, a:b], preferred_element_type=f32)

    gq_ref[...] = proj(_C_GQ, _C_GK) * (GLA_HEAD_K ** -0.5)
    gk_ref[...] = proj(_C_GK, _C_GV)
    gv_ref[...] = proj(_C_GV, _C_GG)
    gg_ref[...] = proj(_C_GG, _C_FQ)
    fq_ref[...] = (proj(_C_FQ, _C_FK) * (FOX_HEAD_DIM ** -0.5)).astype(bf16)
    fk = proj(_C_FK, _C_FV)
    fk_ref[...] = fk
    fkb_ref[...] = fk.astype(bf16)
    fv = proj(_C_FV, _C_SM)
    fv_ref[...] = fv
    fvb_ref[...] = fv.astype(bf16)
    zs = proj(_C_SM, _C_END)
    ga = jnp.dot(zs.astype(bf16), wgk_ref[...], preferred_element_type=f32) + bgk_ref[...]
    la_ref[...] = _log_sigmoid(ga) * (1.0 / GLA_GATE_NORMALIZER)
    lf_ref[...] = _log_sigmoid(zs + bf_ref[...])[:, :FOX_HEADS]


def _inproj(x2, m3, g_mix, w_cat, wgk_pad, bgk, bf_pad, *, nb, rpb, tiles_per_batch):
    t, d = x2.shape
    tm = nb * rpb
    row = lambda w: pl.BlockSpec((tm, w), lambda i: (i, 0))
    mod = lambda c: pl.BlockSpec((nb, 1, d), lambda i, c=c: (i // tiles_per_batch, 0, c))
    full = lambda a: pl.BlockSpec(a.shape, lambda i: (0,) * a.ndim)
    widths = [(GLA_KEY_WIDTH, f32), (GLA_KEY_WIDTH, f32), (GLA_VALUE_WIDTH, f32), (GLA_VALUE_WIDTH, f32),
              (GLA_KEY_WIDTH, f32), (FOX_WIDTH, bf16), (FOX_WIDTH, f32), (FOX_WIDTH, f32),
              (FOX_WIDTH, bf16), (FOX_WIDTH, bf16), (FOX_HEADS, f32)]
    return pl.pallas_call(
        functools.partial(_inproj_body, nb=nb, rpb=rpb),
        grid=(t // tm,),
        in_specs=[row(d), mod(0), mod(1), full(g_mix), full(w_cat), full(wgk_pad), full(bgk), full(bf_pad)],
        out_specs=[row(w) for w, _ in widths],
        out_shape=[jax.ShapeDtypeStruct((t, w), dt) for w, dt in widths],
        compiler_params=_cparams(("arbitrary",)),
        name="inproj",
    )(x2, m3, m3, g_mix, w_cat, wgk_pad, bgk, bf_pad)


def _cumsum_body(x_ref, o_ref, *, chunk, suffix):
    n = x_ref.shape[-1]
    r = lax.broadcasted_iota(i32, (chunk, chunk), 0)
    c = lax.broadcasted_iota(i32, (chunk, chunk), 1)
    tri = (r <= c).astype(f32)
    carry = jnp.zeros((x_ref.shape[1], 1), f32)
    for j in range(n // chunk):
        sl = slice(j * chunk, (j + 1) * chunk)
        cc = jnp.dot(x_ref[0, :, sl], tri, precision=HIGHEST, preferred_element_type=f32) + carry
        o_ref[0, :, sl] = cc
        carry = cc[:, chunk - 1:chunk]
    if suffix:
        o_ref[0] = o_ref[0] - carry


def _cumsum_lanes(x, *, suffix=False):
    b, h, n = x.shape
    return pl.pallas_call(
        functools.partial(_cumsum_body, chunk=128, suffix=suffix),
        grid=(b,),
        in_specs=[pl.BlockSpec((1, h, n), lambda i: (i, 0, 0))],
        out_specs=pl.BlockSpec((1, h, n), lambda i: (i, 0, 0)),
        out_shape=jax.ShapeDtypeStruct((b, h, n), f32),
        compiler_params=_cparams(("arbitrary",)),
        name="cumsum",
    )(x)


def _gla_body(q_ref, k_ref, v_ref, la_ref, s0_ref, o_ref, sout_ref, s_scr, *, chunk, n_chunks):
    ci = pl.program_id(1)

    @pl.when(ci == 0)
    def _():
        s_scr[...] = s0_ref[0]

    g = la_ref[...]
    r = lax.broadcasted_iota(i32, (chunk, chunk), 0)
    c = lax.broadcasted_iota(i32, (chunk, chunk), 1)
    causal = c <= r
    b = jnp.dot(causal.astype(f32), g, precision=HIGHEST, preferred_element_type=f32)
    mid = chunk // 2
    b_ref = b[mid:mid + 1, :]
    b_last = b[chunk - 1:chunk, :]
    q = q_ref[...]
    k = k_ref[...]
    qs = (q * jnp.exp(b - b_ref)).astype(bf16)
    ks = (k * jnp.exp(b_ref - b)).astype(bf16)
    qe = (q * jnp.exp(b)).astype(bf16)
    kd = (k * jnp.exp(b_last - b)).astype(bf16)
    decay = jnp.exp(lax.dot_general(g, jnp.ones((chunk, GLA_HEAD_V), f32), (((0,), (0,)), ((), ())),
                                    precision=HIGHEST, preferred_element_type=f32))
    lane_head = lax.broadcasted_iota(i32, (1, GLA_KEY_WIDTH), 1) // GLA_HEAD_K
    zero = jnp.zeros((), bf16)
    s_old = s_scr[...].reshape(GLA_KEY_WIDTH, GLA_HEAD_V)
    s_old_b = s_old.astype(bf16)
    for h in range(GLA_HEADS):
        hm = lane_head == h
        rows = slice(h * GLA_HEAD_K, (h + 1) * GLA_HEAD_K)
        cols = slice(h * GLA_HEAD_V, (h + 1) * GLA_HEAD_V)
        vh = v_ref[:, cols].astype(bf16)
        a = lax.dot_general(jnp.where(hm, qs, zero), ks, (((1,), (1,)), ((), ())), preferred_element_type=f32)
        a = jnp.where(causal, a, 0.0).astype(bf16)
        o_ref[:, cols] = (jnp.dot(jnp.where(hm, qe, zero), s_old_b, preferred_element_type=f32)
                          + jnp.dot(a, vh, preferred_element_type=f32))
        upd = lax.dot_general(kd, vh, (((0,), (0,)), ((), ())), preferred_element_type=f32)
        s_scr[h] = s_old[rows, :] * decay[rows, :] + upd[rows, :]

    @pl.when(ci == n_chunks - 1)
    def _():
        sout_ref[0] = s_scr[...]


def _gla(gq, gk, gv, la, s0, *, bsz, length, chunk):
    n_chunks = length // chunk
    row = lambda w: pl.BlockSpec((chunk, w), lambda b, c: (b * n_chunks + c, 0))
    st = pl.BlockSpec((1, GLA_HEADS, GLA_HEAD_K, GLA_HEAD_V), lambda b, c: (b, 0, 0, 0))
    return pl.pallas_call(
        functools.partial(_gla_body, chunk=chunk, n_chunks=n_chunks),
        grid=(bsz, n_chunks),
        in_specs=[row(GLA_KEY_WIDTH), row(GLA_KEY_WIDTH), row(GLA_VALUE_WIDTH), row(GLA_KEY_WIDTH), st],
        out_specs=[row(GLA_VALUE_WIDTH), st],
        out_shape=[jax.ShapeDtypeStruct((bsz * length, GLA_VALUE_WIDTH), f32),
                   jax.ShapeDtypeStruct((bsz, GLA_HEADS, GLA_HEAD_K, GLA_HEAD_V), f32)],
        scratch_shapes=[pltpu.VMEM((GLA_HEADS, GLA_HEAD_K, GLA_HEAD_V), f32)],
        compiler_params=_cparams(("arbitrary", "arbitrary")),
        name="gla",
    )(gq, gk, gv, la, s0)


_HG = 4


def _foxp_body(q_ref, k_ref, v_ref, cq_ref, ck_ref, o_ref, m_scr, l_scr, acc_scr, *, tq, tk):
    qi, ki = pl.program_id(2), pl.program_id(3)
    last_k = (qi * tq + tq - 1) // tk

    @pl.when(ki == 0)
    def _():
        m_scr[...] = jnp.full(m_scr.shape, -jnp.inf, f32)
        l_scr[...] = jnp.zeros(l_scr.shape, f32)
        acc_scr[...] = jnp.zeros(acc_scr.shape, f32)

    @pl.when(ki <= last_k)
    def _():
        q = q_ref[...]
        k = k_ref[...]
        v = v_ref[...]
        lane_head = lax.broadcasted_iota(i32, (1, _HG * FOX_HEAD_DIM), 1) // FOX_HEAD_DIM
        qpos = qi * tq + lax.broadcasted_iota(i32, (tq, tk), 0)
        kpos = ki * tk + lax.broadcasted_iota(i32, (tq, tk), 1)
        visible = kpos <= qpos
        zero = jnp.zeros((), bf16)
        for j in range(_HG):
            hm = lane_head == j
            s = lax.dot_general(jnp.where(hm, q, zero), k, (((1,), (1,)), ((), ())), preferred_element_type=f32)
            s = s + cq_ref[0, 0, :, j:j + 1] - ck_ref[0, 0, j:j + 1, :]
            s = jnp.where(visible, s, -jnp.inf)
            m_old = m_scr[j]
            m_new = jnp.maximum(m_old, jnp.max(s, axis=-1, keepdims=True))
            alpha = jnp.exp(m_old - m_new)
            p = jnp.exp(s - m_new)
            l_scr[j] = alpha * l_scr[j] + jnp.sum(p, axis=-1, keepdims=True)
            m_scr[j] = m_new
            pv = jnp.dot(p.astype(bf16), v, preferred_element_type=f32)
            acc = acc_scr[...]
            acc_scr[...] = jnp.where(hm, alpha * acc + pv, acc)

    @pl.when(ki == pl.num_programs(3) - 1)
    def _():
        lane_head = lax.broadcasted_iota(i32, (1, _HG * FOX_HEAD_DIM), 1) // FOX_HEAD_DIM
        inv = jnp.zeros((tq, _HG * FOX_HEAD_DIM), f32)
        for j in range(_HG):
            inv = jnp.where(lane_head == j, 1.0 / l_scr[j], inv)
        o_ref[...] = acc_scr[...] * inv


def _fox_prompt(fq, fkb, fvb, c_rows, c_cols, *, bsz, length, tq=512, tk=512):
    nq, nk = length // tq, length // tk
    w = _HG * FOX_HEAD_DIM
    kmap = lambda b, g, qi, ki: (b * nk + jnp.minimum(ki, (qi * tq + tq - 1) // tk), g)
    return pl.pallas_call(
        functools.partial(_foxp_body, tq=tq, tk=tk),
        grid=(bsz, FOX_HEADS // _HG, nq, nk),
        in_specs=[pl.BlockSpec((tq, w), lambda b, g, qi, ki: (b * nq + qi, g)),
                  pl.BlockSpec((tk, w), kmap),
                  pl.BlockSpec((tk, w), kmap),
                  pl.BlockSpec((1, 1, tq, _HG), lambda b, g, qi, ki: (b, g, qi, 0)),
                  pl.BlockSpec((1, 1, _HG, tk),
                               lambda b, g, qi, ki: (b, g, 0, jnp.minimum(ki, (qi * tq + tq - 1) // tk)))],
        out_specs=pl.BlockSpec((tq, w), lambda b, g, qi, ki: (b * nq + qi, g)),
        out_shape=jax.ShapeDtypeStruct((bsz * length, FOX_WIDTH), f32),
        scratch_shapes=[pltpu.VMEM((_HG, tq, 1), f32), pltpu.VMEM((_HG, tq, 1), f32), pltpu.VMEM((tq, w), f32)],
        compiler_params=_cparams(("arbitrary",) * 4),
        name="fox_prompt",
    )(fq, fkb, fvb, c_rows, c_cols)


def _foxs_body(kc_ref, vc_ref, kn_ref, vn_ref, qbd_ref, ckp_ref, ckn_ref, cq_ref, e_ref, o_ref, s_scr,
               *, past, new, kchunk):
    nlane = FOX_HEADS * new
    qbd = qbd_ref[0]
    cq = cq_ref[0]
    n_chunks = past // kchunk
    m = jnp.full((1, nlane), -jnp.inf, f32)
    for c in range(n_chunks):
        rows = slice(c * kchunk, (c + 1) * kchunk)
        s = jnp.dot(kc_ref[0, rows, :].astype(bf16), qbd, preferred_element_type=f32)
        ck = jnp.dot(ckp_ref[0, rows, :], e_ref[...], precision=HIGHEST, preferred_element_type=f32)
        s = s + cq - ck
        s_scr[rows, :] = s
        m = jnp.maximum(m, jnp.max(s, axis=0, keepdims=True))
    sn = jnp.dot(kn_ref[0].astype(bf16), qbd, preferred_element_type=f32)
    ckn = jnp.dot(ckn_ref[0], e_ref[...], precision=HIGHEST, preferred_element_type=f32)
    sn = sn + cq - ckn
    key_j = lax.broadcasted_iota(i32, (new, nlane), 0)
    qry_i = lax.broadcasted_iota(i32, (new, nlane), 1) % new
    sn = jnp.where(key_j <= qry_i, sn, -jnp.inf)
    m = jnp.maximum(m, jnp.max(sn, axis=0, keepdims=True))
    pn = jnp.exp(sn - m)
    l = jnp.sum(pn, axis=0, keepdims=True)
    for c in range(n_chunks):
        rows = slice(c * kchunk, (c + 1) * kchunk)
        p = jnp.exp(s_scr[rows, :] - m)
        s_scr[rows, :] = p
        l = l + jnp.sum(p, axis=0, keepdims=True)
    inv = 1.0 / l
    tn = (((0,), (0,)), ((), ()))
    acc = lax.dot_general((pn * inv).astype(bf16), vn_ref[0].astype(bf16), tn, preferred_element_type=f32)
    for c in range(n_chunks):
        rows = slice(c * kchunk, (c + 1) * kchunk)
        p = (s_scr[rows, :] * inv).astype(bf16)
        acc = acc + lax.dot_general(p, vc_ref[0, rows, :].astype(bf16), tn, preferred_element_type=f32)
    for h in range(FOX_HEADS):
        o_ref[0, :, h * FOX_HEAD_DIM:(h + 1) * FOX_HEAD_DIM] = (
            acc[h * new:(h + 1) * new, h * FOX_HEAD_DIM:(h + 1) * FOX_HEAD_DIM])


def _fox_sample(kc, vc, kn, vn, qbd, ckp, ckn, cq, e_mat, *, kchunk=512):
    bsz, past, w = kc.shape
    new = kn.shape[1]
    nlane = FOX_HEADS * new
    blk = lambda a: pl.BlockSpec((1,) + a.shape[1:], lambda b: (b,) + (0,) * (a.ndim - 1))
    return pl.pallas_call(
        functools.partial(_foxs_body, past=past, new=new, kchunk=kchunk),
        grid=(bsz,),
        in_specs=[blk(kc), blk(vc), blk(kn), blk(vn), blk(qbd), blk(ckp), blk(ckn), blk(cq),
                  pl.BlockSpec(e_mat.shape, lambda b: (0, 0))],
        out_specs=pl.BlockSpec((1, new, w), lambda b: (b, 0, 0)),
        out_shape=jax.ShapeDtypeStruct((bsz, new, w), f32),
        scratch_shapes=[pltpu.VMEM((past, nlane), f32)],
        compiler_params=_cparams(("arbitrary",)),
        name="fox_sample",
    )(kc, vc, kn, vn, qbd, ckp, ckn, cq, e_mat)


def _route(logits_t, b_router):
    ne, tm = logits_t.shape
    gsz = ne // N_GROUPS
    shape3 = (N_GROUPS, gsz, tm)
    scores = jax.nn.sigmoid(logits_t).reshape(shape3)
    choice = scores + b_router
    im = lax.broadcasted_iota(i32, shape3, 1)
    ig = lax.broadcasted_iota(i32, (N_GROUPS, 1, tm), 0)
    ie = lax.broadcasted_iota(i32, shape3, 0) * gsz + im
    neg = -jnp.inf
    m1 = jnp.max(choice, axis=1, keepdims=True)
    first = jnp.min(jnp.where(choice == m1, im, gsz), axis=1, keepdims=True)
    m2 = jnp.max(jnp.where(im == first, neg, choice), axis=1, keepdims=True)
    cur = m1 + m2
    gsel = jnp.zeros(cur.shape, i32)
    for _ in range(TOPK_GROUPS):
        mx = jnp.max(cur, axis=0, keepdims=True)
        hit = ig == jnp.min(jnp.where(cur == mx, ig, N_GROUPS), axis=0, keepdims=True)
        gsel = jnp.where(hit, 1, gsel)
        cur = jnp.where(hit, neg, cur)
    cur = jnp.where(jnp.broadcast_to(gsel, shape3) > 0, choice, neg)
    ids, ws = [], []
    for _ in range(TOP_K):
        mx = jnp.max(jnp.max(cur, axis=1, keepdims=True), axis=0, keepdims=True)
        cand = jnp.where(cur == mx, ie, ne)
        fi = jnp.min(jnp.min(cand, axis=1, keepdims=True), axis=0, keepdims=True)
        hit = ie == fi
        w = jnp.where(hit, scores, 0.0)
        ids.append(fi.reshape(1, tm))
        ws.append(jnp.sum(jnp.sum(w, axis=1, keepdims=True), axis=0, keepdims=True).reshape(1, tm))
        cur = jnp.where(hit, neg, cur)
    tot = ws[0]
    for w in ws[1:]:
        tot = tot + w
    scale = ROUTED_SCALE / (tot + 1e-20)
    return ids, [w * scale for w in ws]


def _outproj_body(og_ref, gg_ref, of_ref, x_ref, gt1_ref, sh2_ref, sc2_ref, ggla_ref, gfox_ref, gffn_ref,
                  wout_ref, wr_ref, br_ref, x1_ref, hp_ref, idx_ref, wts_ref, *, nb, rpb):
    tm = nb * rpb
    og = og_ref[...]
    parts = []
    for h in range(GLA_HEADS):
        seg = og[:, h * GLA_HEAD_V:(h + 1) * GLA_HEAD_V]
        parts.append(seg * lax.rsqrt(jnp.mean(seg * seg, axis=-1, keepdims=True) + NORM_EPS))
    gla_n = jnp.concatenate(parts, axis=-1) * ggla_ref[...] * _silu(gg_ref[...])
    of = of_ref[...]
    fox_n = of * lax.rsqrt(jnp.mean(of * of, axis=-1, keepdims=True) + NORM_EPS) * gfox_ref[...]
    merged = jnp.concatenate([gla_n, fox_n], axis=-1).astype(bf16)
    mix = jnp.dot(merged, wout_ref[...], preferred_element_type=f32)
    x1 = (x_ref[...].reshape(nb, rpb, D_MODEL) + gt1_ref[...] * mix.reshape(nb, rpb, D_MODEL))
    x1_ref[...] = x1.reshape(tm, D_MODEL)
    x1f = x1.reshape(tm, D_MODEL)
    y = x1f * lax.rsqrt(jnp.mean(x1f * x1f, axis=-1, keepdims=True) + NORM_EPS) * gffn_ref[...]
    h2 = (y.reshape(nb, rpb, D_MODEL) * (1.0 + sc2_ref[...]) + sh2_ref[...]).reshape(tm, D_MODEL)
    hp_ref[...] = _pack_halves(h2)
    logits_t = lax.dot_general(wr_ref[...], h2, (((1,), (1,)), ((), ())), precision=HIGHEST,
                               preferred_element_type=f32)
    ids, ws = _route(logits_t, br_ref[...])
    for j in range(TOP_K):
        idx_ref[j:j + 1, :] = ids[j]
        wts_ref[j:j + 1, :] = ws[j]


def _outproj(og, gg, of, x2, m3, g_gla4, g_fox, g_ffn, w_out, w_rt, b_r, *, nb, rpb, tiles_per_batch):
    t, d = x2.shape
    tm = nb * rpb
    row = lambda w: pl.BlockSpec((tm, w), lambda i: (i, 0))
    col = pl.BlockSpec((TOP_K, tm), lambda i: (0, i))
    mod = lambda c: pl.BlockSpec((nb, 1, d), lambda i, c=c: (i // tiles_per_batch, 0, c))
    full = lambda a: pl.BlockSpec(a.shape, lambda i: (0,) * a.ndim)
    return pl.pallas_call(
        functools.partial(_outproj_body, nb=nb, rpb=rpb),
        grid=(t // tm,),
        in_specs=[row(GLA_VALUE_WIDTH), row(GLA_VALUE_WIDTH), row(FOX_WIDTH), row(d), mod(2), mod(3), mod(4),
                  full(g_gla4), full(g_fox), full(g_ffn), full(w_out), full(w_rt), full(b_r)],
        out_specs=[row(d), row(HALF), col, col],
        out_shape=[jax.ShapeDtypeStruct((t, d), f32), jax.ShapeDtypeStruct((t, HALF), u32),
                   jax.ShapeDtypeStruct((TOP_K, t), i32), jax.ShapeDtypeStruct((TOP_K, t), f32)],
        compiler_params=_cparams(("arbitrary",)),
        name="outproj",
    )(og, gg, of, x2, m3, m3, m3, g_gla4, g_fox, g_ffn, w_out, w_rt, b_r)


def _rank_body(idx_ref, dest_ref, cnt_ref, cnt_scr, base_scr, run_scr, *, tn, block_rows):
    ph, ti = pl.program_id(0), pl.program_id(1)
    ie = lax.broadcasted_iota(i32, (N_EXPERTS, tn), 0)
    idx = idx_ref[...]
    multihot = jnp.zeros((N_EXPERTS, tn), f32)
    for j in range(TOP_K):
        multihot = multihot + (ie == idx[j:j + 1, :]).astype(f32)

    @pl.when(jnp.logical_and(ph == 0, ti == 0))
    def _():
        cnt_scr[...] = jnp.zeros(cnt_scr.shape, f32)

    @pl.when(ph == 0)
    def _():
        cnt_scr[...] += jnp.sum(multihot, axis=1, keepdims=True)

    @pl.when(jnp.logical_and(ph == 1, ti == 0))
    def _():
        cnt = cnt_scr[...]
        padded = jnp.ceil(cnt * (1.0 / block_rows)) * block_rows
        r = lax.broadcasted_iota(i32, (N_EXPERTS, N_EXPERTS), 0)
        c = lax.broadcasted_iota(i32, (N_EXPERTS, N_EXPERTS), 1)
        strict = (c < r).astype(f32)
        base_scr[...] = jnp.dot(strict, jnp.broadcast_to(padded, (N_EXPERTS, 128)), precision=HIGHEST,
                                preferred_element_type=f32)[:, :1]
        run_scr[...] = jnp.zeros(run_scr.shape, f32)
        cnt_ref[...] = cnt

    @pl.when(ph == 1)
    def _():
        r = lax.broadcasted_iota(i32, (tn, tn), 0)
        c = lax.broadcasted_iota(i32, (tn, tn), 1)
        before = (r < c).astype(bf16)
        excl = jnp.dot(multihot.astype(bf16), before, preferred_element_type=f32)
        pos = base_scr[...] + run_scr[...] + excl
        for j in range(TOP_K):
            dest_ref[j:j + 1, :] = jnp.sum(jnp.where(ie == idx[j:j + 1, :], pos, 0.0), axis=0,
                                           keepdims=True).astype(i32)
        run_scr[...] += jnp.sum(multihot, axis=1, keepdims=True)


def _rank(idx, *, block_rows, tn=512):
    _, t = idx.shape
    return pl.pallas_call(
        functools.partial(_rank_body, tn=tn, block_rows=block_rows),
        grid=(2, t // tn),
        in_specs=[pl.BlockSpec((TOP_K, tn), lambda p, i: (0, i))],
        out_specs=[pl.BlockSpec((TOP_K, tn), lambda p, i: (0, i * p)),
                   pl.BlockSpec((N_EXPERTS, 1), lambda p, i: (0, 0))],
        out_shape=[jax.ShapeDtypeStruct((TOP_K, t), i32), jax.ShapeDtypeStruct((N_EXPERTS, 1), f32)],
        scratch_shapes=[pltpu.VMEM((N_EXPERTS, 1), f32)] * 3,
        compiler_params=_cparams(("arbitrary", "arbitrary")),
        name="rank",
    )(idx)


def _dispatch_body(dest_hbm, hp_ref, *rest, tm, aliased):
    xs_hbm, dest_smem, sem_idx, sem_rows = rest[-4:]
    i = pl.program_id(0)
    cp = pltpu.make_async_copy(dest_hbm.at[i], dest_smem, sem_idx)
    cp.start()
    cp.wait()

    def issue(t, carry):
        for j in range(TOP_K):
            pltpu.make_async_copy(hp_ref.at[pl.ds(t, 1), :], xs_hbm.at[pl.ds(dest_smem[j, t], 1), :],
                                  sem_rows).start()
        return carry

    lax.fori_loop(0, tm, issue, 0)
    for j in range(TOP_K):
        pltpu.make_async_copy(hp_ref, xs_hbm.at[pl.ds(0, tm), :], sem_rows).wait()


def _dispatch(dest_tiles, hp, xs_prev, *, n_rows, tm, tile0):
    t = hp.shape[0]
    aliased = xs_prev is not None
    in_specs = [pl.BlockSpec(memory_space=pl.ANY), pl.BlockSpec((tm, HALF), lambda i: (i, 0))]
    args = [dest_tiles[tile0:tile0 + t // tm], hp]
    if aliased:
        in_specs.append(pl.BlockSpec(memory_space=pl.ANY))
        args.append(xs_prev)
    return pl.pallas_call(
        functools.partial(_dispatch_body, tm=tm, aliased=aliased),
        grid=(t // tm,),
        in_specs=in_specs,
        out_specs=pl.BlockSpec(memory_space=pl.ANY),
        out_shape=jax.ShapeDtypeStruct((n_rows, HALF), u32),
        scratch_shapes=[pltpu.SMEM((TOP_K, tm), i32), pltpu.SemaphoreType.DMA, pltpu.SemaphoreType.DMA],
        input_output_aliases={2: 0} if aliased else {},
        compiler_params=_cparams(("arbitrary",)),
        name="dispatch",
    )(*args)


def _experts_body(be_ref, nu_ref, xs_ref, wg_ref, wu_ref, wd_ref, y_ref, wg_s, wu_s, wd_s):
    i = pl.program_id(0)
    changed = jnp.logical_or(i == 0, be_ref[i] != be_ref[jnp.maximum(i - 1, 0)])

    @pl.when(changed)
    def _():
        wg_s[...] = wg_ref[0].astype(bf16)
        wu_s[...] = wu_ref[0].astype(bf16)
        wd_s[...] = wd_ref[0].astype(bf16)

    @pl.when(i < nu_ref[0])
    def _():
        lo, hi = _unpack_halves(xs_ref[...])
        lo, hi = lo.astype(bf16), hi.astype(bf16)
        g = (jnp.dot(lo, wg_s[:HALF, :], preferred_element_type=f32)
             + jnp.dot(hi, wg_s[HALF:, :], preferred_element_type=f32))
        u = (jnp.dot(lo, wu_s[:HALF, :], preferred_element_type=f32)
             + jnp.dot(hi, wu_s[HALF:, :], preferred_element_type=f32))
        a = (_silu(g) * u).astype(bf16)
        y_ref[...] = _pack_halves(jnp.dot(a, wd_s[...], preferred_element_type=f32))


def _experts(blk_e, n_used, xs, w_gate, w_up, w_down, *, block_rows):
    n_rows = xs.shape[0]
    n_blocks = n_rows // block_rows
    d, de = w_gate.shape[1], w_gate.shape[2]
    rows = pl.BlockSpec((block_rows, HALF), lambda i, be, nu: (jnp.minimum(i, nu[0] - 1), 0))
    return pl.pallas_call(
        _experts_body,
        grid_spec=pltpu.PrefetchScalarGridSpec(
            num_scalar_prefetch=2,
            grid=(n_blocks,),
            in_specs=[rows,
                      pl.BlockSpec((1, d, de), lambda i, be, nu: (be[i], 0, 0)),
                      pl.BlockSpec((1, d, de), lambda i, be, nu: (be[i], 0, 0)),
                      pl.BlockSpec((1, de, d), lambda i, be, nu: (be[i], 0, 0))],
            out_specs=rows,
            scratch_shapes=[pltpu.VMEM((d, de), bf16), pltpu.VMEM((d, de), bf16), pltpu.VMEM((de, d), bf16)]),
        out_shape=jax.ShapeDtypeStruct((n_rows, HALF), u32),
        compiler_params=_cparams(("arbitrary",)),
        name="experts",
    )(blk_e, n_used, xs, w_gate, w_up, w_down)


def _combine_body(dest_hbm, y_hbm, wts_ref, hp_ref, x1_ref, gt2_ref, wsg_ref, wsu_ref, wsd_ref, gfin_ref, o_ref,
                  dest_smem, buf, sem_idx, sem_rows, *, nb, rpb):
    tm = nb * rpb
    i = pl.program_id(0)
    cp = pltpu.make_async_copy(dest_hbm.at[i], dest_smem, sem_idx)
    cp.start()
    cp.wait()

    def issue(t, carry):
        for j in range(TOP_K):
            pltpu.make_async_copy(y_hbm.at[pl.ds(dest_smem[j, t], 1), :], buf.at[j, pl.ds(t, 1), :],
                                  sem_rows).start()
        return carry

    lax.fori_loop(0, tm, issue, 0)
    lo, hi = _unpack_halves(hp_ref[...])
    lo, hi = lo.astype(bf16), hi.astype(bf16)
    g = (jnp.dot(lo, wsg_ref[:HALF, :], preferred_element_type=f32)
         + jnp.dot(hi, wsg_ref[HALF:, :], preferred_element_type=f32))
    u = (jnp.dot(lo, wsu_ref[:HALF, :], preferred_element_type=f32)
         + jnp.dot(hi, wsu_ref[HALF:, :], preferred_element_type=f32))
    shared = jnp.dot((_silu(g) * u).astype(bf16), wsd_ref[...], preferred_element_type=f32)
    for j in range(TOP_K):
        pltpu.make_async_copy(y_hbm.at[pl.ds(0, tm), :], buf.at[j], sem_rows).wait()
    r_lo = jnp.zeros((tm, HALF), f32)
    r_hi = jnp.zeros((tm, HALF), f32)
    for j in range(TOP_K):
        ylo, yhi = _unpack_halves(buf[j])
        w = wts_ref[:, j:j + 1]
        r_lo = r_lo + w * ylo
        r_hi = r_hi + w * yhi
    moe = jnp.concatenate([r_lo, r_hi], axis=-1) + shared
    out = (x1_ref[...].reshape(nb, rpb, D_MODEL) + gt2_ref[...] * moe.reshape(nb, rpb, D_MODEL)).reshape(tm, D_MODEL)
    o_ref[...] = out * lax.rsqrt(jnp.mean(out * out, axis=-1, keepdims=True) + NORM_EPS) * gfin_ref[...]


def _combine(dest_tiles, y, wts_rows, hp, x1, m3, w_sg, w_su, w_sd, g_fin, *, nb, rpb, tiles_per_batch, tile0):
    t, d = x1.shape
    tm = nb * rpb
    row = lambda w: pl.BlockSpec((tm, w), lambda i: (i, 0))
    full = lambda a: pl.BlockSpec(a.shape, lambda i: (0,) * a.ndim)
    return pl.pallas_call(
        functools.partial(_combine_body, nb=nb, rpb=rpb),
        grid=(t // tm,),
        in_specs=[pl.BlockSpec(memory_space=pl.ANY), pl.BlockSpec(memory_space=pl.ANY), row(TOP_K), row(HALF), row(d),
                  pl.BlockSpec((nb, 1, d), lambda i: (i // tiles_per_batch, 0, 5)),
                  full(w_sg), full(w_su), full(w_sd), full(g_fin)],
        out_specs=row(d),
        out_shape=jax.ShapeDtypeStruct((t, d), f32),
        scratch_shapes=[pltpu.SMEM((TOP_K, tm), i32), pltpu.VMEM((TOP_K, tm, HALF), u32),
                        pltpu.SemaphoreType.DMA, pltpu.SemaphoreType.DMA],
        compiler_params=_cparams(("arbitrary",)),
        name="combine",
    )(dest_tiles[tile0:tile0 + t // tm], y, wts_rows, hp, x1, m3, w_sg, w_su, w_sd, g_fin)


_ROW_TILE = 256


def kernel(x_prompt, x_sample, state_gla, cache_fox_k, cache_fox_v, cache_fox_logf, c_prompt, c_sample, w_ada, b_ada, g_norm_mix, g_norm_ffn, w_in, w_gla_gk2, b_gla_gk, g_gla_out, b_fox_f, g_fox_out, w_out, w_router, b_router, w_exp_gate, w_exp_up, w_exp_down, w_sh_gate, w_sh_up, w_sh_down, g_final):
    assert w_ada.shape[0] == 1, "single-layer trunk"
    bp, lp, d = x_prompt.shape
    bs, ls, _ = x_sample.shape
    past = cache_fox_k.shape[2]
    tp, ts = bp * lp, bs * ls

    wi = w_in[0]
    o_glr = 2 * GLA_KEY_WIDTH + 2 * GLA_VALUE_WIDTH
    o_fq = o_glr + GLA_LOW_RANK
    o_ff = o_fq + 3 * FOX_WIDTH
    small = jnp.concatenate([wi[:, o_ff:o_ff + FOX_HEADS], wi[:, o_glr:o_fq],
                             jnp.zeros((d, 128 - FOX_HEADS - GLA_LOW_RANK), f32)], axis=1)
    w_cat = jnp.concatenate([wi[:, :o_glr], wi[:, o_fq:o_ff], small], axis=1).astype(bf16)
    wgk_pad = jnp.zeros((128, GLA_KEY_WIDTH), f32).at[FOX_HEADS:FOX_HEADS + GLA_LOW_RANK].set(w_gla_gk2[0]).astype(bf16)
    bgk = b_gla_gk[0].reshape(1, GLA_KEY_WIDTH)
    bf_pad = jnp.zeros((1, 128), f32).at[0, :FOX_HEADS].set(b_fox_f[0])
    g_mix = g_norm_mix[0].reshape(1, d)
    g_ffn = g_norm_ffn[0].reshape(1, d)
    g_gla4 = jnp.tile(g_gla_out[0], GLA_HEADS).reshape(1, GLA_VALUE_WIDTH)
    g_fox = g_fox_out[0].reshape(1, FOX_WIDTH)
    g_fin = g_final.reshape(1, d)
    w_out_b = w_out[0].astype(bf16)
    w_rt = w_router[0].T
    b_r = b_router[0].reshape(N_GROUPS, N_EXPERTS // N_GROUPS, 1)
    w_sg, w_su, w_sd = w_sh_gate[0].astype(bf16), w_sh_up[0].astype(bf16), w_sh_down[0].astype(bf16)

    m_all = _ada(jnp.concatenate([c_prompt, c_sample], axis=0), w_ada[0], b_ada[0])
    m3p = m_all[:bp].reshape(bp, 1, 6 * d)
    m3s = m_all[bp:].reshape(bs, 1, 6 * d)

    groups = []
    for (x, m3, bsz, length) in ((x_prompt, m3p, bp, lp), (x_sample, m3s, bs, ls)):
        if length >= _ROW_TILE:
            nb, rpb_in, tpb_in = 1, 512, length // 512
            rpb, tpb = _ROW_TILE, length // _ROW_TILE
        else:
            nb, rpb_in, tpb_in = _ROW_TILE // length, length, 1
            rpb, tpb = length, 1
        groups.append(dict(x2=x.reshape(bsz * length, d), m3=m3, bsz=bsz, length=length,
                           nb=nb, rpb_in=rpb_in, tpb_in=tpb_in, rpb=rpb, tpb=tpb))

    for gr in groups:
        (gr["gq"], gr["gk"], gr["gv"], gr["gg"], gr["la"], gr["fq"], gr["fk"], gr["fv"], gr["fkb"], gr["fvb"],
         gr["lf"]) = _inproj(gr["x2"], gr["m3"], g_mix, w_cat, wgk_pad, bgk, bf_pad,
                             nb=gr["nb"], rpb=gr["rpb_in"], tiles_per_batch=gr["tpb_in"])

    gp, gs = groups

    s0p = jnp.zeros((bp, GLA_HEADS, GLA_HEAD_K, GLA_HEAD_V), f32)
    gp["og"], sp_new = _gla(gp["gq"], gp["gk"], gp["gv"], gp["la"], s0p, bsz=bp, length=lp, chunk=GLA_CHUNK)
    gs["og"], ss_new = _gla(gs["gq"], gs["gk"], gs["gv"], gs["la"], state_gla[0], bsz=bs, length=ls, chunk=ls)

    lf_p = gp["lf"].reshape(bp, lp, FOX_HEADS)
    c_p = _cumsum_lanes(lf_p.transpose(0, 2, 1))
    c_cols = c_p.reshape(bp, FOX_HEADS // _HG, _HG, lp)
    c_rows = c_cols.transpose(0, 1, 3, 2)
    gp["of"] = _fox_prompt(gp["fq"], gp["fkb"], gp["fvb"], c_rows, c_cols, bsz=bp, length=lp)

    lf_s = gs["lf"].reshape(bs, ls, FOX_HEADS)
    lpad = -(-ls // 128) * 128
    c_s = _cumsum_lanes(jnp.pad(lf_s.transpose(0, 2, 1), ((0, 0), (0, 0), (0, lpad - ls))))[:, :, :ls]
    ck_past = _cumsum_lanes(cache_fox_logf[0].astype(f32).transpose(0, 2, 1), suffix=True)
    fq_s = gs["fq"].reshape(bs, ls, FOX_HEADS, FOX_HEAD_DIM)
    eye_h = jnp.eye(FOX_HEADS, dtype=bf16)
    qbd = jnp.einsum("bihd,hg->bhdgi", fq_s, eye_h).reshape(bs, FOX_WIDTH, FOX_HEADS * ls)
    e_mat = jnp.repeat(jnp.eye(FOX_HEADS, dtype=f32), ls, axis=1)
    of_s = _fox_sample(cache_fox_k[0].reshape(bs, past, FOX_WIDTH), cache_fox_v[0].reshape(bs, past, FOX_WIDTH),
                       gs["fk"].reshape(bs, ls, FOX_WIDTH), gs["fv"].reshape(bs, ls, FOX_WIDTH), qbd,
                       ck_past.transpose(0, 2, 1), c_s.transpose(0, 2, 1), c_s.reshape(bs, 1, FOX_HEADS * ls), e_mat)
    gs["of"] = of_s.reshape(ts, FOX_WIDTH)

    for gr in groups:
        gr["x1"], gr["hp"], gr["idx"], gr["wts"] = _outproj(
            gr["og"], gr["gg"], gr["of"], gr["x2"], gr["m3"], g_gla4, g_fox, g_ffn, w_out_b, w_rt, b_r,
            nb=gr["nb"] if gr["length"] < _ROW_TILE else 1, rpb=gr["rpb"], tiles_per_batch=gr["tpb"])

    t_all = tp + ts
    n_assign = t_all * TOP_K
    n_blocks = -(-n_assign // EXPERT_ROWS) + N_EXPERTS
    n_rows = n_blocks * EXPERT_ROWS
    idx_all = jnp.concatenate([gp["idx"], gs["idx"]], axis=1)
    dest, counts = _rank(idx_all, block_rows=EXPERT_ROWS)
    blocks_per_e = (counts[:, 0].astype(i32) + EXPERT_ROWS - 1) // EXPERT_ROWS
    blk_end = jnp.cumsum(blocks_per_e)
    n_used = blk_end[-1:]
    blk_e = jnp.searchsorted(blk_end, jnp.minimum(jnp.arange(n_blocks, dtype=i32), n_used[0] - 1), side="right")
    blk_e = jnp.minimum(blk_e, N_EXPERTS - 1).astype(i32)
    dest_tiles = dest.reshape(TOP_K, t_all // _ROW_TILE, _ROW_TILE).transpose(1, 0, 2)
    xs = _dispatch(dest_tiles, gp["hp"], None, n_rows=n_rows, tm=_ROW_TILE, tile0=0)
    xs = _dispatch(dest_tiles, gs["hp"], xs, n_rows=n_rows, tm=_ROW_TILE, tile0=tp // _ROW_TILE)
    y = _experts(blk_e, n_used.astype(i32), xs, w_exp_gate[0], w_exp_up[0], w_exp_down[0], block_rows=EXPERT_ROWS)

    outs = []
    for gr, tile0 in ((gp, 0), (gs, tp // _ROW_TILE)):
        outs.append(_combine(dest_tiles, y, gr["wts"].T, gr["hp"], gr["x1"], gr["m3"], w_sg, w_su, w_sd, g_fin,
                             nb=gr["nb"] if gr["length"] < _ROW_TILE else 1, rpb=gr["rpb"],
                             tiles_per_batch=gr["tpb"], tile0=tile0))
    y_prompt = outs[0].reshape(bp, lp, d)
    y_sample = outs[1].reshape(bs, ls, d)

    return (y_prompt, y_sample, sp_new[None], ss_new[None],
            gp["fk"].reshape(1, bp, lp, FOX_HEADS, FOX_HEAD_DIM), gp["fv"].reshape(1, bp, lp, FOX_HEADS, FOX_HEAD_DIM),
            lf_p[None],
            gs["fk"].reshape(1, bs, ls, FOX_HEADS, FOX_HEAD_DIM), gs["fv"].reshape(1, bs, ls, FOX_HEADS, FOX_HEAD_DIM),
            lf_s[None])
```

```python
import functools

import jax
import jax.numpy as jnp
from jax import lax
from jax.experimental import pallas as pl
from jax.experimental.pallas import tpu as pltpu

f32, bf16, i32 = jnp.float32, jnp.bfloat16, jnp.int32
HIGHEST = lax.Precision.HIGHEST

D_MODEL = 1024
GLA_HEADS, GLA_HEAD_K, GLA_HEAD_V = 4, 64, 128
GLA_KEY_WIDTH, GLA_VALUE_WIDTH, GLA_LOW_RANK = 256, 512, 16
GLA_GATE_NORMALIZER = 16.0
GLA_CHUNK = 64
FOX_HEADS, FOX_HEAD_DIM, FOX_WIDTH = 8, 64, 512
N_EXPERTS, N_GROUPS, TOPK_GROUPS, TOP_K = 64, 8, 4, 8
D_EXPERT = 256
ROUTED_SCALE = 2.5
NORM_EPS = 1e-6
EXPERT_ROWS = 512
GROUP = 256
SEG = 16
PCHUNK = 256
STAGE_ROWS = -(-(TOP_K * GROUP + N_EXPERTS * (SEG - 1)) // (2 * PCHUNK)) * (2 * PCHUNK)
MAX_TILES = -(-(STAGE_ROWS // SEG) // 128) * 128
TILES_PER_BLOCK = EXPERT_ROWS // SEG
VMEM_LIMIT = 56 * 1024 * 1024


def _cparams(sem):
    return pltpu.CompilerParams(dimension_semantics=sem, vmem_limit_bytes=VMEM_LIMIT)


def _log_sigmoid(x):
    return jnp.minimum(x, 0.0) - jnp.log1p(jnp.exp(-jnp.abs(x)))


def _silu(x):
    return x * jax.nn.sigmoid(x)


def _ada_body(c_ref, w_ref, b_ref, o_ref):
    a = _silu(c_ref[...]).astype(bf16)
    o_ref[...] = jnp.dot(a, w_ref[...].astype(bf16), preferred_element_type=f32) + b_ref[...]


def _ada(c_all, w_ada, b_ada):
    nb, d = c_all.shape
    n = w_ada.shape[1]
    tn = 768
    return pl.pallas_call(
        _ada_body,
        grid=(n // tn,),
        in_specs=[pl.BlockSpec((nb, d), lambda j: (0, 0)),
                  pl.BlockSpec((d, tn), lambda j: (0, j)),
                  pl.BlockSpec((1, tn), lambda j: (0, j))],
        out_specs=pl.BlockSpec((nb, tn), lambda j: (0, j)),
        out_shape=jax.ShapeDtypeStruct((nb, n), f32),
        compiler_params=_cparams(("arbitrary",)),
        name="ada",
    )(c_all, w_ada, b_ada.reshape(1, n))


_C_GQ, _C_GK, _C_GV, _C_GG, _C_FQ, _C_FK, _C_FV, _C_SM, _C_END = 0, 256, 512, 1024, 1536, 2048, 2560, 3072, 3200


_NT = (((1,), (1,)), ((), ()))
_TN = (((0,), (0,)), ((), ()))


def _inproj_body(x_ref, sh_ref, sc_ref, g_ref, w_ref, wgk_ref, bgk_ref, bf_ref, wt_ref, bfc_ref,
                 gq_ref, gk_ref, gv_ref, gg_ref, la_ref, fq_ref, k_ref, v_ref, *rest, nb, rpb, seq_minor):
    tm = nb * rpb
    x = x_ref[...]
    y = x * lax.rsqrt(jnp.mean(x * x, axis=-1, keepdims=True) + NORM_EPS) * g_ref[...]
    h = (y.reshape(nb, rpb, D_MODEL) * (1.0 + sc_ref[...]) + sh_ref[...]).reshape(tm, D_MODEL).astype(bf16)

    def proj(a, b):
        return jnp.dot(h, w_ref[:, a:b], preferred_element_type=f32)

    gq_ref[...] = proj(_C_GQ, _C_GK) * (GLA_HEAD_K ** -0.5)
    gk_ref[...] = proj(_C_GK, _C_GV)
    gv_ref[...] = proj(_C_GV, _C_GG)
    gg_ref[...] = proj(_C_GG, _C_FQ)
    fq_ref[...] = (proj(_C_FQ, _C_FK) * (FOX_HEAD_DIM ** -0.5)).astype(bf16)
    zs = proj(_C_SM, _C_END)
    ga = jnp.dot(zs.astype(bf16), wgk_ref[...], preferred_element_type=f32) + bgk_ref[...]
    la_ref[...] = _log_sigmoid(ga) * (1.0 / GLA_GATE_NORMALIZER)
    if seq_minor:
        kb_ref, vb_ref, lf_ref = rest
        shape4 = (FOX_HEADS, FOX_HEAD_DIM, tm)
        kt = lax.dot_general(wt_ref[0:FOX_WIDTH, :], h, _NT, preferred_element_type=f32)
        k_ref[0] = kt.reshape(shape4)
        kb_ref[0, 0] = kt.astype(bf16).reshape(shape4)
        vt = lax.dot_general(wt_ref[FOX_WIDTH:2 * FOX_WIDTH, :], h, _NT, preferred_element_type=f32)
        v_ref[0] = vt.reshape(shape4)
        vb_ref[0, 0] = vt.astype(bf16).reshape(shape4)
        zf = lax.dot_general(wt_ref[2 * FOX_WIDTH:, :], h, _NT, preferred_element_type=f32)
        lf_ref[0] = _log_sigmoid(zf + bfc_ref[...])[:FOX_HEADS, :]
    else:
        (lf_ref,) = rest
        k_ref[...] = proj(_C_FK, _C_FV)
        v_ref[...] = proj(_C_FV, _C_SM)
        lf_ref[...] = _log_sigmoid(zs + bf_ref[...])[:, :FOX_HEADS]


def _inproj(x2, m3, g_mix, w_cat, wgk_pad, bgk, bf_pad, w_t, bf_col, *, nb, rpb, tiles_per_batch, seq_minor):
    t, d = x2.shape
    tm = nb * rpb
    tpb = tiles_per_batch
    bsz = t // (tm * tpb) if seq_minor else None
    row = lambda w: pl.BlockSpec((tm, w), lambda i: (i, 0))
    mod = lambda c: pl.BlockSpec((nb, 1, d), lambda i, c=c: (i // tpb, 0, c))
    full = lambda a: pl.BlockSpec(a.shape, lambda i: (0,) * a.ndim)
    widths = [(GLA_KEY_WIDTH, f32), (GLA_KEY_WIDTH, f32), (GLA_VALUE_WIDTH, f32), (GLA_VALUE_WIDTH, f32),
              (GLA_KEY_WIDTH, f32), (FOX_WIDTH, bf16)]
    out_specs = [row(w) for w, _ in widths]
    out_shape = [jax.ShapeDtypeStruct((t, w), dt) for w, dt in widths]
    if seq_minor:
        length = tm * tpb
        hd = pl.BlockSpec((1, FOX_HEADS, FOX_HEAD_DIM, tm), lambda i: (i // tpb, 0, 0, i % tpb))
        hd_tiles = pl.BlockSpec((1, 1, FOX_HEADS, FOX_HEAD_DIM, tm), lambda i: (i // tpb, i % tpb, 0, 0, 0))
        out_specs += [hd, hd, hd_tiles, hd_tiles, pl.BlockSpec((1, FOX_HEADS, tm), lambda i: (i // tpb, 0, i % tpb))]
        out_shape += [jax.ShapeDtypeStruct((bsz, FOX_HEADS, FOX_HEAD_DIM, length), f32)] * 2
        out_shape += [jax.ShapeDtypeStruct((bsz, tpb, FOX_HEADS, FOX_HEAD_DIM, tm), bf16)] * 2
        out_shape += [jax.ShapeDtypeStruct((bsz, FOX_HEADS, length), f32)]
    else:
        out_specs += [row(FOX_WIDTH), row(FOX_WIDTH), row(FOX_HEADS)]
        out_shape += [jax.ShapeDtypeStruct((t, w), f32) for w in (FOX_WIDTH, FOX_WIDTH, FOX_HEADS)]
    return pl.pallas_call(
        functools.partial(_inproj_body, nb=nb, rpb=rpb, seq_minor=seq_minor),
        grid=(t // tm,),
        in_specs=[row(d), mod(0), mod(1), full(g_mix), full(w_cat), full(wgk_pad), full(bgk), full(bf_pad),
                  full(w_t), full(bf_col)],
        out_specs=out_specs,
        out_shape=out_shape,
        compiler_params=_cparams(("arbitrary",)),
        name="inproj",
    )(x2, m3, m3, g_mix, w_cat, wgk_pad, bgk, bf_pad, w_t, bf_col)


def _cumsum_body(x_ref, o_ref, *, chunk, suffix):
    n = x_ref.shape[-1]
    r = lax.broadcasted_iota(i32, (chunk, chunk), 0)
    c = lax.broadcasted_iota(i32, (chunk, chunk), 1)
    tri = (r <= c).astype(f32)
    carry = jnp.zeros((x_ref.shape[1], 1), f32)
    for j in range(n // chunk):
        sl = slice(j * chunk, (j + 1) * chunk)
        cc = jnp.dot(x_ref[0, :, sl], tri, precision=HIGHEST, preferred_element_type=f32) + carry
        o_ref[0, :, sl] = cc
        carry = cc[:, chunk - 1:chunk]
    if suffix:
        o_ref[0] = o_ref[0] - carry


def _cumsum_lanes(x, *, suffix=False):
    b, h, n = x.shape
    return pl.pallas_call(
        functools.partial(_cumsum_body, chunk=128, suffix=suffix),
        grid=(b,),
        in_specs=[pl.BlockSpec((1, h, n), lambda i: (i, 0, 0))],
        out_specs=pl.BlockSpec((1, h, n), lambda i: (i, 0, 0)),
        out_shape=jax.ShapeDtypeStruct((b, h, n), f32),
        compiler_params=_cparams(("arbitrary",)),
        name="cumsum",
    )(x)


def _gla_body(q_ref, k_ref, v_ref, la_ref, s0_ref, o_ref, sout_ref, s_scr, *, chunk, cps, n_steps):
    si = pl.program_id(1)

    @pl.when(si == 0)
    def _():
        s_scr[...] = s0_ref[0].reshape(GLA_KEY_WIDTH, GLA_HEAD_V)

    r = lax.broadcasted_iota(i32, (chunk, chunk), 0)
    c = lax.broadcasted_iota(i32, (chunk, chunk), 1)
    causal = c <= r
    tril = causal.astype(f32)
    ones = jnp.ones((chunk, GLA_HEAD_V), f32)
    lane_head = lax.broadcasted_iota(i32, (1, GLA_KEY_WIDTH), 1) // GLA_HEAD_K
    zero = jnp.zeros((), bf16)
    mid = chunk // 2
    state = s_scr[...]
    for ch in range(cps):
        t = slice(ch * chunk, (ch + 1) * chunk)
        g = la_ref[t, :]
        b = jnp.dot(tril, g, precision=HIGHEST, preferred_element_type=f32)
        b_mid = b[mid:mid + 1, :]
        b_last = b[chunk - 1:chunk, :]
        q = q_ref[t, :]
        k = k_ref[t, :]
        qs = (q * jnp.exp(b - b_mid)).astype(bf16)
        ks = (k * jnp.exp(b_mid - b)).astype(bf16)
        qe = (q * jnp.exp(b)).astype(bf16)
        kd = (k * jnp.exp(b_last - b)).astype(bf16)
        decay = jnp.exp(lax.dot_general(g, ones, _TN, precision=HIGHEST, preferred_element_type=f32))
        state_b = state.astype(bf16)
        upd = []
        for h in range(GLA_HEADS):
            hm = lane_head == h
            rows = slice(h * GLA_HEAD_K, (h + 1) * GLA_HEAD_K)
            cols = slice(h * GLA_HEAD_V, (h + 1) * GLA_HEAD_V)
            vh = v_ref[t, cols].astype(bf16)
            a = lax.dot_general(jnp.where(hm, qs, zero), ks, _NT, preferred_element_type=f32)
            a = jnp.where(causal, a, 0.0).astype(bf16)
            o_ref[t, cols] = (jnp.dot(jnp.where(hm, qe, zero), state_b, preferred_element_type=f32)
                              + jnp.dot(a, vh, preferred_element_type=f32))
            upd.append(lax.dot_general(kd, vh, _TN, preferred_element_type=f32)[rows, :])
        state = state * decay + jnp.concatenate(upd, axis=0)
    s_scr[...] = state

    @pl.when(si == n_steps - 1)
    def _():
        sout_ref[0] = state.reshape(GLA_HEADS, GLA_HEAD_K, GLA_HEAD_V)


def _gla(gq, gk, gv, la, s0, *, bsz, length, chunk, cps):
    n_steps = length // (chunk * cps)
    rows = chunk * cps
    row = lambda w: pl.BlockSpec((rows, w), lambda b, s: (b * n_steps + s, 0))
    st = pl.BlockSpec((1, GLA_HEADS, GLA_HEAD_K, GLA_HEAD_V), lambda b, s: (b, 0, 0, 0))
    return pl.pallas_call(
        functools.partial(_gla_body, chunk=chunk, cps=cps, n_steps=n_steps),
        grid=(bsz, n_steps),
        in_specs=[row(GLA_KEY_WIDTH), row(GLA_KEY_WIDTH), row(GLA_VALUE_WIDTH), row(GLA_KEY_WIDTH), st],
        out_specs=[row(GLA_VALUE_WIDTH), st],
        out_shape=[jax.ShapeDtypeStruct((bsz * length, GLA_VALUE_WIDTH), f32),
                   jax.ShapeDtypeStruct((bsz, GLA_HEADS, GLA_HEAD_K, GLA_HEAD_V), f32)],
        scratch_shapes=[pltpu.VMEM((GLA_KEY_WIDTH, GLA_HEAD_V), f32)],
        compiler_params=_cparams(("arbitrary", "arbitrary")),
        name="gla",
    )(gq, gk, gv, la, s0)


_HG = 4


_HGW = _HG * FOX_HEAD_DIM
_FOX_RB, _FOX_CB = 256, 256


def _foxp_body(q_ref, k_ref, v_ref, cq_ref, ck_ref, o_ref, qm_scr, vone_scr, cqb_scr, m_scr, sh_scr, acc_scr, *, t):
    qi = pl.program_id(2)
    n_tiles = k_ref.shape[1]
    lane_head = lax.broadcasted_iota(i32, (1, _HGW), 1) // FOX_HEAD_DIM

    @pl.when(qi == 0)
    def _():
        row_head = lax.broadcasted_iota(i32, (_HGW, 1), 0) // FOX_HEAD_DIM
        for kt in range(n_tiles):
            vt = v_ref[0, kt].reshape(_HGW, t)
            for j in range(_HG):
                vone_scr[j, kt] = jnp.where(row_head == j, vt, jnp.ones((), bf16))

    q = q_ref[...]
    for j in range(_HG):
        qm_scr[j] = jnp.where(lane_head == j, q, jnp.zeros((), bf16))
        cqb_scr[j] = jnp.broadcast_to(cq_ref[0, 0, :, j:j + 1], (t, 128))
    m_scr[...] = jnp.full(m_scr.shape, -jnp.inf, f32)
    acc_scr[...] = jnp.zeros(acc_scr.shape, f32)

    def logits(ki, j, rb, cb, diagonal):
        r0, c0 = rb * _FOX_RB, cb * _FOX_CB
        if diagonal and c0 > r0 + _FOX_RB - 1:
            return None
        kt = k_ref[0, ki, :, :, c0:c0 + _FOX_CB].reshape(_HGW, _FOX_CB)
        u = jnp.dot(qm_scr[j, r0:r0 + _FOX_RB, :], kt, preferred_element_type=f32) - ck_ref[0, 0, ki, j:j + 1, c0:c0 + _FOX_CB]
        if diagonal and c0 + _FOX_CB - 1 > r0:
            qpos = r0 + lax.broadcasted_iota(i32, (_FOX_RB, _FOX_CB), 0)
            kpos = c0 + lax.broadcasted_iota(i32, (_FOX_RB, _FOX_CB), 1)
            u = jnp.where(kpos <= qpos, u, -jnp.inf)
        return u

    def max_pass(ki, diagonal):
        for j in range(_HG):
            for cb in range(t // _FOX_CB):
                for rb in range(t // _FOX_RB):
                    u = logits(ki, j, rb, cb, diagonal)
                    if u is not None:
                        rows = slice(rb * _FOX_RB, (rb + 1) * _FOX_RB)
                        m = m_scr[j, rows, :]
                        for c in range(_FOX_CB // 128):
                            m = jnp.maximum(m, u[:, c * 128:(c + 1) * 128])
                        m_scr[j, rows, :] = m

    def sum_pass(ki, diagonal):
        for j in range(_HG):
            for cb in range(t // _FOX_CB):
                vt = vone_scr[j, ki, :, cb * _FOX_CB:(cb + 1) * _FOX_CB]
                for rb in range(t // _FOX_RB):
                    u = logits(ki, j, rb, cb, diagonal)
                    if u is not None:
                        rows = slice(rb * _FOX_RB, (rb + 1) * _FOX_RB)
                        p = jnp.exp(u - jnp.concatenate([sh_scr[j, rows, :]] * (_FOX_CB // 128), axis=-1))
                        acc_scr[j, rows, :] += lax.dot_general(p.astype(bf16), vt, _NT, preferred_element_type=f32)

    lax.fori_loop(0, qi, lambda ki, c: (max_pass(ki, False), c)[1], 0)
    for j in range(_HG):
        for rb in range(t // _FOX_RB):
            rows = slice(rb * _FOX_RB, (rb + 1) * _FOX_RB)
            m = m_scr[j, rows, :]
            for cb in range(t // _FOX_CB):
                u = logits(qi, j, rb, cb, True)
                if u is not None:
                    for c in range(_FOX_CB // 128):
                        m = jnp.maximum(m, u[:, c * 128:(c + 1) * 128])
            cq = cqb_scr[j, rows, :]
            m_full = jnp.max(m, axis=-1, keepdims=True) + cq
            sh_scr[j, rows, :] = m_full - cq
    lax.fori_loop(0, qi, lambda ki, c: (sum_pass(ki, False), c)[1], 0)
    sum_pass(qi, True)

    out = jnp.zeros((t, _HGW), f32)
    for j in range(_HG):
        acc = acc_scr[j]
        other = acc[:, 128:256] if j < _HG // 2 else acc[:, 0:128]
        inv = 1.0 / other
        out = jnp.where(lane_head == j, acc * jnp.concatenate([inv, inv], axis=-1), out)
    o_ref[...] = out


def _fox_prompt(fq, kt, vt, c_rows, c_cols, *, bsz, length, t=512):
    n = length // t
    kv = pl.BlockSpec((1, n, _HG, FOX_HEAD_DIM, t), lambda b, g, qi: (b, 0, g, 0, 0))
    return pl.pallas_call(
        functools.partial(_foxp_body, t=t),
        grid=(bsz, FOX_HEADS // _HG, n),
        in_specs=[pl.BlockSpec((t, _HGW), lambda b, g, qi: (b * n + qi, g)), kv, kv,
                  pl.BlockSpec((1, 1, t, _HG), lambda b, g, qi: (b, g, qi, 0)),
                  pl.BlockSpec((1, 1, n, _HG, t), lambda b, g, qi: (b, g, 0, 0, 0))],
        out_specs=pl.BlockSpec((t, _HGW), lambda b, g, qi: (b * n + qi, g)),
        out_shape=jax.ShapeDtypeStruct((bsz * length, FOX_WIDTH), f32),
        scratch_shapes=[pltpu.VMEM((_HG, t, _HGW), bf16), pltpu.VMEM((_HG, n, _HGW, t), bf16),
                        pltpu.VMEM((_HG, t, 128), f32), pltpu.VMEM((_HG, t, 128), f32),
                        pltpu.VMEM((_HG, t, 128), f32), pltpu.VMEM((_HG, t, _HGW), f32)],
        compiler_params=_cparams(("arbitrary",) * 3),
        name="fox_prompt",
    )(fq, kt, vt, c_rows, c_cols)


def _foxs_body(kc_ref, vc_ref, kn_ref, vn_ref, qbd_ref, ckp_ref, ckn_ref, cq_ref, o_ref, s_scr,
               *, past, new, kchunk):
    nrow = FOX_HEADS * new
    qbd = qbd_ref[0]
    cq = cq_ref[0]

    def per_head_rows(c8):
        return jnp.concatenate([jnp.broadcast_to(c8[h:h + 1, :], (new, c8.shape[1])) for h in range(FOX_HEADS)], axis=0)

    n_chunks = past // kchunk
    m = jnp.full((nrow, 1), -jnp.inf, f32)
    for c in range(n_chunks):
        cols = slice(c * kchunk, (c + 1) * kchunk)
        kt = kc_ref[0, :, :, cols].reshape(FOX_WIDTH, kchunk).astype(bf16)
        s = jnp.dot(qbd, kt, preferred_element_type=f32) + cq - per_head_rows(ckp_ref[0, :, cols])
        s_scr[:, cols] = s
        m = jnp.maximum(m, jnp.max(s, axis=-1, keepdims=True))
    sn = lax.dot_general(qbd, kn_ref[0].astype(bf16), _NT, preferred_element_type=f32)
    sn = sn + cq - per_head_rows(ckn_ref[0])
    qry_i = lax.broadcasted_iota(i32, (nrow, new), 0) % new
    key_j = lax.broadcasted_iota(i32, (nrow, new), 1)
    sn = jnp.where(key_j <= qry_i, sn, -jnp.inf)
    m = jnp.maximum(m, jnp.max(sn, axis=-1, keepdims=True))
    pn = jnp.exp(sn - m)
    l = jnp.sum(pn, axis=-1, keepdims=True)
    for c in range(n_chunks):
        cols = slice(c * kchunk, (c + 1) * kchunk)
        p = jnp.exp(s_scr[:, cols] - m)
        s_scr[:, cols] = p
        l = l + jnp.sum(p, axis=-1, keepdims=True)
    inv = 1.0 / l
    acc = jnp.dot((pn * inv).astype(bf16), vn_ref[0].astype(bf16), preferred_element_type=f32)
    for c in range(n_chunks):
        cols = slice(c * kchunk, (c + 1) * kchunk)
        vt = vc_ref[0, :, :, cols].reshape(FOX_WIDTH, kchunk).astype(bf16)
        acc = acc + lax.dot_general((s_scr[:, cols] * inv).astype(bf16), vt, _NT, preferred_element_type=f32)
    for h in range(FOX_HEADS):
        o_ref[0, :, h * FOX_HEAD_DIM:(h + 1) * FOX_HEAD_DIM] = (
            acc[h * new:(h + 1) * new, h * FOX_HEAD_DIM:(h + 1) * FOX_HEAD_DIM])


def _fox_sample(kc_t, vc_t, kn, vn, qbd, ckp, ckn, cq, *, kchunk=512):
    bsz, _, _, past = kc_t.shape
    new, w = kn.shape[1], kn.shape[2]
    blk = lambda a: pl.BlockSpec((1,) + a.shape[1:], lambda b: (b,) + (0,) * (a.ndim - 1))
    return pl.pallas_call(
        functools.partial(_foxs_body, past=past, new=new, kchunk=kchunk),
        grid=(bsz,),
        in_specs=[blk(kc_t), blk(vc_t), blk(kn), blk(vn), blk(qbd), blk(ckp), blk(ckn), blk(cq)],
        out_specs=pl.BlockSpec((1, new, w), lambda b: (b, 0, 0)),
        out_shape=jax.ShapeDtypeStruct((bsz, new, w), f32),
        scratch_shapes=[pltpu.VMEM((FOX_HEADS * new, past), f32)],
        compiler_params=_cparams(("arbitrary",)),
        name="fox_sample",
    )(kc_t, vc_t, kn, vn, qbd, ckp, ckn, cq)


def _route(logits_t, b_router):
    ne, tm = logits_t.shape
    gsz = ne // N_GROUPS
    shape3 = (N_GROUPS, gsz, tm)
    scores = jax.nn.sigmoid(logits_t).reshape(shape3)
    choice = scores + b_router
    im = lax.broadcasted_iota(i32, shape3, 1)
    ig = lax.broadcasted_iota(i32, (N_GROUPS, 1, tm), 0)
    ie = lax.broadcasted_iota(i32, shape3, 0) * gsz + im
    neg = -jnp.inf
    m1 = jnp.max(choice, axis=1, keepdims=True)
    first = jnp.min(jnp.where(choice == m1, im, gsz), axis=1, keepdims=True)
    m2 = jnp.max(jnp.where(im == first, neg, choice), axis=1, keepdims=True)
    cur = m1 + m2
    gsel = jnp.zeros(cur.shape, i32)
    for _ in range(TOPK_GROUPS):
        mx = jnp.max(cur, axis=0, keepdims=True)
        hit = ig == jnp.min(jnp.where(cur == mx, ig, N_GROUPS), axis=0, keepdims=True)
        gsel = jnp.where(hit, 1, gsel)
        cur = jnp.where(hit, neg, cur)
    cur = jnp.where(jnp.broadcast_to(gsel, shape3) > 0, choice, neg)
    ids, ws = [], []
    for _ in range(TOP_K):
        mx = jnp.max(jnp.max(cur, axis=1, keepdims=True), axis=0, keepdims=True)
        cand = jnp.where(cur == mx, ie, ne)
        fi = jnp.min(jnp.min(cand, axis=1, keepdims=True), axis=0, keepdims=True)
        hit = ie == fi
        w = jnp.where(hit, scores, 0.0)
        ids.append(fi.reshape(1, tm))
        ws.append(jnp.sum(jnp.sum(w, axis=1, keepdims=True), axis=0, keepdims=True).reshape(1, tm))
        cur = jnp.where(hit, neg, cur)
    tot = ws[0]
    for w in ws[1:]:
        tot = tot + w
    scale = ROUTED_SCALE / (tot + 1e-20)
    return ids, [w * scale for w in ws]


def _outproj_body(og_ref, gg_ref, of_ref, x_ref, gt1_ref, sh2_ref, sc2_ref, ggla_ref, gfox_ref, gffn_ref,
                  wout_ref, wr_ref, br_ref, x1_ref, hp_ref, idx_ref, wts_ref, *, nb, rpb):
    tm = nb * rpb
    og = og_ref[...]
    parts = []
    for h in range(GLA_HEADS):
        seg = og[:, h * GLA_HEAD_V:(h + 1) * GLA_HEAD_V]
        parts.append(seg * lax.rsqrt(jnp.mean(seg * seg, axis=-1, keepdims=True) + NORM_EPS))
    gla_n = jnp.concatenate(parts, axis=-1) * ggla_ref[...] * _silu(gg_ref[...])
    of = of_ref[...]
    fox_n = of * lax.rsqrt(jnp.mean(of * of, axis=-1, keepdims=True) + NORM_EPS) * gfox_ref[...]
    merged = jnp.concatenate([gla_n, fox_n], axis=-1).astype(bf16)
    mix = jnp.dot(merged, wout_ref[...], preferred_element_type=f32)
    x1 = (x_ref[...].reshape(nb, rpb, D_MODEL) + gt1_ref[...] * mix.reshape(nb, rpb, D_MODEL))
    x1_ref[...] = x1.reshape(tm, D_MODEL)
    x1f = x1.reshape(tm, D_MODEL)
    y = x1f * lax.rsqrt(jnp.mean(x1f * x1f, axis=-1, keepdims=True) + NORM_EPS) * gffn_ref[...]
    h2 = (y.reshape(nb, rpb, D_MODEL) * (1.0 + sc2_ref[...]) + sh2_ref[...]).reshape(tm, D_MODEL)
    hp_ref[...] = h2.astype(bf16)
    logits_t = lax.dot_general(wr_ref[...], h2, _NT, precision=HIGHEST, preferred_element_type=f32)
    ids, ws = _route(logits_t, br_ref[...])
    for j in range(TOP_K):
        idx_ref[j:j + 1, :] = ids[j]
        wts_ref[j:j + 1, :] = ws[j]


def _outproj(og, gg, of, x2, m3, g_gla4, g_fox, g_ffn, w_out, w_rt, b_r, *, nb, rpb, tiles_per_batch):
    t, d = x2.shape
    tm = nb * rpb
    row = lambda w: pl.BlockSpec((tm, w), lambda i: (i, 0))
    col = pl.BlockSpec((TOP_K, tm), lambda i: (0, i))
    mod = lambda c: pl.BlockSpec((nb, 1, d), lambda i, c=c: (i // tiles_per_batch, 0, c))
    full = lambda a: pl.BlockSpec(a.shape, lambda i: (0,) * a.ndim)
    return pl.pallas_call(
        functools.partial(_outproj_body, nb=nb, rpb=rpb),
        grid=(t // tm,),
        in_specs=[row(GLA_VALUE_WIDTH), row(GLA_VALUE_WIDTH), row(FOX_WIDTH), row(d), mod(2), mod(3), mod(4),
                  full(g_gla4), full(g_fox), full(g_ffn), full(w_out), full(w_rt), full(b_r)],
        out_specs=[row(d), row(d), col, col],
        out_shape=[jax.ShapeDtypeStruct((t, d), f32), jax.ShapeDtypeStruct((t, d), bf16),
                   jax.ShapeDtypeStruct((TOP_K, t), i32), jax.ShapeDtypeStruct((TOP_K, t), f32)],
        compiler_params=_cparams(("arbitrary",)),
        name="outproj",
    )(og, gg, of, x2, m3, m3, m3, g_gla4, g_fox, g_ffn, w_out, w_rt, b_r)


def _rank_body(idx_ref, lpos_ref, dtile_ref, ntile_ref, tot_ref, tot_scr, base_scr, run_scr, *, tn):
    ph, gi = pl.program_id(0), pl.program_id(1)
    ie = lax.broadcasted_iota(i32, (N_EXPERTS, tn), 0)
    idx = idx_ref[...]
    multihot = jnp.zeros((N_EXPERTS, tn), f32)
    for j in range(TOP_K):
        multihot = multihot + (ie == idx[j:j + 1, :]).astype(f32)
    seg = jnp.ceil(jnp.sum(multihot, axis=1, keepdims=True) * (1.0 / SEG))
    r = lax.broadcasted_iota(i32, (N_EXPERTS, N_EXPERTS), 0)
    c = lax.broadcasted_iota(i32, (N_EXPERTS, N_EXPERTS), 1)
    strict = (c < r).astype(f32)

    def excl_cumsum(col):
        return jnp.dot(strict, jnp.broadcast_to(col, (N_EXPERTS, 128)), precision=HIGHEST,
                       preferred_element_type=f32)[:, :1]

    @pl.when(jnp.logical_and(ph == 0, gi == 0))
    def _():
        tot_scr[...] = jnp.zeros(tot_scr.shape, f32)

    @pl.when(ph == 0)
    def _():
        tot_scr[...] += seg

    @pl.when(jnp.logical_and(ph == 1, gi == 0))
    def _():
        tot = tot_scr[...]
        blocks = jnp.ceil(tot * (float(SEG) / EXPERT_ROWS))
        base_scr[...] = excl_cumsum(blocks) * float(EXPERT_ROWS // SEG)
        run_scr[...] = jnp.zeros(run_scr.shape, f32)
        tot_ref[...] = tot

    @pl.when(ph == 1)
    def _():
        lstart = excl_cumsum(seg)
        rr = lax.broadcasted_iota(i32, (tn, tn), 0)
        cc = lax.broadcasted_iota(i32, (tn, tn), 1)
        before = (rr < cc).astype(bf16)
        excl = jnp.dot(multihot.astype(bf16), before, preferred_element_type=f32)
        pos = lstart * float(SEG) + excl
        for j in range(TOP_K):
            lpos_ref[j:j + 1, :] = jnp.sum(jnp.where(ie == idx[j:j + 1, :], pos, 0.0), axis=0,
                                           keepdims=True).astype(i32)
        dstart = base_scr[...] + run_scr[...]
        kk = lax.broadcasted_iota(i32, (N_EXPERTS, MAX_TILES), 1).astype(f32)
        e_of_k = jnp.sum((lstart + seg <= kk).astype(f32), axis=0, keepdims=True)
        own = lax.broadcasted_iota(i32, (N_EXPERTS, MAX_TILES), 0).astype(f32) == e_of_k
        dtile = jnp.sum(jnp.where(own, dstart - lstart, 0.0), axis=0, keepdims=True) + kk[:1, :]
        dtile_ref[0] = dtile.astype(i32)
        ntile_ref[0] = jnp.sum(seg, axis=0, keepdims=True).astype(i32)
        run_scr[...] += seg


def _rank(idx, *, tn):
    _, t = idx.shape
    ng = t // tn
    return pl.pallas_call(
        functools.partial(_rank_body, tn=tn),
        grid=(2, ng),
        in_specs=[pl.BlockSpec((TOP_K, tn), lambda p, g: (0, g))],
        out_specs=[pl.BlockSpec((TOP_K, tn), lambda p, g: (0, g * p)),
                   pl.BlockSpec((1, 1, MAX_TILES), lambda p, g: (g * p, 0, 0)),
                   pl.BlockSpec((1, 1, 1), lambda p, g: (g * p, 0, 0)),
                   pl.BlockSpec((N_EXPERTS, 1), lambda p, g: (0, 0))],
        out_shape=[jax.ShapeDtypeStruct((TOP_K, t), i32), jax.ShapeDtypeStruct((ng, 1, MAX_TILES), i32),
                   jax.ShapeDtypeStruct((ng, 1, 1), i32), jax.ShapeDtypeStruct((N_EXPERTS, 1), f32)],
        scratch_shapes=[pltpu.VMEM((N_EXPERTS, 1), f32)] * 3,
        compiler_params=_cparams(("arbitrary", "arbitrary")),
        name="rank",
    )(idx)


def _start_tile_copies(dtile_ref, ntile_ref, g, make_copy):
    def per_tile(k, carry):
        make_copy(k, dtile_ref[g * MAX_TILES + k]).start()
        return carry

    lax.fori_loop(0, ntile_ref[g], per_tile, 0)


_CHUNKS_PER_TRIP = 2


def _placement_rows(lpos, vals, r0):
    rid = lax.broadcasted_iota(i32, (PCHUNK, GROUP), 0).astype(f32).astype(bf16)
    rel = (lpos - r0).astype(f32).astype(bf16)
    p = jnp.zeros((PCHUNK, GROUP), bf16)
    for j in range(TOP_K):
        v = jnp.ones((), bf16) if vals is None else vals[j:j + 1, :]
        p = jnp.where(rel[j:j + 1, :] == rid, v, p)
    return p


def _chunk_trips(n_rows):
    return (n_rows + _CHUNKS_PER_TRIP * PCHUNK - 1) // (_CHUNKS_PER_TRIP * PCHUNK)


def _dispatch_body(dtile_ref, ntile_ref, tail0_ref, tailn_ref, misc_ref, lpos_ref, hp_ref, hs_ref, xs_hbm,
                   stage, zeros, sem, zsem, *, ngp, ng, n_blocks):
    g = pl.program_id(0)
    slot = g % 2
    h = jnp.where(g < ngp, hp_ref[...], hs_ref[...])
    n_tiles = ntile_ref[g]
    lpos = lpos_ref[...]

    def trip(c, carry):
        for k in range(_CHUNKS_PER_TRIP):
            r0 = pl.multiple_of((c * _CHUNKS_PER_TRIP + k) * PCHUNK, PCHUNK)
            p = _placement_rows(lpos, None, r0)
            rows = jnp.dot(p, h, preferred_element_type=f32).astype(bf16)
            stage[slot, pl.ds(pl.multiple_of(r0 // SEG, PCHUNK // SEG), PCHUNK // SEG)] = rows.reshape(
                PCHUNK // SEG, SEG, D_MODEL)
        return carry

    lax.fori_loop(0, _chunk_trips(n_tiles * SEG), trip, 0)
    _start_tile_copies(dtile_ref, ntile_ref, g, lambda k, d: pltpu.make_async_copy(
        stage.at[slot, k], xs_hbm.at[d], sem.at[slot]))

    def wait_group(gg, sl):
        n = ntile_ref[gg]
        pltpu.make_async_copy(stage.at[sl, pl.ds(0, n)], xs_hbm.at[pl.ds(0, n)], sem.at[sl]).wait()

    @pl.when(g > 0)
    def _():
        wait_group(g - 1, 1 - slot)

    @pl.when(g == ng - 1)
    def _():
        wait_group(g, slot)
        zeros[...] = jnp.zeros(zeros.shape, bf16)

        def per_expert(e, carry):
            t0 = tail0_ref[e]

            def per_tile(i, c2):
                pltpu.make_async_copy(zeros.at[0], xs_hbm.at[t0 + i], zsem).start()
                return c2

            lax.fori_loop(0, tailn_ref[e], per_tile, 0)
            return carry

        lax.fori_loop(0, N_EXPERTS, per_expert, 0)
        n_used = misc_ref[0]

        def per_block(b, carry):
            pltpu.make_async_copy(zeros, xs_hbm.at[pl.ds(b * TILES_PER_BLOCK, TILES_PER_BLOCK)], zsem).start()
            return carry

        lax.fori_loop(n_used, n_blocks, per_block, 0)
        n_zero = misc_ref[1] + (n_blocks - n_used) * TILES_PER_BLOCK

        @pl.when(n_zero > 0)
        def _():
            pltpu.make_async_copy(xs_hbm.at[pl.ds(0, n_zero)], xs_hbm.at[pl.ds(0, n_zero)], zsem).wait()


def _dispatch(tables, tail0, tailn, misc, lpos, hp, hs, *, n_blocks):
    ngp, ngs = hp.shape[0] // GROUP, hs.shape[0] // GROUP
    ng = ngp + ngs
    d = hp.shape[1]
    return pl.pallas_call(
        functools.partial(_dispatch_body, ngp=ngp, ng=ng, n_blocks=n_blocks),
        grid_spec=pltpu.PrefetchScalarGridSpec(
            num_scalar_prefetch=5,
            grid=(ng,),
            in_specs=[pl.BlockSpec((TOP_K, GROUP), lambda g, *_: (0, g)),
                      pl.BlockSpec((GROUP, d), lambda g, *_: (jnp.minimum(g, ngp - 1), 0)),
                      pl.BlockSpec((GROUP, d), lambda g, *_: (jnp.maximum(g - ngp, 0), 0))],
            out_specs=pl.BlockSpec(memory_space=pl.ANY),
            scratch_shapes=[pltpu.VMEM((2, STAGE_ROWS // SEG, SEG, d), bf16), pltpu.VMEM((TILES_PER_BLOCK, SEG, d), bf16),
                            pltpu.SemaphoreType.DMA((2,)), pltpu.SemaphoreType.DMA]),
        out_shape=jax.ShapeDtypeStruct((n_blocks * TILES_PER_BLOCK, SEG, d), bf16),
        compiler_params=_cparams(("arbitrary",)),
        name="dispatch",
    )(*tables, tail0, tailn, misc, lpos, hp, hs)


def _experts_body(be_ref, nu_ref, xs_ref, wg_ref, wu_ref, wd_ref, y_ref, wg_s, wu_s, wd_s):
    i = pl.program_id(0)
    changed = jnp.logical_or(i == 0, be_ref[i] != be_ref[jnp.maximum(i - 1, 0)])

    @pl.when(changed)
    def _():
        wg_s[...] = wg_ref[0].astype(bf16)
        wu_s[...] = wu_ref[0].astype(bf16)
        wd_s[...] = wd_ref[0].astype(bf16)

    @pl.when(i < nu_ref[0])
    def _():
        x = xs_ref[...]
        g = jnp.dot(x, wg_s[...], preferred_element_type=f32)
        u = jnp.dot(x, wu_s[...], preferred_element_type=f32)
        a = (_silu(g) * u).astype(bf16)
        y_ref[...] = jnp.dot(a, wd_s[...], preferred_element_type=f32).astype(bf16)

    @pl.when(i >= nu_ref[0])
    def _():
        y_ref[...] = jnp.zeros(y_ref.shape, bf16)


def _experts(blk_e, n_used, xs, w_gate, w_up, w_down, *, block_rows):
    n_rows, d = xs.shape
    n_blocks = n_rows // block_rows
    de = w_gate.shape[2]
    return pl.pallas_call(
        _experts_body,
        grid_spec=pltpu.PrefetchScalarGridSpec(
            num_scalar_prefetch=2,
            grid=(n_blocks,),
            in_specs=[pl.BlockSpec((block_rows, d), lambda i, be, nu: (jnp.minimum(i, nu[0] - 1), 0)),
                      pl.BlockSpec((1, d, de), lambda i, be, nu: (be[i], 0, 0)),
                      pl.BlockSpec((1, d, de), lambda i, be, nu: (be[i], 0, 0)),
                      pl.BlockSpec((1, de, d), lambda i, be, nu: (be[i], 0, 0))],
            out_specs=pl.BlockSpec((block_rows, d), lambda i, be, nu: (i, 0)),
            scratch_shapes=[pltpu.VMEM((d, de), bf16), pltpu.VMEM((d, de), bf16), pltpu.VMEM((de, d), bf16)]),
        out_shape=jax.ShapeDtypeStruct((n_rows, d), bf16),
        compiler_params=_cparams(("arbitrary",)),
        name="experts",
    )(blk_e, n_used, xs, w_gate, w_up, w_down)


def _combine_body(dtile_ref, ntile_ref, lpos_ref, wts_ref, h_ref, x1_ref, gt2_ref, wsg_ref, wsu_ref, wsd_ref,
                  gfin_ref, y_hbm, o_ref, ybuf, acc, sem, *, nb, rpb, g0, ng):
    i = pl.program_id(0)
    g = g0 + i
    slot = i % 2

    def start_gather(gg, sl):
        _start_tile_copies(dtile_ref, ntile_ref, gg, lambda k, d: pltpu.make_async_copy(
            y_hbm.at[d], ybuf.at[sl, k], sem.at[sl]))

    @pl.when(i == 0)
    def _():
        ybuf[...] = jnp.zeros(ybuf.shape, bf16)
        start_gather(g, slot)

    @pl.when(i + 1 < ng)
    def _():
        start_gather(g + 1, 1 - slot)

    h = h_ref[...]
    gate = jnp.dot(h, wsg_ref[...], preferred_element_type=f32)
    up = jnp.dot(h, wsu_ref[...], preferred_element_type=f32)
    acc[...] = jnp.dot((_silu(gate) * up).astype(bf16), wsd_ref[...], preferred_element_type=f32)

    n_tiles = ntile_ref[g]
    n_rows = n_tiles * SEG
    pltpu.make_async_copy(y_hbm.at[pl.ds(0, n_tiles)], ybuf.at[slot, pl.ds(0, n_tiles)], sem.at[slot]).wait()

    lpos = lpos_ref[...]
    wts = wts_ref[...].astype(bf16)

    def trip(c, carry):
        for k in range(_CHUNKS_PER_TRIP):
            r0 = pl.multiple_of((c * _CHUNKS_PER_TRIP + k) * PCHUNK, PCHUNK)
            wt = _placement_rows(lpos, wts, r0)
            yrows = ybuf[slot, pl.ds(pl.multiple_of(r0 // SEG, PCHUNK // SEG), PCHUNK // SEG)].reshape(PCHUNK, D_MODEL)
            acc[...] += lax.dot_general(wt, yrows, _TN, preferred_element_type=f32)
        return carry

    lax.fori_loop(0, _chunk_trips(n_rows), trip, 0)
    out = (x1_ref[...].reshape(nb, rpb, D_MODEL) + gt2_ref[...] * acc[...].reshape(nb, rpb, D_MODEL))
    out = out.reshape(GROUP, D_MODEL)
    o_ref[...] = out * lax.rsqrt(jnp.mean(out * out, axis=-1, keepdims=True) + NORM_EPS) * gfin_ref[...]


def _combine(tables, lpos_rows, wts_rows, y, h2, x1, m3, w_sg, w_su, w_sd, g_fin, *, nb, rpb, tiles_per_batch, g0):
    t, d = x1.shape
    assert nb * rpb == GROUP
    ng = t // GROUP
    row = lambda w: pl.BlockSpec((GROUP, w), lambda i, *_: (i, 0))
    grow = lambda w: pl.BlockSpec((w, GROUP), lambda i, *_: (0, g0 + i))
    full = lambda a: pl.BlockSpec(a.shape, lambda i, *_: (0,) * a.ndim)
    return pl.pallas_call(
        functools.partial(_combine_body, nb=nb, rpb=rpb, g0=g0, ng=ng),
        grid_spec=pltpu.PrefetchScalarGridSpec(
            num_scalar_prefetch=2,
            grid=(ng,),
            in_specs=[grow(TOP_K), grow(TOP_K), row(d), row(d),
                      pl.BlockSpec((nb, 1, d), lambda i, *_: (i // tiles_per_batch, 0, 5)),
                      full(w_sg), full(w_su), full(w_sd), full(g_fin), pl.BlockSpec(memory_space=pl.ANY)],
            out_specs=row(d),
            scratch_shapes=[pltpu.VMEM((2, STAGE_ROWS // SEG, SEG, d), bf16), pltpu.VMEM((GROUP, d), f32),
                            pltpu.SemaphoreType.DMA((2,))]),
        out_shape=jax.ShapeDtypeStruct((t, d), f32),
        compiler_params=_cparams(("arbitrary",)),
        name="combine",
    )(*tables, lpos_rows, wts_rows, h2, x1, m3, w_sg, w_su, w_sd, g_fin, y)


_GLA_CHUNKS_PER_STEP = 4


def kernel(x_prompt, x_sample, state_gla, cache_fox_k, cache_fox_v, cache_fox_logf, c_prompt, c_sample, w_ada, b_ada, g_norm_mix, g_norm_ffn, w_in, w_gla_gk2, b_gla_gk, g_gla_out, b_fox_f, g_fox_out, w_out, w_router, b_router, w_exp_gate, w_exp_up, w_exp_down, w_sh_gate, w_sh_up, w_sh_down, g_final):
    assert w_ada.shape[0] == 1, "single-layer trunk"
    bp, lp, d = x_prompt.shape
    bs, ls, _ = x_sample.shape
    past = cache_fox_k.shape[2]
    tp, ts = bp * lp, bs * ls

    wi = w_in[0]
    o_glr = 2 * GLA_KEY_WIDTH + 2 * GLA_VALUE_WIDTH
    o_fq = o_glr + GLA_LOW_RANK
    o_ff = o_fq + 3 * FOX_WIDTH
    small = jnp.concatenate([wi[:, o_ff:o_ff + FOX_HEADS], wi[:, o_glr:o_fq],
                             jnp.zeros((d, 128 - FOX_HEADS - GLA_LOW_RANK), f32)], axis=1)
    w_cat = jnp.concatenate([wi[:, :o_glr], wi[:, o_fq:o_ff], small], axis=1).astype(bf16)
    w_t = jnp.concatenate([wi[:, o_fq + FOX_WIDTH:o_ff + FOX_HEADS], jnp.zeros((d, 16 - FOX_HEADS), f32)],
                          axis=1).T.astype(bf16)
    wgk_pad = jnp.zeros((128, GLA_KEY_WIDTH), f32).at[FOX_HEADS:FOX_HEADS + GLA_LOW_RANK].set(w_gla_gk2[0]).astype(bf16)
    bgk = b_gla_gk[0].reshape(1, GLA_KEY_WIDTH)
    bf_pad = jnp.zeros((1, 128), f32).at[0, :FOX_HEADS].set(b_fox_f[0])
    bf_col = jnp.zeros((16, 1), f32).at[:FOX_HEADS, 0].set(b_fox_f[0])
    g_mix = g_norm_mix[0].reshape(1, d)
    g_ffn = g_norm_ffn[0].reshape(1, d)
    g_gla4 = jnp.tile(g_gla_out[0], GLA_HEADS).reshape(1, GLA_VALUE_WIDTH)
    g_fox = g_fox_out[0].reshape(1, FOX_WIDTH)
    g_fin = g_final.reshape(1, d)
    w_out_b = w_out[0].astype(bf16)
    w_rt = w_router[0].T
    b_r = b_router[0].reshape(N_GROUPS, N_EXPERTS // N_GROUPS, 1)
    w_sg, w_su, w_sd = w_sh_gate[0].astype(bf16), w_sh_up[0].astype(bf16), w_sh_down[0].astype(bf16)

    m_all = _ada(jnp.concatenate([c_prompt, c_sample], axis=0), w_ada[0], b_ada[0])
    m3p = m_all[:bp].reshape(bp, 1, 6 * d)
    m3s = m_all[bp:].reshape(bs, 1, 6 * d)

    groups = []
    for (x, m3, bsz, length) in ((x_prompt, m3p, bp, lp), (x_sample, m3s, bs, ls)):
        if length >= GROUP:
            nb, rpb_in, tpb_in = 1, 512, length // 512
            rpb, tpb = GROUP, length // GROUP
        else:
            nb, rpb_in, tpb_in = GROUP // length, length, 1
            rpb, tpb = length, 1
        groups.append(dict(x2=x.reshape(bsz * length, d), m3=m3, bsz=bsz, length=length,
                           nb=nb, rpb_in=rpb_in, tpb_in=tpb_in, rpb=rpb, tpb=tpb))

    gp, gs = groups
    (gp["gq"], gp["gk"], gp["gv"], gp["gg"], gp["la"], gp["fq"], kt_p, vt_p, ktb_p, vtb_p, lft_p) = _inproj(
        gp["x2"], gp["m3"], g_mix, w_cat, wgk_pad, bgk, bf_pad, w_t, bf_col,
        nb=gp["nb"], rpb=gp["rpb_in"], tiles_per_batch=gp["tpb_in"], seq_minor=True)
    (gs["gq"], gs["gk"], gs["gv"], gs["gg"], gs["la"], gs["fq"], fk_s, fv_s, lf_s) = _inproj(
        gs["x2"], gs["m3"], g_mix, w_cat, wgk_pad, bgk, bf_pad, w_t, bf_col,
        nb=gs["nb"], rpb=gs["rpb_in"], tiles_per_batch=gs["tpb_in"], seq_minor=False)

    s0p = jnp.zeros((bp, GLA_HEADS, GLA_HEAD_K, GLA_HEAD_V), f32)
    gp["og"], sp_new = _gla(gp["gq"], gp["gk"], gp["gv"], gp["la"], s0p, bsz=bp, length=lp, chunk=GLA_CHUNK,
                            cps=_GLA_CHUNKS_PER_STEP)
    gs["og"], ss_new = _gla(gs["gq"], gs["gk"], gs["gv"], gs["la"], state_gla[0], bsz=bs, length=ls, chunk=ls, cps=1)

    c_p = _cumsum_lanes(lft_p)
    c_grp = c_p.reshape(bp, FOX_HEADS // _HG, _HG, lp)
    c_rows = c_grp.transpose(0, 1, 3, 2)
    t_att = gp["rpb_in"]
    c_cols = c_grp.reshape(bp, FOX_HEADS // _HG, _HG, lp // t_att, t_att).transpose(0, 1, 3, 2, 4)
    gp["of"] = _fox_prompt(gp["fq"], ktb_p, vtb_p, c_rows, c_cols, bsz=bp, length=lp, t=t_att)

    lf_s = lf_s.reshape(bs, ls, FOX_HEADS)
    lpad = -(-ls // 128) * 128
    c_s = _cumsum_lanes(jnp.pad(lf_s.transpose(0, 2, 1), ((0, 0), (0, 0), (0, lpad - ls))))[:, :, :ls]
    ck_past = _cumsum_lanes(cache_fox_logf[0].astype(f32).transpose(0, 2, 1), suffix=True)
    fq_s = gs["fq"].reshape(bs, ls, FOX_HEADS, FOX_HEAD_DIM)
    eye_h = jnp.eye(FOX_HEADS, dtype=bf16)
    qbd = jnp.einsum("bihd,hg->bhigd", fq_s, eye_h).reshape(bs, FOX_HEADS * ls, FOX_WIDTH)
    of_s = _fox_sample(cache_fox_k[0].transpose(0, 2, 3, 1), cache_fox_v[0].transpose(0, 2, 3, 1),
                       fk_s.reshape(bs, ls, FOX_WIDTH), fv_s.reshape(bs, ls, FOX_WIDTH), qbd,
                       ck_past, c_s, c_s.reshape(bs, FOX_HEADS * ls, 1))
    gs["of"] = of_s.reshape(ts, FOX_WIDTH)

    for gr in groups:
        gr["x1"], gr["hp"], gr["idx"], gr["wts"] = _outproj(
            gr["og"], gr["gg"], gr["of"], gr["x2"], gr["m3"], g_gla4, g_fox, g_ffn, w_out_b, w_rt, b_r,
            nb=gr["nb"] if gr["length"] < GROUP else 1, rpb=gr["rpb"], tiles_per_batch=gr["tpb"])

    t_all = tp + ts
    n_groups = t_all // GROUP
    tiles_per_block = EXPERT_ROWS // SEG
    n_blocks = -(-(t_all * TOP_K + n_groups * N_EXPERTS * (SEG - 1)) // EXPERT_ROWS) + N_EXPERTS
    idx_all = jnp.concatenate([gp["idx"], gs["idx"]], axis=1)
    lpos, dtile, ntile, tot = _rank(idx_all, tn=GROUP)
    tables = (dtile.reshape(n_groups * MAX_TILES), ntile.reshape(n_groups))
    tot_e = tot[:, 0].astype(i32)
    blocks_per_e = (tot_e + tiles_per_block - 1) // tiles_per_block
    blk_end = jnp.cumsum(blocks_per_e)
    n_used = blk_end[-1:]
    blk = jnp.minimum(jnp.arange(n_blocks, dtype=i32), n_used[0] - 1)
    blk_e = jnp.minimum(jnp.sum((blk_end[None, :] <= blk[:, None]).astype(i32), axis=1), N_EXPERTS - 1)
    tail0 = (blk_end - blocks_per_e) * tiles_per_block + tot_e
    tailn = blocks_per_e * tiles_per_block - tot_e
    misc = jnp.stack([n_used[0], jnp.sum(tailn)]).astype(i32)
    xs = _dispatch(tables, tail0, tailn, misc, lpos, gp["hp"], gs["hp"], n_blocks=n_blocks)
    y = _experts(blk_e, n_used.astype(i32), xs.reshape(n_blocks * EXPERT_ROWS, d), w_exp_gate[0], w_exp_up[0],
                 w_exp_down[0], block_rows=EXPERT_ROWS)
    y = y.reshape(n_blocks * TILES_PER_BLOCK, SEG, d)

    wts_all = jnp.concatenate([gp["wts"], gs["wts"]], axis=1)
    outs = []
    for gr, g0 in ((gp, 0), (gs, tp // GROUP)):
        outs.append(_combine(tables, lpos, wts_all, y, gr["hp"], gr["x1"], gr["m3"], w_sg, w_su, w_sd, g_fin,
                             nb=gr["nb"] if gr["length"] < GROUP else 1, rpb=gr["rpb"],
                             tiles_per_batch=gr["tpb"], g0=g0))
    y_prompt = outs[0].reshape(bp, lp, d)
    y_sample = outs[1].reshape(bs, ls, d)

    return (y_prompt, y_sample, sp_new[None], ss_new[None],
            kt_p.transpose(0, 3, 1, 2)[None], vt_p.transpose(0, 3, 1, 2)[None], lft_p.transpose(0, 2, 1)[None],
            fk_s.reshape(1, bs, ls, FOX_HEADS, FOX_HEAD_DIM), fv_s.reshape(1, bs, ls, FOX_HEADS, FOX_HEAD_DIM),
            lf_s[None])
```

```python
import functools

import jax
import jax.numpy as jnp
from jax import lax
from jax.experimental import pallas as pl
from jax.experimental.pallas import tpu as pltpu

f32, bf16, i32 = jnp.float32, jnp.bfloat16, jnp.int32
HIGHEST = lax.Precision.HIGHEST

D_MODEL = 1024
GLA_HEADS, GLA_HEAD_K, GLA_HEAD_V = 4, 64, 128
GLA_KEY_WIDTH, GLA_VALUE_WIDTH, GLA_LOW_RANK = 256, 512, 16
GLA_GATE_NORMALIZER = 16.0
GLA_CHUNK = 64
FOX_HEADS, FOX_HEAD_DIM, FOX_WIDTH = 8, 64, 512
N_EXPERTS, N_GROUPS, TOPK_GROUPS, TOP_K = 64, 8, 4, 8
D_EXPERT = 256
ROUTED_SCALE = 2.5
NORM_EPS = 1e-6
EXPERT_ROWS = 512
GROUP = 256
SEG = 16
PCHUNK = 256
STAGE_ROWS = -(-(TOP_K * GROUP + N_EXPERTS * SEG) // (2 * PCHUNK)) * (2 * PCHUNK)
TILES_PER_BLOCK = EXPERT_ROWS // SEG
VMEM_LIMIT = 56 * 1024 * 1024


def _cparams(sem):
    return pltpu.CompilerParams(dimension_semantics=sem, vmem_limit_bytes=VMEM_LIMIT)


def _log_sigmoid(x):
    return jnp.minimum(x, 0.0) - jnp.log1p(jnp.exp(-jnp.abs(x)))


def _silu(x):
    return x * jax.nn.sigmoid(x)


def _ada_body(c_ref, w_ref, b_ref, o_ref):
    a = _silu(c_ref[...]).astype(bf16)
    o_ref[...] = jnp.dot(a, w_ref[...].astype(bf16), preferred_element_type=f32) + b_ref[...]


def _ada(c_all, w_ada, b_ada):
    nb, d = c_all.shape
    n = w_ada.shape[1]
    tn = 768
    return pl.pallas_call(
        _ada_body,
        grid=(n // tn,),
        in_specs=[pl.BlockSpec((nb, d), lambda j: (0, 0)),
                  pl.BlockSpec((d, tn), lambda j: (0, j)),
                  pl.BlockSpec((1, tn), lambda j: (0, j))],
        out_specs=pl.BlockSpec((nb, tn), lambda j: (0, j)),
        out_shape=jax.ShapeDtypeStruct((nb, n), f32),
        compiler_params=_cparams(("arbitrary",)),
        name="ada",
    )(c_all, w_ada, b_ada.reshape(1, n))


_C_GQ, _C_GK, _C_GV, _C_GG, _C_FQ, _C_FK, _C_FV, _C_SM, _C_END = 0, 256, 512, 1024, 1536, 2048, 2560, 3072, 3200


_NT = (((1,), (1,)), ((), ()))
_TN = (((0,), (0,)), ((), ()))


def _inproj_body(x_ref, sh_ref, sc_ref, g_ref, w_ref, wgk_ref, bgk_ref, bf_ref, wt_ref, bfc_ref,
                 gq_ref, gk_ref, gv_ref, gg_ref, la_ref, fq_ref, k_ref, v_ref, *rest, nb, rpb, seq_minor):
    tm = nb * rpb
    x = x_ref[...]
    y = x * lax.rsqrt(jnp.mean(x * x, axis=-1, keepdims=True) + NORM_EPS) * g_ref[...]
    h = (y.reshape(nb, rpb, D_MODEL) * (1.0 + sc_ref[...]) + sh_ref[...]).reshape(tm, D_MODEL).astype(bf16)

    def proj(a, b):
        return jnp.dot(h, w_ref[:, a:b], preferred_element_type=f32)

    gq_ref[...] = proj(_C_GQ, _C_GK) * (GLA_HEAD_K ** -0.5)
    gk_ref[...] = proj(_C_GK, _C_GV)
    gv_ref[...] = proj(_C_GV, _C_GG)
    gg_ref[...] = proj(_C_GG, _C_FQ)
    fq_ref[...] = (proj(_C_FQ, _C_FK) * (FOX_HEAD_DIM ** -0.5)).astype(bf16)
    zs = proj(_C_SM, _C_END)
    ga = jnp.dot(zs.astype(bf16), wgk_ref[...], preferred_element_type=f32) + bgk_ref[...]
    la_ref[...] = _log_sigmoid(ga) * (1.0 / GLA_GATE_NORMALIZER)
    if seq_minor:
        kb_ref, vb_ref, lf_ref = rest
        shape4 = (FOX_HEADS, FOX_HEAD_DIM, tm)
        kt = lax.dot_general(wt_ref[0:FOX_WIDTH, :], h, _NT, preferred_element_type=f32)
        k_ref[0] = kt.reshape(shape4)
        kb_ref[0, 0] = kt.astype(bf16).reshape(shape4)
        vt = lax.dot_general(wt_ref[FOX_WIDTH:2 * FOX_WIDTH, :], h, _NT, preferred_element_type=f32)
        v_ref[0] = vt.reshape(shape4)
        vb_ref[0, 0] = vt.astype(bf16).reshape(shape4)
        zf = lax.dot_general(wt_ref[2 * FOX_WIDTH:, :], h, _NT, preferred_element_type=f32)
        lf_ref[0] = _log_sigmoid(zf + bfc_ref[...])[:FOX_HEADS, :]
    else:
        (lf_ref,) = rest
        k_ref[...] = proj(_C_FK, _C_FV)
        v_ref[...] = proj(_C_FV, _C_SM)
        lf_ref[...] = _log_sigmoid(zs + bf_ref[...])[:, :FOX_HEADS]


def _inproj(x2, m3, g_mix, w_cat, wgk_pad, bgk, bf_pad, w_t, bf_col, *, nb, rpb, tiles_per_batch, seq_minor):
    t, d = x2.shape
    tm = nb * rpb
    tpb = tiles_per_batch
    bsz = t // (tm * tpb) if seq_minor else None
    row = lambda w: pl.BlockSpec((tm, w), lambda i: (i, 0))
    mod = lambda c: pl.BlockSpec((nb, 1, d), lambda i, c=c: (i // tpb, 0, c))
    full = lambda a: pl.BlockSpec(a.shape, lambda i: (0,) * a.ndim)
    widths = [(GLA_KEY_WIDTH, f32), (GLA_KEY_WIDTH, f32), (GLA_VALUE_WIDTH, f32), (GLA_VALUE_WIDTH, f32),
              (GLA_KEY_WIDTH, f32), (FOX_WIDTH, bf16)]
    out_specs = [row(w) for w, _ in widths]
    out_shape = [jax.ShapeDtypeStruct((t, w), dt) for w, dt in widths]
    if seq_minor:
        length = tm * tpb
        hd = pl.BlockSpec((1, FOX_HEADS, FOX_HEAD_DIM, tm), lambda i: (i // tpb, 0, 0, i % tpb))
        hd_tiles = pl.BlockSpec((1, 1, FOX_HEADS, FOX_HEAD_DIM, tm), lambda i: (i // tpb, i % tpb, 0, 0, 0))
        out_specs += [hd, hd, hd_tiles, hd_tiles, pl.BlockSpec((1, FOX_HEADS, tm), lambda i: (i // tpb, 0, i % tpb))]
        out_shape += [jax.ShapeDtypeStruct((bsz, FOX_HEADS, FOX_HEAD_DIM, length), f32)] * 2
        out_shape += [jax.ShapeDtypeStruct((bsz, tpb, FOX_HEADS, FOX_HEAD_DIM, tm), bf16)] * 2
        out_shape += [jax.ShapeDtypeStruct((bsz, FOX_HEADS, length), f32)]
    else:
        out_specs += [row(FOX_WIDTH), row(FOX_WIDTH), row(FOX_HEADS)]
        out_shape += [jax.ShapeDtypeStruct((t, w), f32) for w in (FOX_WIDTH, FOX_WIDTH, FOX_HEADS)]
    return pl.pallas_call(
        functools.partial(_inproj_body, nb=nb, rpb=rpb, seq_minor=seq_minor),
        grid=(t // tm,),
        in_specs=[row(d), mod(0), mod(1), full(g_mix), full(w_cat), full(wgk_pad), full(bgk), full(bf_pad),
                  full(w_t), full(bf_col)],
        out_specs=out_specs,
        out_shape=out_shape,
        compiler_params=_cparams(("arbitrary",)),
        name="inproj",
    )(x2, m3, m3, g_mix, w_cat, wgk_pad, bgk, bf_pad, w_t, bf_col)


def _cumsum_body(x_ref, o_ref, *, chunk, suffix):
    n = x_ref.shape[-1]
    r = lax.broadcasted_iota(i32, (chunk, chunk), 0)
    c = lax.broadcasted_iota(i32, (chunk, chunk), 1)
    tri = (r <= c).astype(f32)
    carry = jnp.zeros((x_ref.shape[1], 1), f32)
    for j in range(n // chunk):
        sl = slice(j * chunk, (j + 1) * chunk)
        cc = jnp.dot(x_ref[0, :, sl], tri, precision=HIGHEST, preferred_element_type=f32) + carry
        o_ref[0, :, sl] = cc
        carry = cc[:, chunk - 1:chunk]
    if suffix:
        o_ref[0] = o_ref[0] - carry


def _cumsum_lanes(x, *, suffix=False):
    b, h, n = x.shape
    return pl.pallas_call(
        functools.partial(_cumsum_body, chunk=128, suffix=suffix),
        grid=(b,),
        in_specs=[pl.BlockSpec((1, h, n), lambda i: (i, 0, 0))],
        out_specs=pl.BlockSpec((1, h, n), lambda i: (i, 0, 0)),
        out_shape=jax.ShapeDtypeStruct((b, h, n), f32),
        compiler_params=_cparams(("arbitrary",)),
        name="cumsum",
    )(x)


def _gla_body(q_ref, k_ref, v_ref, la_ref, s0_ref, o_ref, sout_ref, s_scr, *, chunk, cps, n_steps):
    si = pl.program_id(1)

    @pl.when(si == 0)
    def _():
        s_scr[...] = s0_ref[0].reshape(GLA_KEY_WIDTH, GLA_HEAD_V)

    r = lax.broadcasted_iota(i32, (chunk, chunk), 0)
    c = lax.broadcasted_iota(i32, (chunk, chunk), 1)
    causal = c <= r
    tril = causal.astype(bf16)
    ones = jnp.ones((chunk, GLA_HEAD_V), bf16)
    lane_head = lax.broadcasted_iota(i32, (1, GLA_KEY_WIDTH), 1) // GLA_HEAD_K
    zero = jnp.zeros((), bf16)
    mid = chunk // 2
    state = s_scr[...]
    for ch in range(cps):
        t = slice(ch * chunk, (ch + 1) * chunk)
        g = la_ref[t, :]
        g1 = g.astype(bf16)
        r1 = g - g1.astype(f32)
        g2 = r1.astype(bf16)
        g3 = (r1 - g2.astype(f32)).astype(bf16)
        b = (jnp.dot(tril, g1, preferred_element_type=f32) + jnp.dot(tril, g2, preferred_element_type=f32)
             + jnp.dot(tril, g3, preferred_element_type=f32))
        b_mid = b[mid:mid + 1, :]
        b_last = b[chunk - 1:chunk, :]
        q = q_ref[t, :]
        k = k_ref[t, :]
        qs = (q * jnp.exp(b - b_mid)).astype(bf16)
        ks = (k * jnp.exp(b_mid - b)).astype(bf16)
        qe = (q * jnp.exp(b)).astype(bf16)
        kd = (k * jnp.exp(b_last - b)).astype(bf16)
        decay = jnp.exp(lax.dot_general(g1, ones, _TN, preferred_element_type=f32)
                        + lax.dot_general(g2, ones, _TN, preferred_element_type=f32)
                        + lax.dot_general(g3, ones, _TN, preferred_element_type=f32))
        state_b = state.astype(bf16)
        upd = []
        for h in range(GLA_HEADS):
            hm = lane_head == h
            rows = slice(h * GLA_HEAD_K, (h + 1) * GLA_HEAD_K)
            cols = slice(h * GLA_HEAD_V, (h + 1) * GLA_HEAD_V)
            vh = v_ref[t, cols].astype(bf16)
            a = lax.dot_general(jnp.where(hm, qs, zero), ks, _NT, preferred_element_type=f32)
            a = jnp.where(causal, a, 0.0).astype(bf16)
            o_ref[t, cols] = (jnp.dot(jnp.where(hm, qe, zero), state_b, preferred_element_type=f32)
                              + jnp.dot(a, vh, preferred_element_type=f32))
            upd.append(lax.dot_general(kd, vh, _TN, preferred_element_type=f32)[rows, :])
        state = state * decay + jnp.concatenate(upd, axis=0)
    s_scr[...] = state

    @pl.when(si == n_steps - 1)
    def _():
        sout_ref[0] = state.reshape(GLA_HEADS, GLA_HEAD_K, GLA_HEAD_V)


def _gla(gq, gk, gv, la, s0, *, bsz, length, chunk, cps):
    n_steps = length // (chunk * cps)
    rows = chunk * cps
    row = lambda w: pl.BlockSpec((rows, w), lambda b, s: (b * n_steps + s, 0))
    st = pl.BlockSpec((1, GLA_HEADS, GLA_HEAD_K, GLA_HEAD_V), lambda b, s: (b, 0, 0, 0))
    return pl.pallas_call(
        functools.partial(_gla_body, chunk=chunk, cps=cps, n_steps=n_steps),
        grid=(bsz, n_steps),
        in_specs=[row(GLA_KEY_WIDTH), row(GLA_KEY_WIDTH), row(GLA_VALUE_WIDTH), row(GLA_KEY_WIDTH), st],
        out_specs=[row(GLA_VALUE_WIDTH), st],
        out_shape=[jax.ShapeDtypeStruct((bsz * length, GLA_VALUE_WIDTH), f32),
                   jax.ShapeDtypeStruct((bsz, GLA_HEADS, GLA_HEAD_K, GLA_HEAD_V), f32)],
        scratch_shapes=[pltpu.VMEM((GLA_KEY_WIDTH, GLA_HEAD_V), f32)],
        compiler_params=_cparams(("arbitrary", "arbitrary")),
        name="gla",
    )(gq, gk, gv, la, s0)


_HG = 4


_HGW = _HG * FOX_HEAD_DIM
_FOX_RB, _FOX_CB = 256, 256


def _foxp_body(q_ref, k_ref, v_ref, cq_ref, ck_ref, o_ref, qm_scr, vone_scr, cqb_scr, m_scr, sh_scr, acc_scr, *, t):
    qi = pl.program_id(2)
    n_tiles = k_ref.shape[1]
    lane_head = lax.broadcasted_iota(i32, (1, _HGW), 1) // FOX_HEAD_DIM

    @pl.when(qi == 0)
    def _():
        row_head = lax.broadcasted_iota(i32, (_HGW, 1), 0) // FOX_HEAD_DIM
        for kt in range(n_tiles):
            vt = v_ref[0, kt].reshape(_HGW, t)
            for j in range(_HG):
                vone_scr[j, kt] = jnp.where(row_head == j, vt, jnp.ones((), bf16))

    q = q_ref[...]
    for j in range(_HG):
        qm_scr[j] = jnp.where(lane_head == j, q, jnp.zeros((), bf16))
        cqb_scr[j] = jnp.broadcast_to(cq_ref[0, 0, :, j:j + 1], (t, 128))
    m_scr[...] = jnp.full(m_scr.shape, -jnp.inf, f32)
    acc_scr[...] = jnp.zeros(acc_scr.shape, f32)

    def logits(ki, j, rb, cb, diagonal):
        r0, c0 = rb * _FOX_RB, cb * _FOX_CB
        if diagonal and c0 > r0 + _FOX_RB - 1:
            return None
        kt = k_ref[0, ki, :, :, c0:c0 + _FOX_CB].reshape(_HGW, _FOX_CB)
        u = jnp.dot(qm_scr[j, r0:r0 + _FOX_RB, :], kt, preferred_element_type=f32) - ck_ref[0, 0, ki, j:j + 1, c0:c0 + _FOX_CB]
        if diagonal and c0 + _FOX_CB - 1 > r0:
            qpos = r0 + lax.broadcasted_iota(i32, (_FOX_RB, _FOX_CB), 0)
            kpos = c0 + lax.broadcasted_iota(i32, (_FOX_RB, _FOX_CB), 1)
            u = jnp.where(kpos <= qpos, u, -jnp.inf)
        return u

    def max_pass(ki, diagonal):
        for j in range(_HG):
            for cb in range(t // _FOX_CB):
                for rb in range(t // _FOX_RB):
                    u = logits(ki, j, rb, cb, diagonal)
                    if u is not None:
                        rows = slice(rb * _FOX_RB, (rb + 1) * _FOX_RB)
                        m = m_scr[j, rows, :]
                        for c in range(_FOX_CB // 128):
                            m = jnp.maximum(m, u[:, c * 128:(c + 1) * 128])
                        m_scr[j, rows, :] = m

    def sum_pass(ki, diagonal):
        for j in range(_HG):
            for cb in range(t // _FOX_CB):
                vt = vone_scr[j, ki, :, cb * _FOX_CB:(cb + 1) * _FOX_CB]
                for rb in range(t // _FOX_RB):
                    u = logits(ki, j, rb, cb, diagonal)
                    if u is not None:
                        rows = slice(rb * _FOX_RB, (rb + 1) * _FOX_RB)
                        p = jnp.exp(u - jnp.concatenate([sh_scr[j, rows, :]] * (_FOX_CB // 128), axis=-1))
                        acc_scr[j, rows, :] += lax.dot_general(p.astype(bf16), vt, _NT, preferred_element_type=f32)

    lax.fori_loop(0, qi, lambda ki, c: (max_pass(ki, False), c)[1], 0)
    for j in range(_HG):
        for rb in range(t // _FOX_RB):
            rows = slice(rb * _FOX_RB, (rb + 1) * _FOX_RB)
            m = m_scr[j, rows, :]
            for cb in range(t // _FOX_CB):
                u = logits(qi, j, rb, cb, True)
                if u is not None:
                    for c in range(_FOX_CB // 128):
                        m = jnp.maximum(m, u[:, c * 128:(c + 1) * 128])
            cq = cqb_scr[j, rows, :]
            m_full = jnp.max(m, axis=-1, keepdims=True) + cq
            sh_scr[j, rows, :] = m_full - cq
    lax.fori_loop(0, qi, lambda ki, c: (sum_pass(ki, False), c)[1], 0)
    sum_pass(qi, True)

    out = jnp.zeros((t, _HGW), f32)
    for j in range(_HG):
        acc = acc_scr[j]
        other = acc[:, 128:256] if j < _HG // 2 else acc[:, 0:128]
        inv = 1.0 / other
        out = jnp.where(lane_head == j, acc * jnp.concatenate([inv, inv], axis=-1), out)
    o_ref[...] = out


def _fox_prompt(fq, kt, vt, c_rows, c_cols, *, bsz, length, t=512):
    n = length // t
    kv = pl.BlockSpec((1, n, _HG, FOX_HEAD_DIM, t), lambda b, g, qi: (b, 0, g, 0, 0))
    return pl.pallas_call(
        functools.partial(_foxp_body, t=t),
        grid=(bsz, FOX_HEADS // _HG, n),
        in_specs=[pl.BlockSpec((t, _HGW), lambda b, g, qi: (b * n + qi, g)), kv, kv,
                  pl.BlockSpec((1, 1, t, _HG), lambda b, g, qi: (b, g, qi, 0)),
                  pl.BlockSpec((1, 1, n, _HG, t), lambda b, g, qi: (b, g, 0, 0, 0))],
        out_specs=pl.BlockSpec((t, _HGW), lambda b, g, qi: (b * n + qi, g)),
        out_shape=jax.ShapeDtypeStruct((bsz * length, FOX_WIDTH), f32),
        scratch_shapes=[pltpu.VMEM((_HG, t, _HGW), bf16), pltpu.VMEM((_HG, n, _HGW, t), bf16),
                        pltpu.VMEM((_HG, t, 128), f32), pltpu.VMEM((_HG, t, 128), f32),
                        pltpu.VMEM((_HG, t, 128), f32), pltpu.VMEM((_HG, t, _HGW), f32)],
        compiler_params=_cparams(("arbitrary",) * 3),
        name="fox_prompt",
    )(fq, kt, vt, c_rows, c_cols)


def _foxs_body(kc_ref, vc_ref, kn_ref, vn_ref, qbd_ref, ckp_ref, ckn_ref, cq_ref, o_ref, s_scr,
               *, past, new, kchunk):
    nrow = FOX_HEADS * new
    qbd = qbd_ref[0]
    cq = cq_ref[0]

    def per_head_rows(c8):
        return jnp.concatenate([jnp.broadcast_to(c8[h:h + 1, :], (new, c8.shape[1])) for h in range(FOX_HEADS)], axis=0)

    n_chunks = past // kchunk
    m = jnp.full((nrow, 1), -jnp.inf, f32)
    for c in range(n_chunks):
        cols = slice(c * kchunk, (c + 1) * kchunk)
        kt = kc_ref[0, :, :, cols].reshape(FOX_WIDTH, kchunk).astype(bf16)
        s = jnp.dot(qbd, kt, preferred_element_type=f32) + cq - per_head_rows(ckp_ref[0, :, cols])
        s_scr[:, cols] = s
        m = jnp.maximum(m, jnp.max(s, axis=-1, keepdims=True))
    sn = lax.dot_general(qbd, kn_ref[0].astype(bf16), _NT, preferred_element_type=f32)
    sn = sn + cq - per_head_rows(ckn_ref[0])
    qry_i = lax.broadcasted_iota(i32, (nrow, new), 0) % new
    key_j = lax.broadcasted_iota(i32, (nrow, new), 1)
    sn = jnp.where(key_j <= qry_i, sn, -jnp.inf)
    m = jnp.maximum(m, jnp.max(sn, axis=-1, keepdims=True))
    pn = jnp.exp(sn - m)
    l = jnp.sum(pn, axis=-1, keepdims=True)
    for c in range(n_chunks):
        cols = slice(c * kchunk, (c + 1) * kchunk)
        p = jnp.exp(s_scr[:, cols] - m)
        s_scr[:, cols] = p
        l = l + jnp.sum(p, axis=-1, keepdims=True)
    inv = 1.0 / l
    acc = jnp.dot((pn * inv).astype(bf16), vn_ref[0].astype(bf16), preferred_element_type=f32)
    for c in range(n_chunks):
        cols = slice(c * kchunk, (c + 1) * kchunk)
        vt = vc_ref[0, :, :, cols].reshape(FOX_WIDTH, kchunk).astype(bf16)
        acc = acc + lax.dot_general((s_scr[:, cols] * inv).astype(bf16), vt, _NT, preferred_element_type=f32)
    for h in range(FOX_HEADS):
        o_ref[0, :, h * FOX_HEAD_DIM:(h + 1) * FOX_HEAD_DIM] = (
            acc[h * new:(h + 1) * new, h * FOX_HEAD_DIM:(h + 1) * FOX_HEAD_DIM])


def _fox_sample(kc_t, vc_t, kn, vn, qbd, ckp, ckn, cq, *, kchunk=512):
    bsz, _, _, past = kc_t.shape
    new, w = kn.shape[1], kn.shape[2]
    blk = lambda a: pl.BlockSpec((1,) + a.shape[1:], lambda b: (b,) + (0,) * (a.ndim - 1))
    return pl.pallas_call(
        functools.partial(_foxs_body, past=past, new=new, kchunk=kchunk),
        grid=(bsz,),
        in_specs=[blk(kc_t), blk(vc_t), blk(kn), blk(vn), blk(qbd), blk(ckp), blk(ckn), blk(cq)],
        out_specs=pl.BlockSpec((1, new, w), lambda b: (b, 0, 0)),
        out_shape=jax.ShapeDtypeStruct((bsz, new, w), f32),
        scratch_shapes=[pltpu.VMEM((FOX_HEADS * new, past), f32)],
        compiler_params=_cparams(("arbitrary",)),
        name="fox_sample",
    )(kc_t, vc_t, kn, vn, qbd, ckp, ckn, cq)


def _route(logits_t, b_router):
    ne, tm = logits_t.shape
    gsz = ne // N_GROUPS
    shape3 = (N_GROUPS, gsz, tm)
    scores = jax.nn.sigmoid(logits_t).reshape(shape3)
    choice = scores + b_router
    im = lax.broadcasted_iota(i32, shape3, 1)
    ig = lax.broadcasted_iota(i32, (N_GROUPS, 1, tm), 0)
    ie = lax.broadcasted_iota(i32, shape3, 0) * gsz + im
    neg = -jnp.inf
    m1 = jnp.max(choice, axis=1, keepdims=True)
    first = jnp.min(jnp.where(choice == m1, im, gsz), axis=1, keepdims=True)
    m2 = jnp.max(jnp.where(im == first, neg, choice), axis=1, keepdims=True)
    cur = m1 + m2
    gsel = jnp.zeros(cur.shape, i32)
    for _ in range(TOPK_GROUPS):
        mx = jnp.max(cur, axis=0, keepdims=True)
        hit = ig == jnp.min(jnp.where(cur == mx, ig, N_GROUPS), axis=0, keepdims=True)
        gsel = jnp.where(hit, 1, gsel)
        cur = jnp.where(hit, neg, cur)
    cur = jnp.where(jnp.broadcast_to(gsel, shape3) > 0, choice, neg)
    ids, ws = [], []
    for _ in range(TOP_K):
        mx = jnp.max(jnp.max(cur, axis=1, keepdims=True), axis=0, keepdims=True)
        cand = jnp.where(cur == mx, ie, ne)
        fi = jnp.min(jnp.min(cand, axis=1, keepdims=True), axis=0, keepdims=True)
        hit = ie == fi
        w = jnp.where(hit, scores, 0.0)
        ids.append(fi.reshape(1, tm))
        ws.append(jnp.sum(jnp.sum(w, axis=1, keepdims=True), axis=0, keepdims=True).reshape(1, tm))
        cur = jnp.where(hit, neg, cur)
    tot = ws[0]
    for w in ws[1:]:
        tot = tot + w
    scale = ROUTED_SCALE / (tot + 1e-20)
    return ids, [w * scale for w in ws]


def _outproj_body(og_ref, gg_ref, of_ref, x_ref, gt1_ref, sh2_ref, sc2_ref, ggla_ref, gfox_ref, gffn_ref,
                  wout_ref, wrh_ref, wrl_ref, br_ref, x1_ref, hp_ref, idx_ref, wts_ref, *, nb, rpb):
    tm = nb * rpb
    og = og_ref[...]
    parts = []
    for h in range(GLA_HEADS):
        seg = og[:, h * GLA_HEAD_V:(h + 1) * GLA_HEAD_V]
        parts.append(seg * lax.rsqrt(jnp.mean(seg * seg, axis=-1, keepdims=True) + NORM_EPS))
    gla_n = jnp.concatenate(parts, axis=-1) * ggla_ref[...] * _silu(gg_ref[...])
    of = of_ref[...]
    fox_n = of * lax.rsqrt(jnp.mean(of * of, axis=-1, keepdims=True) + NORM_EPS) * gfox_ref[...]
    merged = jnp.concatenate([gla_n, fox_n], axis=-1).astype(bf16)
    mix = jnp.dot(merged, wout_ref[...], preferred_element_type=f32)
    x1 = (x_ref[...].reshape(nb, rpb, D_MODEL) + gt1_ref[...] * mix.reshape(nb, rpb, D_MODEL))
    x1_ref[...] = x1.reshape(tm, D_MODEL)
    x1f = x1.reshape(tm, D_MODEL)
    y = x1f * lax.rsqrt(jnp.mean(x1f * x1f, axis=-1, keepdims=True) + NORM_EPS) * gffn_ref[...]
    h2 = (y.reshape(nb, rpb, D_MODEL) * (1.0 + sc2_ref[...]) + sh2_ref[...]).reshape(tm, D_MODEL)
    h_hi = h2.astype(bf16)
    hp_ref[...] = h_hi
    h_lo = (h2 - h_hi.astype(f32)).astype(bf16)
    logits_t = (lax.dot_general(wrh_ref[...], h_hi, _NT, preferred_element_type=f32)
                + lax.dot_general(wrh_ref[...], h_lo, _NT, preferred_element_type=f32)
                + lax.dot_general(wrl_ref[...], h_hi, _NT, preferred_element_type=f32))
    ids, ws = _route(logits_t, br_ref[...])
    for j in range(TOP_K):
        idx_ref[j:j + 1, :] = ids[j]
        wts_ref[j:j + 1, :] = ws[j]


def _outproj(og, gg, of, x2, m3, g_gla4, g_fox, g_ffn, w_out, w_rt_hi, w_rt_lo, b_r, *, nb, rpb, tiles_per_batch):
    t, d = x2.shape
    tm = nb * rpb
    row = lambda w: pl.BlockSpec((tm, w), lambda i: (i, 0))
    col = pl.BlockSpec((TOP_K, tm), lambda i: (0, i))
    mod = lambda c: pl.BlockSpec((nb, 1, d), lambda i, c=c: (i // tiles_per_batch, 0, c))
    full = lambda a: pl.BlockSpec(a.shape, lambda i: (0,) * a.ndim)
    return pl.pallas_call(
        functools.partial(_outproj_body, nb=nb, rpb=rpb),
        grid=(t // tm,),
        in_specs=[row(GLA_VALUE_WIDTH), row(GLA_VALUE_WIDTH), row(FOX_WIDTH), row(d), mod(2), mod(3), mod(4),
                  full(g_gla4), full(g_fox), full(g_ffn), full(w_out), full(w_rt_hi), full(w_rt_lo), full(b_r)],
        out_specs=[row(d), row(d), col, col],
        out_shape=[jax.ShapeDtypeStruct((t, d), f32), jax.ShapeDtypeStruct((t, d), bf16),
                   jax.ShapeDtypeStruct((TOP_K, t), i32), jax.ShapeDtypeStruct((TOP_K, t), f32)],
        compiler_params=_cparams(("arbitrary",)),
        name="outproj",
    )(og, gg, of, x2, m3, m3, m3, g_gla4, g_fox, g_ffn, w_out, w_rt_hi, w_rt_lo, b_r)


def _rank_body(idx_ref, lpos_ref, lst_ref, seg_ref, dst_ref, tot_ref, tot_scr, base_scr, run_scr, *, tn):
    ph, gi = pl.program_id(0), pl.program_id(1)
    ie = lax.broadcasted_iota(i32, (N_EXPERTS, tn), 0)
    idx = idx_ref[...]
    multihot = jnp.zeros((N_EXPERTS, tn), f32)
    for j in range(TOP_K):
        multihot = multihot + (ie == idx[j:j + 1, :]).astype(f32)
    seg = jnp.maximum(jnp.ceil(jnp.sum(multihot, axis=1, keepdims=True) * (1.0 / SEG)), 1.0)
    r = lax.broadcasted_iota(i32, (N_EXPERTS, N_EXPERTS), 0)
    c = lax.broadcasted_iota(i32, (N_EXPERTS, N_EXPERTS), 1)
    strict = (c < r).astype(f32)

    def excl_cumsum(col):
        return jnp.dot(strict, jnp.broadcast_to(col, (N_EXPERTS, 128)), precision=HIGHEST,
                       preferred_element_type=f32)[:, :1]

    @pl.when(jnp.logical_and(ph == 0, gi == 0))
    def _():
        tot_scr[...] = jnp.zeros(tot_scr.shape, f32)

    @pl.when(ph == 0)
    def _():
        tot_scr[...] += seg

    @pl.when(jnp.logical_and(ph == 1, gi == 0))
    def _():
        tot = tot_scr[...]
        blocks = jnp.ceil(tot * (float(SEG) / EXPERT_ROWS))
        base_scr[...] = excl_cumsum(blocks) * float(EXPERT_ROWS // SEG)
        run_scr[...] = jnp.zeros(run_scr.shape, f32)
        tot_ref[...] = tot

    @pl.when(ph == 1)
    def _():
        lstart = excl_cumsum(seg)
        rr = lax.broadcasted_iota(i32, (tn, tn), 0)
        cc = lax.broadcasted_iota(i32, (tn, tn), 1)
        before = (rr < cc).astype(bf16)
        excl = jnp.dot(multihot.astype(bf16), before, preferred_element_type=f32)
        pos = lstart * float(SEG) + excl
        for j in range(TOP_K):
            lpos_ref[j:j + 1, :] = jnp.sum(jnp.where(ie == idx[j:j + 1, :], pos, 0.0), axis=0,
                                           keepdims=True).astype(i32)
        lst_ref[0] = lstart.astype(i32)
        seg_ref[0] = seg.astype(i32)
        dst_ref[0] = (base_scr[...] + run_scr[...]).astype(i32)
        run_scr[...] += seg


def _rank(idx, *, tn):
    _, t = idx.shape
    ng = t // tn
    table = pl.BlockSpec((1, N_EXPERTS, 1), lambda p, g: (g * p, 0, 0))
    return pl.pallas_call(
        functools.partial(_rank_body, tn=tn),
        grid=(2, ng),
        in_specs=[pl.BlockSpec((TOP_K, tn), lambda p, g: (0, g))],
        out_specs=[pl.BlockSpec((TOP_K, tn), lambda p, g: (0, g * p)), table, table, table,
                   pl.BlockSpec((N_EXPERTS, 1), lambda p, g: (0, 0))],
        out_shape=[jax.ShapeDtypeStruct((TOP_K, t), i32)] + [jax.ShapeDtypeStruct((ng, N_EXPERTS, 1), i32)] * 3
                  + [jax.ShapeDtypeStruct((N_EXPERTS, 1), f32)],
        scratch_shapes=[pltpu.VMEM((N_EXPERTS, 1), f32)] * 3,
        compiler_params=_cparams(("arbitrary", "arbitrary")),
        name="rank",
    )(idx)


def _group_rows(lst_ref, seg_ref, g):
    k = g * N_EXPERTS + N_EXPERTS - 1
    return (lst_ref[k] + seg_ref[k]) * SEG


def _start_segment_copies(lst_ref, seg_ref, dst_ref, g, make_copy):
    def per_expert(e, carry):
        k = g * N_EXPERTS + e
        make_copy(pl.multiple_of(lst_ref[k] * SEG, SEG), pl.multiple_of(dst_ref[k] * SEG, SEG),
                  seg_ref[k] * SEG).start()
        return carry

    lax.fori_loop(0, N_EXPERTS, per_expert, 0, unroll=8)


_CHUNKS_PER_TRIP = 2


def _placement_rows(lpos, vals, r0):
    rid = lax.broadcasted_iota(i32, (PCHUNK, GROUP), 0).astype(f32).astype(bf16)
    rel = (lpos - r0).astype(f32).astype(bf16)
    p = jnp.zeros((PCHUNK, GROUP), bf16)
    for j in range(TOP_K):
        v = jnp.ones((), bf16) if vals is None else vals[j:j + 1, :]
        p = jnp.where(rel[j:j + 1, :] == rid, v, p)
    return p


def _chunk_trips(n_rows):
    return (n_rows + _CHUNKS_PER_TRIP * PCHUNK - 1) // (_CHUNKS_PER_TRIP * PCHUNK)


def _dispatch_body(lst_ref, seg_ref, dst_ref, tail0_ref, tailn_ref, misc_ref, lpos_ref, hp_ref, hs_ref, xs_hbm,
                   stage, zeros, sem, zsem, *, ngp, ng, n_blocks):
    g = pl.program_id(0)
    slot = g % 2
    h = jnp.where(g < ngp, hp_ref[...], hs_ref[...])
    lpos = lpos_ref[...]

    def trip(c, carry):
        for k in range(_CHUNKS_PER_TRIP):
            r0 = pl.multiple_of((c * _CHUNKS_PER_TRIP + k) * PCHUNK, PCHUNK)
            p = _placement_rows(lpos, None, r0)
            stage[slot, pl.ds(r0, PCHUNK), :] = jnp.dot(p, h, preferred_element_type=f32).astype(bf16)
        return carry

    lax.fori_loop(0, _chunk_trips(_group_rows(lst_ref, seg_ref, g)), trip, 0)
    _start_segment_copies(lst_ref, seg_ref, dst_ref, g, lambda s, d, n: pltpu.make_async_copy(
        stage.at[slot, pl.ds(s, n), :], xs_hbm.at[pl.ds(d, n), :], sem.at[slot]))

    def wait_group(gg, sl):
        n = _group_rows(lst_ref, seg_ref, gg)
        pltpu.make_async_copy(stage.at[sl, pl.ds(0, n), :], xs_hbm.at[pl.ds(0, n), :], sem.at[sl]).wait()

    @pl.when(g > 0)
    def _():
        wait_group(g - 1, 1 - slot)

    @pl.when(g == ng - 1)
    def _():
        wait_group(g, slot)
        zeros[...] = jnp.zeros(zeros.shape, bf16)

        def per_expert(e, carry):
            n = tailn_ref[e] * SEG

            @pl.when(n > 0)
            def _():
                pltpu.make_async_copy(zeros.at[pl.ds(0, n), :],
                                      xs_hbm.at[pl.ds(pl.multiple_of(tail0_ref[e] * SEG, SEG), n), :], zsem).start()

            return carry

        lax.fori_loop(0, N_EXPERTS, per_expert, 0)
        n_used = misc_ref[0]

        def per_block(b, carry):
            pltpu.make_async_copy(zeros, xs_hbm.at[pl.ds(pl.multiple_of(b * EXPERT_ROWS, EXPERT_ROWS), EXPERT_ROWS), :],
                                  zsem).start()
            return carry

        lax.fori_loop(n_used, n_blocks, per_block, 0)
        n_zero = misc_ref[1] * SEG + (n_blocks - n_used) * EXPERT_ROWS

        @pl.when(n_zero > 0)
        def _():
            pltpu.make_async_copy(xs_hbm.at[pl.ds(0, n_zero), :], xs_hbm.at[pl.ds(0, n_zero), :], zsem).wait()


def _dispatch(tables, tail0, tailn, misc, lpos, hp, hs, *, n_blocks):
    ngp, ngs = hp.shape[0] // GROUP, hs.shape[0] // GROUP
    ng = ngp + ngs
    d = hp.shape[1]
    return pl.pallas_call(
        functools.partial(_dispatch_body, ngp=ngp, ng=ng, n_blocks=n_blocks),
        grid_spec=pltpu.PrefetchScalarGridSpec(
            num_scalar_prefetch=6,
            grid=(ng,),
            in_specs=[pl.BlockSpec((TOP_K, GROUP), lambda g, *_: (0, g)),
                      pl.BlockSpec((GROUP, d), lambda g, *_: (jnp.minimum(g, ngp - 1), 0)),
                      pl.BlockSpec((GROUP, d), lambda g, *_: (jnp.maximum(g - ngp, 0), 0))],
            out_specs=pl.BlockSpec(memory_space=pl.ANY),
            scratch_shapes=[pltpu.VMEM((2, STAGE_ROWS, d), bf16), pltpu.VMEM((EXPERT_ROWS, d), bf16),
                            pltpu.SemaphoreType.DMA((2,)), pltpu.SemaphoreType.DMA]),
        out_shape=jax.ShapeDtypeStruct((n_blocks * EXPERT_ROWS, d), bf16),
        compiler_params=_cparams(("arbitrary",)),
        name="dispatch",
    )(*tables, tail0, tailn, misc, lpos, hp, hs)


def _experts_body(be_ref, nu_ref, xs_ref, wg_ref, wu_ref, wd_ref, y_ref, wg_s, wu_s, wd_s):
    i = pl.program_id(0)
    changed = jnp.logical_or(i == 0, be_ref[i] != be_ref[jnp.maximum(i - 1, 0)])

    @pl.when(changed)
    def _():
        wg_s[...] = wg_ref[0].astype(bf16)
        wu_s[...] = wu_ref[0].astype(bf16)
        wd_s[...] = wd_ref[0].astype(bf16)

    @pl.when(i < nu_ref[0])
    def _():
        x = xs_ref[...]
        g = jnp.dot(x, wg_s[...], preferred_element_type=f32)
        u = jnp.dot(x, wu_s[...], preferred_element_type=f32)
        a = (_silu(g) * u).astype(bf16)
        y_ref[...] = jnp.dot(a, wd_s[...], preferred_element_type=f32).astype(bf16)

    @pl.when(i >= nu_ref[0])
    def _():
        y_ref[...] = jnp.zeros(y_ref.shape, bf16)


def _experts(blk_e, n_used, xs, w_gate, w_up, w_down, *, block_rows):
    n_rows, d = xs.shape
    n_blocks = n_rows // block_rows
    de = w_gate.shape[2]
    return pl.pallas_call(
        _experts_body,
        grid_spec=pltpu.PrefetchScalarGridSpec(
            num_scalar_prefetch=2,
            grid=(n_blocks,),
            in_specs=[pl.BlockSpec((block_rows, d), lambda i, be, nu: (jnp.minimum(i, nu[0] - 1), 0)),
                      pl.BlockSpec((1, d, de), lambda i, be, nu: (be[i], 0, 0)),
                      pl.BlockSpec((1, d, de), lambda i, be, nu: (be[i], 0, 0)),
                      pl.BlockSpec((1, de, d), lambda i, be, nu: (be[i], 0, 0))],
            out_specs=pl.BlockSpec((block_rows, d), lambda i, be, nu: (i, 0)),
            scratch_shapes=[pltpu.VMEM((d, de), bf16), pltpu.VMEM((d, de), bf16), pltpu.VMEM((de, d), bf16)]),
        out_shape=jax.ShapeDtypeStruct((n_rows, d), bf16),
        compiler_params=_cparams(("arbitrary",)),
        name="experts",
    )(blk_e, n_used, xs, w_gate, w_up, w_down)


def _combine_body(lst_ref, seg_ref, dst_ref, lpos_ref, wts_ref, h_ref, x1_ref, gt2_ref, wsg_ref, wsu_ref, wsd_ref,
                  gfin_ref, y_hbm, o_ref, ybuf, acc, sem, *, nb, rpb, g0, ng):
    i = pl.program_id(0)
    g = g0 + i
    slot = i % 2

    def start_gather(gg, sl):
        _start_segment_copies(lst_ref, seg_ref, dst_ref, gg, lambda s, d, n: pltpu.make_async_copy(
            y_hbm.at[pl.ds(d, n), :], ybuf.at[sl, pl.ds(s, n), :], sem.at[sl]))

    @pl.when(i == 0)
    def _():
        ybuf[...] = jnp.zeros(ybuf.shape, bf16)
        start_gather(g, slot)

    @pl.when(i + 1 < ng)
    def _():
        start_gather(g + 1, 1 - slot)

    h = h_ref[...]
    gate = jnp.dot(h, wsg_ref[...], preferred_element_type=f32)
    up = jnp.dot(h, wsu_ref[...], preferred_element_type=f32)
    acc[...] = jnp.dot((_silu(gate) * up).astype(bf16), wsd_ref[...], preferred_element_type=f32)

    n_rows = _group_rows(lst_ref, seg_ref, g)
    pltpu.make_async_copy(y_hbm.at[pl.ds(0, n_rows), :], ybuf.at[slot, pl.ds(0, n_rows), :], sem.at[slot]).wait()

    lpos = lpos_ref[...]
    wts = wts_ref[...].astype(bf16)

    def trip(c, carry):
        for k in range(_CHUNKS_PER_TRIP):
            r0 = pl.multiple_of((c * _CHUNKS_PER_TRIP + k) * PCHUNK, PCHUNK)
            wt = _placement_rows(lpos, wts, r0)
            acc[...] += lax.dot_general(wt, ybuf[slot, pl.ds(r0, PCHUNK), :], _TN, preferred_element_type=f32)
        return carry

    lax.fori_loop(0, _chunk_trips(n_rows), trip, 0)
    out = (x1_ref[...].reshape(nb, rpb, D_MODEL) + gt2_ref[...] * acc[...].reshape(nb, rpb, D_MODEL))
    out = out.reshape(GROUP, D_MODEL)
    o_ref[...] = out * lax.rsqrt(jnp.mean(out * out, axis=-1, keepdims=True) + NORM_EPS) * gfin_ref[...]


def _combine(tables, lpos_rows, wts_rows, y, h2, x1, m3, w_sg, w_su, w_sd, g_fin, *, nb, rpb, tiles_per_batch, g0):
    t, d = x1.shape
    assert nb * rpb == GROUP
    ng = t // GROUP
    row = lambda w: pl.BlockSpec((GROUP, w), lambda i, *_: (i, 0))
    grow = lambda w: pl.BlockSpec((w, GROUP), lambda i, *_: (0, g0 + i))
    full = lambda a: pl.BlockSpec(a.shape, lambda i, *_: (0,) * a.ndim)
    return pl.pallas_call(
        functools.partial(_combine_body, nb=nb, rpb=rpb, g0=g0, ng=ng),
        grid_spec=pltpu.PrefetchScalarGridSpec(
            num_scalar_prefetch=3,
            grid=(ng,),
            in_specs=[grow(TOP_K), grow(TOP_K), row(d), row(d),
                      pl.BlockSpec((nb, 1, d), lambda i, *_: (i // tiles_per_batch, 0, 5)),
                      full(w_sg), full(w_su), full(w_sd), full(g_fin), pl.BlockSpec(memory_space=pl.ANY)],
            out_specs=row(d),
            scratch_shapes=[pltpu.VMEM((2, STAGE_ROWS, d), bf16), pltpu.VMEM((GROUP, d), f32),
                            pltpu.SemaphoreType.DMA((2,))]),
        out_shape=jax.ShapeDtypeStruct((t, d), f32),
        compiler_params=_cparams(("arbitrary",)),
        name="combine",
    )(*tables, lpos_rows, wts_rows, h2, x1, m3, w_sg, w_su, w_sd, g_fin, y)


_GLA_CHUNKS_PER_STEP = 4


def kernel(x_prompt, x_sample, state_gla, cache_fox_k, cache_fox_v, cache_fox_logf, c_prompt, c_sample, w_ada, b_ada, g_norm_mix, g_norm_ffn, w_in, w_gla_gk2, b_gla_gk, g_gla_out, b_fox_f, g_fox_out, w_out, w_router, b_router, w_exp_gate, w_exp_up, w_exp_down, w_sh_gate, w_sh_up, w_sh_down, g_final):
    assert w_ada.shape[0] == 1, "single-layer trunk"
    bp, lp, d = x_prompt.shape
    bs, ls, _ = x_sample.shape
    past = cache_fox_k.shape[2]
    tp, ts = bp * lp, bs * ls

    wi = w_in[0]
    o_glr = 2 * GLA_KEY_WIDTH + 2 * GLA_VALUE_WIDTH
    o_fq = o_glr + GLA_LOW_RANK
    o_ff = o_fq + 3 * FOX_WIDTH
    small = jnp.concatenate([wi[:, o_ff:o_ff + FOX_HEADS], wi[:, o_glr:o_fq],
                             jnp.zeros((d, 128 - FOX_HEADS - GLA_LOW_RANK), f32)], axis=1)
    w_cat = jnp.concatenate([wi[:, :o_glr], wi[:, o_fq:o_ff], small], axis=1).astype(bf16)
    w_t = jnp.concatenate([wi[:, o_fq + FOX_WIDTH:o_ff + FOX_HEADS], jnp.zeros((d, 16 - FOX_HEADS), f32)],
                          axis=1).T.astype(bf16)
    wgk_pad = jnp.zeros((128, GLA_KEY_WIDTH), f32).at[FOX_HEADS:FOX_HEADS + GLA_LOW_RANK].set(w_gla_gk2[0]).astype(bf16)
    bgk = b_gla_gk[0].reshape(1, GLA_KEY_WIDTH)
    bf_pad = jnp.zeros((1, 128), f32).at[0, :FOX_HEADS].set(b_fox_f[0])
    bf_col = jnp.zeros((16, 1), f32).at[:FOX_HEADS, 0].set(b_fox_f[0])
    g_mix = g_norm_mix[0].reshape(1, d)
    g_ffn = g_norm_ffn[0].reshape(1, d)
    g_gla4 = jnp.tile(g_gla_out[0], GLA_HEADS).reshape(1, GLA_VALUE_WIDTH)
    g_fox = g_fox_out[0].reshape(1, FOX_WIDTH)
    g_fin = g_final.reshape(1, d)
    w_out_b = w_out[0].astype(bf16)
    w_rt = w_router[0].T
    w_rt_hi = w_rt.astype(bf16)
    w_rt_lo = (w_rt - w_rt_hi.astype(f32)).astype(bf16)
    b_r = b_router[0].reshape(N_GROUPS, N_EXPERTS // N_GROUPS, 1)
    w_sg, w_su, w_sd = w_sh_gate[0].astype(bf16), w_sh_up[0].astype(bf16), w_sh_down[0].astype(bf16)

    m_all = _ada(jnp.concatenate([c_prompt, c_sample], axis=0), w_ada[0], b_ada[0])
    m3p = m_all[:bp].reshape(bp, 1, 6 * d)
    m3s = m_all[bp:].reshape(bs, 1, 6 * d)

    groups = []
    for (x, m3, bsz, length) in ((x_prompt, m3p, bp, lp), (x_sample, m3s, bs, ls)):
        if length >= GROUP:
            nb, rpb_in, tpb_in = 1, 512, length // 512
            rpb, tpb = GROUP, length // GROUP
        else:
            nb, rpb_in, tpb_in = GROUP // length, length, 1
            rpb, tpb = length, 1
        groups.append(dict(x2=x.reshape(bsz * length, d), m3=m3, bsz=bsz, length=length,
                           nb=nb, rpb_in=rpb_in, tpb_in=tpb_in, rpb=rpb, tpb=tpb))

    gp, gs = groups
    (gp["gq"], gp["gk"], gp["gv"], gp["gg"], gp["la"], gp["fq"], kt_p, vt_p, ktb_p, vtb_p, lft_p) = _inproj(
        gp["x2"], gp["m3"], g_mix, w_cat, wgk_pad, bgk, bf_pad, w_t, bf_col,
        nb=gp["nb"], rpb=gp["rpb_in"], tiles_per_batch=gp["tpb_in"], seq_minor=True)
    (gs["gq"], gs["gk"], gs["gv"], gs["gg"], gs["la"], gs["fq"], fk_s, fv_s, lf_s) = _inproj(
        gs["x2"], gs["m3"], g_mix, w_cat, wgk_pad, bgk, bf_pad, w_t, bf_col,
        nb=gs["nb"], rpb=gs["rpb_in"], tiles_per_batch=gs["tpb_in"], seq_minor=False)

    s0p = jnp.zeros((bp, GLA_HEADS, GLA_HEAD_K, GLA_HEAD_V), f32)
    gp["og"], sp_new = _gla(gp["gq"], gp["gk"], gp["gv"], gp["la"], s0p, bsz=bp, length=lp, chunk=GLA_CHUNK,
                            cps=_GLA_CHUNKS_PER_STEP)
    gs["og"], ss_new = _gla(gs["gq"], gs["gk"], gs["gv"], gs["la"], state_gla[0], bsz=bs, length=ls, chunk=ls, cps=1)

    c_p = _cumsum_lanes(lft_p)
    c_grp = c_p.reshape(bp, FOX_HEADS // _HG, _HG, lp)
    c_rows = c_grp.transpose(0, 1, 3, 2)
    t_att = gp["rpb_in"]
    c_cols = c_grp.reshape(bp, FOX_HEADS // _HG, _HG, lp // t_att, t_att).transpose(0, 1, 3, 2, 4)
    gp["of"] = _fox_prompt(gp["fq"], ktb_p, vtb_p, c_rows, c_cols, bsz=bp, length=lp, t=t_att)

    lf_s = lf_s.reshape(bs, ls, FOX_HEADS)
    lpad = -(-ls // 128) * 128
    c_s = _cumsum_lanes(jnp.pad(lf_s.transpose(0, 2, 1), ((0, 0), (0, 0), (0, lpad - ls))))[:, :, :ls]
    ck_past = _cumsum_lanes(cache_fox_logf[0].astype(f32).transpose(0, 2, 1), suffix=True)
    fq_s = gs["fq"].reshape(bs, ls, FOX_HEADS, FOX_HEAD_DIM)
    eye_h = jnp.eye(FOX_HEADS, dtype=bf16)
    qbd = jnp.einsum("bihd,hg->bhigd", fq_s, eye_h).reshape(bs, FOX_HEADS * ls, FOX_WIDTH)
    of_s = _fox_sample(cache_fox_k[0].transpose(0, 2, 3, 1), cache_fox_v[0].transpose(0, 2, 3, 1),
                       fk_s.reshape(bs, ls, FOX_WIDTH), fv_s.reshape(bs, ls, FOX_WIDTH), qbd,
                       ck_past, c_s, c_s.reshape(bs, FOX_HEADS * ls, 1))
    gs["of"] = of_s.reshape(ts, FOX_WIDTH)

    for gr in groups:
        gr["x1"], gr["hp"], gr["idx"], gr["wts"] = _outproj(
            gr["og"], gr["gg"], gr["of"], gr["x2"], gr["m3"], g_gla4, g_fox, g_ffn, w_out_b, w_rt_hi, w_rt_lo, b_r,
            nb=gr["nb"] if gr["length"] < GROUP else 1, rpb=gr["rpb"], tiles_per_batch=gr["tpb"])

    t_all = tp + ts
    n_groups = t_all // GROUP
    tiles_per_block = EXPERT_ROWS // SEG
    n_blocks = -(-(t_all * TOP_K + n_groups * N_EXPERTS * SEG) // EXPERT_ROWS) + N_EXPERTS
    idx_all = jnp.concatenate([gp["idx"], gs["idx"]], axis=1)
    lpos, lst, seg, dst, tot = _rank(idx_all, tn=GROUP)
    tables = tuple(a.reshape(n_groups * N_EXPERTS) for a in (lst, seg, dst))
    tot_e = tot[:, 0].astype(i32)
    blocks_per_e = (tot_e + tiles_per_block - 1) // tiles_per_block
    blk_end = jnp.cumsum(blocks_per_e)
    n_used = blk_end[-1:]
    blk = jnp.minimum(jnp.arange(n_blocks, dtype=i32), n_used[0] - 1)
    blk_e = jnp.minimum(jnp.sum((blk_end[None, :] <= blk[:, None]).astype(i32), axis=1), N_EXPERTS - 1)
    tail0 = (blk_end - blocks_per_e) * tiles_per_block + tot_e
    tailn = blocks_per_e * tiles_per_block - tot_e
    misc = jnp.stack([n_used[0], jnp.sum(tailn)]).astype(i32)
    xs = _dispatch(tables, tail0, tailn, misc, lpos, gp["hp"], gs["hp"], n_blocks=n_blocks)
    y = _experts(blk_e, n_used.astype(i32), xs, w_exp_gate[0], w_exp_up[0], w_exp_down[0], block_rows=EXPERT_ROWS)

    wts_all = jnp.concatenate([gp["wts"], gs["wts"]], axis=1)
    outs = []
    for gr, g0 in ((gp, 0), (gs, tp // GROUP)):
        outs.append(_combine(tables, lpos, wts_all, y, gr["hp"], gr["x1"], gr["m3"], w_sg, w_su, w_sd, g_fin,
                             nb=gr["nb"] if gr["length"] < GROUP else 1, rpb=gr["rpb"],
                             tiles_per_batch=gr["tpb"], g0=g0))
    y_prompt = outs[0].reshape(bp, lp, d)
    y_sample = outs[1].reshape(bs, ls, d)

    return (y_prompt, y_sample, sp_new[None], ss_new[None],
            kt_p.transpose(0, 3, 1, 2)[None], vt_p.transpose(0, 3, 1, 2)[None], lft_p.transpose(0, 2, 1)[None],
            fk_s.reshape(1, bs, ls, FOX_HEADS, FOX_HEAD_DIM), fv_s.reshape(1, bs, ls, FOX_HEADS, FOX_HEAD_DIM),
            lf_s[None])
```

```python
import functools

import jax
import jax.numpy as jnp
from jax import lax
from jax.experimental import pallas as pl
from jax.experimental.pallas import tpu as pltpu

f32, bf16, i32 = jnp.float32, jnp.bfloat16, jnp.int32
HIGHEST = lax.Precision.HIGHEST

D_MODEL = 1024
GLA_HEADS, GLA_HEAD_K, GLA_HEAD_V = 4, 64, 128
GLA_KEY_WIDTH, GLA_VALUE_WIDTH, GLA_LOW_RANK = 256, 512, 16
GLA_GATE_NORMALIZER = 16.0
GLA_CHUNK = 64
FOX_HEADS, FOX_HEAD_DIM, FOX_WIDTH = 8, 64, 512
N_EXPERTS, N_GROUPS, TOPK_GROUPS, TOP_K = 64, 8, 4, 8
D_EXPERT = 256
ROUTED_SCALE = 2.5
NORM_EPS = 1e-6
EXPERT_ROWS = 1024
GROUP = 256
SEG = 16
PCHUNK = 256
STAGE_ROWS = -(-(TOP_K * GROUP + N_EXPERTS * SEG) // (2 * PCHUNK)) * (2 * PCHUNK)
TILES_PER_BLOCK = EXPERT_ROWS // SEG
VMEM_LIMIT = 56 * 1024 * 1024


def _cparams(sem):
    return pltpu.CompilerParams(dimension_semantics=sem, vmem_limit_bytes=VMEM_LIMIT)


def _log_sigmoid(x):
    return jnp.minimum(x, 0.0) - jnp.log1p(jnp.exp(-jnp.abs(x)))


def _silu(x):
    return x * jax.nn.sigmoid(x)


def _ada_body(c_ref, w_ref, b_ref, o_ref):
    a = _silu(c_ref[...]).astype(bf16)
    o_ref[...] = jnp.dot(a, w_ref[...].astype(bf16), preferred_element_type=f32) + b_ref[...]


def _ada(c_all, w_ada, b_ada):
    nb, d = c_all.shape
    n = w_ada.shape[1]
    tn = 768
    return pl.pallas_call(
        _ada_body,
        grid=(n // tn,),
        in_specs=[pl.BlockSpec((nb, d), lambda j: (0, 0)),
                  pl.BlockSpec((d, tn), lambda j: (0, j)),
                  pl.BlockSpec((1, tn), lambda j: (0, j))],
        out_specs=pl.BlockSpec((nb, tn), lambda j: (0, j)),
        out_shape=jax.ShapeDtypeStruct((nb, n), f32),
        compiler_params=_cparams(("arbitrary",)),
        name="ada",
    )(c_all, w_ada, b_ada.reshape(1, n))


_C_GQ, _C_GK, _C_GV, _C_GG, _C_FQ, _C_FK, _C_FV, _C_SM, _C_END = 0, 256, 512, 1024, 1536, 2048, 2560, 3072, 3200


_NT = (((1,), (1,)), ((), ()))
_TN = (((0,), (0,)), ((), ()))


def _inproj_body(x_ref, sh_ref, sc_ref, g_ref, w_ref, wgk_ref, bgk_ref, bf_ref, wt_ref, bfc_ref,
                 gq_ref, gk_ref, gv_ref, gg_ref, la_ref, fq_ref, k_ref, v_ref, *rest, nb, rpb, seq_minor):
    tm = nb * rpb
    x = x_ref[...]
    y = x * lax.rsqrt(jnp.mean(x * x, axis=-1, keepdims=True) + NORM_EPS) * g_ref[...]
    h = (y.reshape(nb, rpb, D_MODEL) * (1.0 + sc_ref[...]) + sh_ref[...]).reshape(tm, D_MODEL).astype(bf16)

    def proj(a, b):
        return jnp.dot(h, w_ref[:, a:b], preferred_element_type=f32)

    gq_ref[...] = proj(_C_GQ, _C_GK) * (GLA_HEAD_K ** -0.5)
    gk_ref[...] = proj(_C_GK, _C_GV)
    gv_ref[...] = proj(_C_GV, _C_GG)
    gg_ref[...] = proj(_C_GG, _C_FQ)
    fq_ref[...] = (proj(_C_FQ, _C_FK) * (FOX_HEAD_DIM ** -0.5)).astype(bf16)
    zs = proj(_C_SM, _C_END)
    ga = jnp.dot(zs.astype(bf16), wgk_ref[...], preferred_element_type=f32) + bgk_ref[...]
    la_ref[...] = _log_sigmoid(ga) * (1.0 / GLA_GATE_NORMALIZER)
    if seq_minor:
        kb_ref, vb_ref, lf_ref = rest
        shape4 = (FOX_HEADS, FOX_HEAD_DIM, tm)
        kt = lax.dot_general(wt_ref[0:FOX_WIDTH, :], h, _NT, preferred_element_type=f32)
        k_ref[0] = kt.reshape(shape4)
        kb_ref[0, 0] = kt.astype(bf16).reshape(shape4)
        vt = lax.dot_general(wt_ref[FOX_WIDTH:2 * FOX_WIDTH, :], h, _NT, preferred_element_type=f32)
        v_ref[0] = vt.reshape(shape4)
        vb_ref[0, 0] = vt.astype(bf16).reshape(shape4)
        zf = lax.dot_general(wt_ref[2 * FOX_WIDTH:, :], h, _NT, preferred_element_type=f32)
        lf_ref[0] = _log_sigmoid(zf + bfc_ref[...])[:FOX_HEADS, :]
    else:
        (lf_ref,) = rest
        k_ref[...] = proj(_C_FK, _C_FV)
        v_ref[...] = proj(_C_FV, _C_SM)
        lf_ref[...] = _log_sigmoid(zs + bf_ref[...])[:, :FOX_HEADS]


def _inproj(x2, m3, g_mix, w_cat, wgk_pad, bgk, bf_pad, w_t, bf_col, *, nb, rpb, tiles_per_batch, seq_minor):
    t, d = x2.shape
    tm = nb * rpb
    tpb = tiles_per_batch
    bsz = t // (tm * tpb) if seq_minor else None
    row = lambda w: pl.BlockSpec((tm, w), lambda i: (i, 0))
    mod = lambda c: pl.BlockSpec((nb, 1, d), lambda i, c=c: (i // tpb, 0, c))
    full = lambda a: pl.BlockSpec(a.shape, lambda i: (0,) * a.ndim)
    widths = [(GLA_KEY_WIDTH, f32), (GLA_KEY_WIDTH, f32), (GLA_VALUE_WIDTH, f32), (GLA_VALUE_WIDTH, f32),
              (GLA_KEY_WIDTH, f32), (FOX_WIDTH, bf16)]
    out_specs = [row(w) for w, _ in widths]
    out_shape = [jax.ShapeDtypeStruct((t, w), dt) for w, dt in widths]
    if seq_minor:
        length = tm * tpb
        hd = pl.BlockSpec((1, FOX_HEADS, FOX_HEAD_DIM, tm), lambda i: (i // tpb, 0, 0, i % tpb))
        hd_tiles = pl.BlockSpec((1, 1, FOX_HEADS, FOX_HEAD_DIM, tm), lambda i: (i // tpb, i % tpb, 0, 0, 0))
        out_specs += [hd, hd, hd_tiles, hd_tiles, pl.BlockSpec((1, FOX_HEADS, tm), lambda i: (i // tpb, 0, i % tpb))]
        out_shape += [jax.ShapeDtypeStruct((bsz, FOX_HEADS, FOX_HEAD_DIM, length), f32)] * 2
        out_shape += [jax.ShapeDtypeStruct((bsz, tpb, FOX_HEADS, FOX_HEAD_DIM, tm), bf16)] * 2
        out_shape += [jax.ShapeDtypeStruct((bsz, FOX_HEADS, length), f32)]
    else:
        out_specs += [row(FOX_WIDTH), row(FOX_WIDTH), row(FOX_HEADS)]
        out_shape += [jax.ShapeDtypeStruct((t, w), f32) for w in (FOX_WIDTH, FOX_WIDTH, FOX_HEADS)]
    return pl.pallas_call(
        functools.partial(_inproj_body, nb=nb, rpb=rpb, seq_minor=seq_minor),
        grid=(t // tm,),
        in_specs=[row(d), mod(0), mod(1), full(g_mix), full(w_cat), full(wgk_pad), full(bgk), full(bf_pad),
                  full(w_t), full(bf_col)],
        out_specs=out_specs,
        out_shape=out_shape,
        compiler_params=_cparams(("arbitrary",)),
        name="inproj",
    )(x2, m3, m3, g_mix, w_cat, wgk_pad, bgk, bf_pad, w_t, bf_col)


def _cumsum_body(x_ref, o_ref, *, chunk, suffix):
    n = x_ref.shape[-1]
    r = lax.broadcasted_iota(i32, (chunk, chunk), 0)
    c = lax.broadcasted_iota(i32, (chunk, chunk), 1)
    tri = (r <= c).astype(bf16)
    local = []
    for j in range(n // chunk):
        x = x_ref[0, :, j * chunk:(j + 1) * chunk]
        x1 = x.astype(bf16)
        r1 = x - x1.astype(f32)
        x2 = r1.astype(bf16)
        x3 = (r1 - x2.astype(f32)).astype(bf16)
        local.append(jnp.dot(x1, tri, preferred_element_type=f32) + jnp.dot(x2, tri, preferred_element_type=f32)
                     + jnp.dot(x3, tri, preferred_element_type=f32))
    offsets = [jnp.zeros((x_ref.shape[1], 1), f32)]
    for j in range(n // chunk):
        offsets.append(offsets[-1] + local[j][:, chunk - 1:chunk])
    total = offsets[-1] if suffix else 0.0
    for j in range(n // chunk):
        o_ref[0, :, j * chunk:(j + 1) * chunk] = local[j] + (offsets[j] - total)


def _cumsum_lanes(x, *, suffix=False):
    b, h, n = x.shape
    return pl.pallas_call(
        functools.partial(_cumsum_body, chunk=128, suffix=suffix),
        grid=(b,),
        in_specs=[pl.BlockSpec((1, h, n), lambda i: (i, 0, 0))],
        out_specs=pl.BlockSpec((1, h, n), lambda i: (i, 0, 0)),
        out_shape=jax.ShapeDtypeStruct((b, h, n), f32),
        compiler_params=_cparams(("arbitrary",)),
        name="cumsum",
    )(x)


def _gla_body(q_ref, k_ref, v_ref, la_ref, s0_ref, o_ref, sout_ref, s_scr, *, chunk, cps, n_steps):
    si = pl.program_id(1)

    @pl.when(si == 0)
    def _():
        s_scr[...] = s0_ref[0].reshape(GLA_KEY_WIDTH, GLA_HEAD_V)

    r = lax.broadcasted_iota(i32, (chunk, chunk), 0)
    c = lax.broadcasted_iota(i32, (chunk, chunk), 1)
    causal = c <= r
    tril = causal.astype(bf16)
    ones = jnp.ones((chunk, GLA_HEAD_V), bf16)
    lane_head = lax.broadcasted_iota(i32, (1, GLA_KEY_WIDTH), 1) // GLA_HEAD_K
    zero = jnp.zeros((), bf16)
    mid = chunk // 2
    state = s_scr[...]
    for ch in range(cps):
        t = slice(ch * chunk, (ch + 1) * chunk)
        g = la_ref[t, :]
        g1 = g.astype(bf16)
        r1 = g - g1.astype(f32)
        g2 = r1.astype(bf16)
        g3 = (r1 - g2.astype(f32)).astype(bf16)
        b = (jnp.dot(tril, g1, preferred_element_type=f32) + jnp.dot(tril, g2, preferred_element_type=f32)
             + jnp.dot(tril, g3, preferred_element_type=f32))
        b_mid = b[mid:mid + 1, :]
        b_last = b[chunk - 1:chunk, :]
        q = q_ref[t, :]
        k = k_ref[t, :]
        qs = (q * jnp.exp(b - b_mid)).astype(bf16)
        ks = (k * jnp.exp(b_mid - b)).astype(bf16)
        qe = (q * jnp.exp(b)).astype(bf16)
        kd = (k * jnp.exp(b_last - b)).astype(bf16)
        decay = jnp.exp(lax.dot_general(g1, ones, _TN, preferred_element_type=f32)
                        + lax.dot_general(g2, ones, _TN, preferred_element_type=f32)
                        + lax.dot_general(g3, ones, _TN, preferred_element_type=f32))
        state_b = state.astype(bf16)
        upd = []
        for h in range(GLA_HEADS):
            hm = lane_head == h
            rows = slice(h * GLA_HEAD_K, (h + 1) * GLA_HEAD_K)
            cols = slice(h * GLA_HEAD_V, (h + 1) * GLA_HEAD_V)
            vh = v_ref[t, cols].astype(bf16)
            a = lax.dot_general(jnp.where(hm, qs, zero), ks, _NT, preferred_element_type=f32)
            a = jnp.where(causal, a, 0.0).astype(bf16)
            o_ref[t, cols] = (jnp.dot(jnp.where(hm, qe, zero), state_b, preferred_element_type=f32)
                              + jnp.dot(a, vh, preferred_element_type=f32))
            upd.append(lax.dot_general(kd, vh, _TN, preferred_element_type=f32)[rows, :])
        state = state * decay + jnp.concatenate(upd, axis=0)
    s_scr[...] = state

    @pl.when(si == n_steps - 1)
    def _():
        sout_ref[0] = state.reshape(GLA_HEADS, GLA_HEAD_K, GLA_HEAD_V)


def _gla(gq, gk, gv, la, s0, *, bsz, length, chunk, cps):
    n_steps = length // (chunk * cps)
    rows = chunk * cps
    row = lambda w: pl.BlockSpec((rows, w), lambda b, s: (b * n_steps + s, 0))
    st = pl.BlockSpec((1, GLA_HEADS, GLA_HEAD_K, GLA_HEAD_V), lambda b, s: (b, 0, 0, 0))
    return pl.pallas_call(
        functools.partial(_gla_body, chunk=chunk, cps=cps, n_steps=n_steps),
        grid=(bsz, n_steps),
        in_specs=[row(GLA_KEY_WIDTH), row(GLA_KEY_WIDTH), row(GLA_VALUE_WIDTH), row(GLA_KEY_WIDTH), st],
        out_specs=[row(GLA_VALUE_WIDTH), st],
        out_shape=[jax.ShapeDtypeStruct((bsz * length, GLA_VALUE_WIDTH), f32),
                   jax.ShapeDtypeStruct((bsz, GLA_HEADS, GLA_HEAD_K, GLA_HEAD_V), f32)],
        scratch_shapes=[pltpu.VMEM((GLA_KEY_WIDTH, GLA_HEAD_V), f32)],
        compiler_params=_cparams(("arbitrary", "arbitrary")),
        name="gla",
    )(gq, gk, gv, la, s0)


_HG = 4


_HGW = _HG * FOX_HEAD_DIM
_FOX_RB, _FOX_CB = 256, 256


def _foxp_body(q_ref, k_ref, v_ref, cq_ref, ck_ref, o_ref, qm_scr, vone_scr, cqb_scr, m_scr, sh_scr, acc_scr, *, t):
    qi = pl.program_id(2)
    n_tiles = k_ref.shape[1]
    lane_head = lax.broadcasted_iota(i32, (1, _HGW), 1) // FOX_HEAD_DIM

    @pl.when(qi == 0)
    def _():
        row_head = lax.broadcasted_iota(i32, (_HGW, 1), 0) // FOX_HEAD_DIM
        for kt in range(n_tiles):
            vt = v_ref[0, kt].reshape(_HGW, t)
            for j in range(_HG):
                vone_scr[j, kt] = jnp.where(row_head == j, vt, jnp.ones((), bf16))

    q = q_ref[...]
    for j in range(_HG):
        qm_scr[j] = jnp.where(lane_head == j, q, jnp.zeros((), bf16))
        cqb_scr[j] = jnp.broadcast_to(cq_ref[0, 0, :, j:j + 1], (t, 128))
    m_scr[...] = jnp.full(m_scr.shape, -jnp.inf, f32)
    acc_scr[...] = jnp.zeros(acc_scr.shape, f32)

    def logits(ki, j, rb, cb, diagonal):
        r0, c0 = rb * _FOX_RB, cb * _FOX_CB
        if diagonal and c0 > r0 + _FOX_RB - 1:
            return None
        kt = k_ref[0, ki, :, :, c0:c0 + _FOX_CB].reshape(_HGW, _FOX_CB)
        u = jnp.dot(qm_scr[j, r0:r0 + _FOX_RB, :], kt, preferred_element_type=f32) - ck_ref[0, 0, ki, j:j + 1, c0:c0 + _FOX_CB]
        if diagonal and c0 + _FOX_CB - 1 > r0:
            qpos = r0 + lax.broadcasted_iota(i32, (_FOX_RB, _FOX_CB), 0)
            kpos = c0 + lax.broadcasted_iota(i32, (_FOX_RB, _FOX_CB), 1)
            u = jnp.where(kpos <= qpos, u, -jnp.inf)
        return u

    def max_pass(ki, diagonal):
        for j in range(_HG):
            for cb in range(t // _FOX_CB):
                for rb in range(t // _FOX_RB):
                    u = logits(ki, j, rb, cb, diagonal)
                    if u is not None:
                        rows = slice(rb * _FOX_RB, (rb + 1) * _FOX_RB)
                        m = m_scr[j, rows, :]
                        for c in range(_FOX_CB // 128):
                            m = jnp.maximum(m, u[:, c * 128:(c + 1) * 128])
                        m_scr[j, rows, :] = m

    def sum_pass(ki, diagonal):
        for j in range(_HG):
            for cb in range(t // _FOX_CB):
                vt = vone_scr[j, ki, :, cb * _FOX_CB:(cb + 1) * _FOX_CB]
                for rb in range(t // _FOX_RB):
                    u = logits(ki, j, rb, cb, diagonal)
                    if u is not None:
                        rows = slice(rb * _FOX_RB, (rb + 1) * _FOX_RB)
                        p = jnp.exp(u - jnp.concatenate([sh_scr[j, rows, :]] * (_FOX_CB // 128), axis=-1))
                        acc_scr[j, rows, :] += lax.dot_general(p.astype(bf16), vt, _NT, preferred_element_type=f32)

    lax.fori_loop(0, qi, lambda ki, c: (max_pass(ki, False), c)[1], 0)
    for j in range(_HG):
        for rb in range(t // _FOX_RB):
            rows = slice(rb * _FOX_RB, (rb + 1) * _FOX_RB)
            m = m_scr[j, rows, :]
            for cb in range(t // _FOX_CB):
                u = logits(qi, j, rb, cb, True)
                if u is not None:
                    for c in range(_FOX_CB // 128):
                        m = jnp.maximum(m, u[:, c * 128:(c + 1) * 128])
            cq = cqb_scr[j, rows, :]
            m_full = jnp.max(m, axis=-1, keepdims=True) + cq
            sh_scr[j, rows, :] = m_full - cq
    lax.fori_loop(0, qi, lambda ki, c: (sum_pass(ki, False), c)[1], 0)
    sum_pass(qi, True)

    out = jnp.zeros((t, _HGW), f32)
    for j in range(_HG):
        acc = acc_scr[j]
        other = acc[:, 128:256] if j < _HG // 2 else acc[:, 0:128]
        inv = 1.0 / other
        out = jnp.where(lane_head == j, acc * jnp.concatenate([inv, inv], axis=-1), out)
    o_ref[...] = out


def _fox_prompt(fq, kt, vt, c_rows, c_cols, *, bsz, length, t=512):
    n = length // t
    kv = pl.BlockSpec((1, n, _HG, FOX_HEAD_DIM, t), lambda b, g, qi: (b, 0, g, 0, 0))
    return pl.pallas_call(
        functools.partial(_foxp_body, t=t),
        grid=(bsz, FOX_HEADS // _HG, n),
        in_specs=[pl.BlockSpec((t, _HGW), lambda b, g, qi: (b * n + qi, g)), kv, kv,
                  pl.BlockSpec((1, 1, t, _HG), lambda b, g, qi: (b, g, qi, 0)),
                  pl.BlockSpec((1, 1, n, _HG, t), lambda b, g, qi: (b, g, 0, 0, 0))],
        out_specs=pl.BlockSpec((t, _HGW), lambda b, g, qi: (b * n + qi, g)),
        out_shape=jax.ShapeDtypeStruct((bsz * length, FOX_WIDTH), f32),
        scratch_shapes=[pltpu.VMEM((_HG, t, _HGW), bf16), pltpu.VMEM((_HG, n, _HGW, t), bf16),
                        pltpu.VMEM((_HG, t, 128), f32), pltpu.VMEM((_HG, t, 128), f32),
                        pltpu.VMEM((_HG, t, 128), f32), pltpu.VMEM((_HG, t, _HGW), f32)],
        compiler_params=_cparams(("arbitrary",) * 3),
        name="fox_prompt",
    )(fq, kt, vt, c_rows, c_cols)


def _foxs_body(kc_ref, vc_ref, kn_ref, vn_ref, qbd_ref, ckp_ref, ckn_ref, cq_ref, o_ref, s_scr,
               *, past, new, kchunk):
    nrow = FOX_HEADS * new
    qbd = qbd_ref[0]
    cq = cq_ref[0]

    def per_head_rows(c8):
        return jnp.concatenate([jnp.broadcast_to(c8[h:h + 1, :], (new, c8.shape[1])) for h in range(FOX_HEADS)], axis=0)

    n_chunks = past // kchunk
    m = jnp.full((nrow, 1), -jnp.inf, f32)
    for c in range(n_chunks):
        cols = slice(c * kchunk, (c + 1) * kchunk)
        kt = kc_ref[0, :, :, cols].reshape(FOX_WIDTH, kchunk).astype(bf16)
        s = jnp.dot(qbd, kt, preferred_element_type=f32) + cq - per_head_rows(ckp_ref[0, :, cols])
        s_scr[:, cols] = s
        m = jnp.maximum(m, jnp.max(s, axis=-1, keepdims=True))
    sn = lax.dot_general(qbd, kn_ref[0].astype(bf16), _NT, preferred_element_type=f32)
    sn = sn + cq - per_head_rows(ckn_ref[0])
    qry_i = lax.broadcasted_iota(i32, (nrow, new), 0) % new
    key_j = lax.broadcasted_iota(i32, (nrow, new), 1)
    sn = jnp.where(key_j <= qry_i, sn, -jnp.inf)
    m = jnp.maximum(m, jnp.max(sn, axis=-1, keepdims=True))
    pn = jnp.exp(sn - m)
    l = jnp.sum(pn, axis=-1, keepdims=True)
    for c in range(n_chunks):
        cols = slice(c * kchunk, (c + 1) * kchunk)
        p = jnp.exp(s_scr[:, cols] - m)
        s_scr[:, cols] = p
        l = l + jnp.sum(p, axis=-1, keepdims=True)
    inv = 1.0 / l
    acc = jnp.dot((pn * inv).astype(bf16), vn_ref[0].astype(bf16), preferred_element_type=f32)
    for c in range(n_chunks):
        cols = slice(c * kchunk, (c + 1) * kchunk)
        vt = vc_ref[0, :, :, cols].reshape(FOX_WIDTH, kchunk).astype(bf16)
        acc = acc + lax.dot_general((s_scr[:, cols] * inv).astype(bf16), vt, _NT, preferred_element_type=f32)
    for h in range(FOX_HEADS):
        o_ref[0, :, h * FOX_HEAD_DIM:(h + 1) * FOX_HEAD_DIM] = (
            acc[h * new:(h + 1) * new, h * FOX_HEAD_DIM:(h + 1) * FOX_HEAD_DIM])


def _fox_sample(kc_t, vc_t, kn, vn, qbd, ckp, ckn, cq, *, kchunk=512):
    bsz, _, _, past = kc_t.shape
    new, w = kn.shape[1], kn.shape[2]
    blk = lambda a: pl.BlockSpec((1,) + a.shape[1:], lambda b: (b,) + (0,) * (a.ndim - 1))
    return pl.pallas_call(
        functools.partial(_foxs_body, past=past, new=new, kchunk=kchunk),
        grid=(bsz,),
        in_specs=[blk(kc_t), blk(vc_t), blk(kn), blk(vn), blk(qbd), blk(ckp), blk(ckn), blk(cq)],
        out_specs=pl.BlockSpec((1, new, w), lambda b: (b, 0, 0)),
        out_shape=jax.ShapeDtypeStruct((bsz, new, w), f32),
        scratch_shapes=[pltpu.VMEM((FOX_HEADS * new, past), f32)],
        compiler_params=_cparams(("arbitrary",)),
        name="fox_sample",
    )(kc_t, vc_t, kn, vn, qbd, ckp, ckn, cq)


def _route(logits_t, b_router):
    ne, tm = logits_t.shape
    gsz = ne // N_GROUPS
    shape3 = (N_GROUPS, gsz, tm)
    scores = jax.nn.sigmoid(logits_t).reshape(shape3)
    choice = scores + b_router
    im = lax.broadcasted_iota(i32, shape3, 1)
    ig = lax.broadcasted_iota(i32, (N_GROUPS, 1, tm), 0)
    ie = lax.broadcasted_iota(i32, shape3, 0) * gsz + im
    neg = -jnp.inf
    m1 = jnp.max(choice, axis=1, keepdims=True)
    first = jnp.min(jnp.where(choice == m1, im, gsz), axis=1, keepdims=True)
    m2 = jnp.max(jnp.where(im == first, neg, choice), axis=1, keepdims=True)
    cur = m1 + m2
    gsel = jnp.zeros(cur.shape, i32)
    for _ in range(TOPK_GROUPS):
        mx = jnp.max(cur, axis=0, keepdims=True)
        hit = ig == jnp.min(jnp.where(cur == mx, ig, N_GROUPS), axis=0, keepdims=True)
        gsel = jnp.where(hit, 1, gsel)
        cur = jnp.where(hit, neg, cur)
    cur = jnp.where(jnp.broadcast_to(gsel, shape3) > 0, choice, neg)
    ids, ws = [], []
    for _ in range(TOP_K):
        mx = jnp.max(jnp.max(cur, axis=1, keepdims=True), axis=0, keepdims=True)
        cand = jnp.where(cur == mx, ie, ne)
        fi = jnp.min(jnp.min(cand, axis=1, keepdims=True), axis=0, keepdims=True)
        hit = ie == fi
        w = jnp.where(hit, scores, 0.0)
        ids.append(fi.reshape(1, tm))
        ws.append(jnp.sum(jnp.sum(w, axis=1, keepdims=True), axis=0, keepdims=True).reshape(1, tm))
        cur = jnp.where(hit, neg, cur)
    tot = ws[0]
    for w in ws[1:]:
        tot = tot + w
    scale = ROUTED_SCALE / (tot + 1e-20)
    return ids, [w * scale for w in ws]


def _outproj_body(og_ref, gg_ref, of_ref, x_ref, gt1_ref, sh2_ref, sc2_ref, ggla_ref, gfox_ref, gffn_ref,
                  wout_ref, wrh_ref, wrl_ref, br_ref, x1_ref, hp_ref, idx_ref, wts_ref, *, nb, rpb):
    tm = nb * rpb
    og = og_ref[...]
    parts = []
    for h in range(GLA_HEADS):
        seg = og[:, h * GLA_HEAD_V:(h + 1) * GLA_HEAD_V]
        parts.append(seg * lax.rsqrt(jnp.mean(seg * seg, axis=-1, keepdims=True) + NORM_EPS))
    gla_n = jnp.concatenate(parts, axis=-1) * ggla_ref[...] * _silu(gg_ref[...])
    of = of_ref[...]
    fox_n = of * lax.rsqrt(jnp.mean(of * of, axis=-1, keepdims=True) + NORM_EPS) * gfox_ref[...]
    merged = jnp.concatenate([gla_n, fox_n], axis=-1).astype(bf16)
    mix = jnp.dot(merged, wout_ref[...], preferred_element_type=f32)
    x1 = (x_ref[...].reshape(nb, rpb, D_MODEL) + gt1_ref[...] * mix.reshape(nb, rpb, D_MODEL))
    x1_ref[...] = x1.reshape(tm, D_MODEL)
    x1f = x1.reshape(tm, D_MODEL)
    y = x1f * lax.rsqrt(jnp.mean(x1f * x1f, axis=-1, keepdims=True) + NORM_EPS) * gffn_ref[...]
    h2 = (y.reshape(nb, rpb, D_MODEL) * (1.0 + sc2_ref[...]) + sh2_ref[...]).reshape(tm, D_MODEL)
    h_hi = h2.astype(bf16)
    hp_ref[...] = h_hi
    h_lo = (h2 - h_hi.astype(f32)).astype(bf16)
    logits_t = (lax.dot_general(wrh_ref[...], h_hi, _NT, preferred_element_type=f32)
                + lax.dot_general(wrh_ref[...], h_lo, _NT, preferred_element_type=f32)
                + lax.dot_general(wrl_ref[...], h_hi, _NT, preferred_element_type=f32))
    ids, ws = _route(logits_t, br_ref[...])
    for j in range(TOP_K):
        idx_ref[j:j + 1, :] = ids[j]
        wts_ref[j:j + 1, :] = ws[j]


def _outproj(og, gg, of, x2, m3, g_gla4, g_fox, g_ffn, w_out, w_rt_hi, w_rt_lo, b_r, *, nb, rpb, tiles_per_batch):
    t, d = x2.shape
    tm = nb * rpb
    row = lambda w: pl.BlockSpec((tm, w), lambda i: (i, 0))
    col = pl.BlockSpec((TOP_K, tm), lambda i: (0, i))
    mod = lambda c: pl.BlockSpec((nb, 1, d), lambda i, c=c: (i // tiles_per_batch, 0, c))
    full = lambda a: pl.BlockSpec(a.shape, lambda i: (0,) * a.ndim)
    return pl.pallas_call(
        functools.partial(_outproj_body, nb=nb, rpb=rpb),
        grid=(t // tm,),
        in_specs=[row(GLA_VALUE_WIDTH), row(GLA_VALUE_WIDTH), row(FOX_WIDTH), row(d), mod(2), mod(3), mod(4),
                  full(g_gla4), full(g_fox), full(g_ffn), full(w_out), full(w_rt_hi), full(w_rt_lo), full(b_r)],
        out_specs=[row(d), row(d), col, col],
        out_shape=[jax.ShapeDtypeStruct((t, d), f32), jax.ShapeDtypeStruct((t, d), bf16),
                   jax.ShapeDtypeStruct((TOP_K, t), i32), jax.ShapeDtypeStruct((TOP_K, t), f32)],
        compiler_params=_cparams(("arbitrary",)),
        name="outproj",
    )(og, gg, of, x2, m3, m3, m3, g_gla4, g_fox, g_ffn, w_out, w_rt_hi, w_rt_lo, b_r)


def _rank_body(idx_ref, lpos_ref, lst_ref, seg_ref, dst_ref, tot_ref, tot_scr, base_scr, run_scr, *, tn):
    ph, gi = pl.program_id(0), pl.program_id(1)
    ie = lax.broadcasted_iota(i32, (N_EXPERTS, tn), 0)
    idx = idx_ref[...]
    multihot = jnp.zeros((N_EXPERTS, tn), f32)
    for j in range(TOP_K):
        multihot = multihot + (ie == idx[j:j + 1, :]).astype(f32)
    seg = jnp.maximum(jnp.ceil(jnp.sum(multihot, axis=1, keepdims=True) * (1.0 / SEG)), 1.0)
    r = lax.broadcasted_iota(i32, (N_EXPERTS, N_EXPERTS), 0)
    c = lax.broadcasted_iota(i32, (N_EXPERTS, N_EXPERTS), 1)
    strict = (c < r).astype(f32)

    def excl_cumsum(col):
        return jnp.dot(strict, jnp.broadcast_to(col, (N_EXPERTS, 128)), precision=HIGHEST,
                       preferred_element_type=f32)[:, :1]

    @pl.when(jnp.logical_and(ph == 0, gi == 0))
    def _():
        tot_scr[...] = jnp.zeros(tot_scr.shape, f32)

    @pl.when(ph == 0)
    def _():
        tot_scr[...] += seg

    @pl.when(jnp.logical_and(ph == 1, gi == 0))
    def _():
        tot = tot_scr[...]
        blocks = jnp.ceil(tot * (float(SEG) / EXPERT_ROWS))
        base_scr[...] = excl_cumsum(blocks) * float(EXPERT_ROWS // SEG)
        run_scr[...] = jnp.zeros(run_scr.shape, f32)
        tot_ref[...] = tot

    @pl.when(ph == 1)
    def _():
        lstart = excl_cumsum(seg)
        rr = lax.broadcasted_iota(i32, (tn, tn), 0)
        cc = lax.broadcasted_iota(i32, (tn, tn), 1)
        before = (rr < cc).astype(bf16)
        excl = jnp.dot(multihot.astype(bf16), before, preferred_element_type=f32)
        pos = lstart * float(SEG) + excl
        for j in range(TOP_K):
            lpos_ref[j:j + 1, :] = jnp.sum(jnp.where(ie == idx[j:j + 1, :], pos, 0.0), axis=0,
                                           keepdims=True).astype(i32)
        lst_ref[0] = lstart.astype(i32)
        seg_ref[0] = seg.astype(i32)
        dst_ref[0] = (base_scr[...] + run_scr[...]).astype(i32)
        run_scr[...] += seg


def _rank(idx, *, tn):
    _, t = idx.shape
    ng = t // tn
    table = pl.BlockSpec((1, N_EXPERTS, 1), lambda p, g: (g * p, 0, 0))
    return pl.pallas_call(
        functools.partial(_rank_body, tn=tn),
        grid=(2, ng),
        in_specs=[pl.BlockSpec((TOP_K, tn), lambda p, g: (0, g))],
        out_specs=[pl.BlockSpec((TOP_K, tn), lambda p, g: (0, g * p)), table, table, table,
                   pl.BlockSpec((N_EXPERTS, 1), lambda p, g: (0, 0))],
        out_shape=[jax.ShapeDtypeStruct((TOP_K, t), i32)] + [jax.ShapeDtypeStruct((ng, N_EXPERTS, 1), i32)] * 3
                  + [jax.ShapeDtypeStruct((N_EXPERTS, 1), f32)],
        scratch_shapes=[pltpu.VMEM((N_EXPERTS, 1), f32)] * 3,
        compiler_params=_cparams(("arbitrary", "arbitrary")),
        name="rank",
    )(idx)


def _group_rows(lst_ref, seg_ref, g):
    k = g * N_EXPERTS + N_EXPERTS - 1
    return (lst_ref[k] + seg_ref[k]) * SEG


def _start_segment_copies(lst_ref, seg_ref, dst_ref, g, make_copy, experts=range(N_EXPERTS)):
    for e in experts:
        k = g * N_EXPERTS + e
        make_copy(pl.multiple_of(lst_ref[k] * SEG, SEG), pl.multiple_of(dst_ref[k] * SEG, SEG),
                  seg_ref[k] * SEG).start()


_CHUNKS_PER_TRIP = 2
_MIN_TRIPS = TOP_K * GROUP // (_CHUNKS_PER_TRIP * PCHUNK)
_EXPERTS_PER_TRIP = N_EXPERTS // _MIN_TRIPS


def _staged_trips(n_rows, pair, issue):
    for c in range(_MIN_TRIPS):
        pair(c)
        if issue is not None:
            issue(range(c * _EXPERTS_PER_TRIP, (c + 1) * _EXPERTS_PER_TRIP))
    lax.fori_loop(_MIN_TRIPS, _chunk_trips(n_rows), lambda c, carry: (pair(c), carry)[1], 0)


def _placement_rows(lpos, vals, r0):
    rid = lax.broadcasted_iota(i32, (PCHUNK, GROUP), 0).astype(f32).astype(bf16)
    rel = (lpos - r0).astype(f32).astype(bf16)
    p = jnp.zeros((PCHUNK, GROUP), bf16)
    for j in range(TOP_K):
        v = jnp.ones((), bf16) if vals is None else vals[j:j + 1, :]
        p = jnp.where(rel[j:j + 1, :] == rid, v, p)
    return p


def _chunk_trips(n_rows):
    return (n_rows + _CHUNKS_PER_TRIP * PCHUNK - 1) // (_CHUNKS_PER_TRIP * PCHUNK)


def _dispatch_body(lst_ref, seg_ref, dst_ref, tail0_ref, tailn_ref, misc_ref, lpos_ref, hp_ref, hs_ref, xs_hbm,
                   stage, zeros, sem, zsem, *, ngp, ng, n_blocks):
    s = pl.program_id(0)
    slot = s % 2
    h = jnp.where(s < ngp, hp_ref[...], hs_ref[...])
    lpos = lpos_ref[...]

    def pair(c):
        for k in range(_CHUNKS_PER_TRIP):
            r0 = pl.multiple_of((c * _CHUNKS_PER_TRIP + k) * PCHUNK, PCHUNK)
            p = _placement_rows(lpos, None, r0)
            stage[slot, pl.ds(r0, PCHUNK), :] = jnp.dot(p, h, preferred_element_type=f32).astype(bf16)

    def issue_previous(experts):
        _start_segment_copies(lst_ref, seg_ref, dst_ref, s - 1, lambda a, d, n: pltpu.make_async_copy(
            stage.at[1 - slot, pl.ds(a, n), :], xs_hbm.at[pl.ds(d, n), :], sem.at[1 - slot]), experts)

    def wait_group(gg, sl):
        n = _group_rows(lst_ref, seg_ref, gg)
        pltpu.make_async_copy(stage.at[sl, pl.ds(0, n), :], xs_hbm.at[pl.ds(0, n), :], sem.at[sl]).wait()

    @pl.when(s >= 2)
    def _():
        wait_group(s - 2, slot)

    @pl.when(s == 0)
    def _():
        _staged_trips(_group_rows(lst_ref, seg_ref, s), pair, None)

    @pl.when(jnp.logical_and(s > 0, s < ng))
    def _():
        _staged_trips(_group_rows(lst_ref, seg_ref, s), pair, issue_previous)

    @pl.when(s == ng)
    def _():
        issue_previous(range(N_EXPERTS))
        wait_group(s - 1, 1 - slot)
        zeros[...] = jnp.zeros(zeros.shape, bf16)

        def per_expert(e, carry):
            n = tailn_ref[e] * SEG

            @pl.when(n > 0)
            def _():
                pltpu.make_async_copy(zeros.at[pl.ds(0, n), :],
                                      xs_hbm.at[pl.ds(pl.multiple_of(tail0_ref[e] * SEG, SEG), n), :], zsem).start()

            return carry

        lax.fori_loop(0, N_EXPERTS, per_expert, 0)
        n_used = misc_ref[0]

        def per_block(b, carry):
            pltpu.make_async_copy(zeros, xs_hbm.at[pl.ds(pl.multiple_of(b * EXPERT_ROWS, EXPERT_ROWS), EXPERT_ROWS), :],
                                  zsem).start()
            return carry

        lax.fori_loop(n_used, n_blocks, per_block, 0)
        n_zero = misc_ref[1] * SEG + (n_blocks - n_used) * EXPERT_ROWS

        @pl.when(n_zero > 0)
        def _():
            pltpu.make_async_copy(xs_hbm.at[pl.ds(0, n_zero), :], xs_hbm.at[pl.ds(0, n_zero), :], zsem).wait()


def _dispatch(tables, tail0, tailn, misc, lpos, hp, hs, *, n_blocks):
    ngp, ngs = hp.shape[0] // GROUP, hs.shape[0] // GROUP
    ng = ngp + ngs
    d = hp.shape[1]
    return pl.pallas_call(
        functools.partial(_dispatch_body, ngp=ngp, ng=ng, n_blocks=n_blocks),
        grid_spec=pltpu.PrefetchScalarGridSpec(
            num_scalar_prefetch=6,
            grid=(ng + 1,),
            in_specs=[pl.BlockSpec((TOP_K, GROUP), lambda g, *_: (0, jnp.minimum(g, ng - 1))),
                      pl.BlockSpec((GROUP, d), lambda g, *_: (jnp.minimum(g, ngp - 1), 0)),
                      pl.BlockSpec((GROUP, d), lambda g, *_: (jnp.clip(g - ngp, 0, ngs - 1), 0))],
            out_specs=pl.BlockSpec(memory_space=pl.ANY),
            scratch_shapes=[pltpu.VMEM((2, STAGE_ROWS, d), bf16), pltpu.VMEM((EXPERT_ROWS, d), bf16),
                            pltpu.SemaphoreType.DMA((2,)), pltpu.SemaphoreType.DMA]),
        out_shape=jax.ShapeDtypeStruct((n_blocks * EXPERT_ROWS, d), bf16),
        compiler_params=_cparams(("arbitrary",)),
        name="dispatch",
    )(*tables, tail0, tailn, misc, lpos, hp, hs)


def _experts_body(be_ref, nu_ref, xs_ref, wg_ref, wu_ref, wd_ref, y_ref, wg_s, wu_s, wd_s):
    i = pl.program_id(0)
    changed = jnp.logical_or(i == 0, be_ref[i] != be_ref[jnp.maximum(i - 1, 0)])

    @pl.when(changed)
    def _():
        wg_s[...] = wg_ref[0].astype(bf16)
        wu_s[...] = wu_ref[0].astype(bf16)
        wd_s[...] = wd_ref[0].astype(bf16)

    @pl.when(i < nu_ref[0])
    def _():
        x = xs_ref[...]
        g = jnp.dot(x, wg_s[...], preferred_element_type=f32)
        u = jnp.dot(x, wu_s[...], preferred_element_type=f32)
        a = (_silu(g) * u).astype(bf16)
        y_ref[...] = jnp.dot(a, wd_s[...], preferred_element_type=f32).astype(bf16)

    @pl.when(i >= nu_ref[0])
    def _():
        y_ref[...] = jnp.zeros(y_ref.shape, bf16)


def _experts(blk_e, n_used, xs, w_gate, w_up, w_down, *, block_rows):
    n_rows, d = xs.shape
    n_blocks = n_rows // block_rows
    de = w_gate.shape[2]
    return pl.pallas_call(
        _experts_body,
        grid_spec=pltpu.PrefetchScalarGridSpec(
            num_scalar_prefetch=2,
            grid=(n_blocks,),
            in_specs=[pl.BlockSpec((block_rows, d), lambda i, be, nu: (jnp.minimum(i, nu[0] - 1), 0)),
                      pl.BlockSpec((1, d, de), lambda i, be, nu: (be[i], 0, 0)),
                      pl.BlockSpec((1, d, de), lambda i, be, nu: (be[i], 0, 0)),
                      pl.BlockSpec((1, de, d), lambda i, be, nu: (be[i], 0, 0))],
            out_specs=pl.BlockSpec((block_rows, d), lambda i, be, nu: (i, 0)),
            scratch_shapes=[pltpu.VMEM((d, de), bf16), pltpu.VMEM((d, de), bf16), pltpu.VMEM((de, d), bf16)]),
        out_shape=jax.ShapeDtypeStruct((n_rows, d), bf16),
        compiler_params=_cparams(("arbitrary",)),
        name="experts",
    )(blk_e, n_used, xs, w_gate, w_up, w_down)


def _combine_body(lst_ref, seg_ref, dst_ref, lpos_ref, wts_ref, h_ref, x1_ref, gt2_ref, wsg_ref, wsu_ref, wsd_ref,
                  gfin_ref, y_hbm, o_ref, ybuf, acc, sem, *, nb, rpb, g0, ng):
    i = pl.program_id(0)
    g = g0 + i
    slot = i % 2

    def start_gather(gg, sl, experts=range(N_EXPERTS)):
        _start_segment_copies(lst_ref, seg_ref, dst_ref, gg, lambda s, d, n: pltpu.make_async_copy(
            y_hbm.at[pl.ds(d, n), :], ybuf.at[sl, pl.ds(s, n), :], sem.at[sl]), experts)

    @pl.when(i == 0)
    def _():
        ybuf[...] = jnp.zeros(ybuf.shape, bf16)
        start_gather(g, slot)

    h = h_ref[...]
    gate = jnp.dot(h, wsg_ref[...], preferred_element_type=f32)
    up = jnp.dot(h, wsu_ref[...], preferred_element_type=f32)
    acc[...] = jnp.dot((_silu(gate) * up).astype(bf16), wsd_ref[...], preferred_element_type=f32)

    n_rows = _group_rows(lst_ref, seg_ref, g)
    pltpu.make_async_copy(y_hbm.at[pl.ds(0, n_rows), :], ybuf.at[slot, pl.ds(0, n_rows), :], sem.at[slot]).wait()

    lpos = lpos_ref[...]
    wts = wts_ref[...].astype(bf16)

    def pair(c):
        for k in range(_CHUNKS_PER_TRIP):
            r0 = pl.multiple_of((c * _CHUNKS_PER_TRIP + k) * PCHUNK, PCHUNK)
            wt = _placement_rows(lpos, wts, r0)
            acc[...] += lax.dot_general(wt, ybuf[slot, pl.ds(r0, PCHUNK), :], _TN, preferred_element_type=f32)

    @pl.when(i + 1 < ng)
    def _():
        _staged_trips(n_rows, pair, lambda experts: start_gather(g + 1, 1 - slot, experts))

    @pl.when(i + 1 == ng)
    def _():
        _staged_trips(n_rows, pair, None)

    out = (x1_ref[...].reshape(nb, rpb, D_MODEL) + gt2_ref[...] * acc[...].reshape(nb, rpb, D_MODEL))
    out = out.reshape(GROUP, D_MODEL)
    o_ref[...] = out * lax.rsqrt(jnp.mean(out * out, axis=-1, keepdims=True) + NORM_EPS) * gfin_ref[...]


def _combine(tables, lpos_rows, wts_rows, y, h2, x1, m3, w_sg, w_su, w_sd, g_fin, *, nb, rpb, tiles_per_batch, g0):
    t, d = x1.shape
    assert nb * rpb == GROUP
    ng = t // GROUP
    row = lambda w: pl.BlockSpec((GROUP, w), lambda i, *_: (i, 0))
    grow = lambda w: pl.BlockSpec((w, GROUP), lambda i, *_: (0, g0 + i))
    full = lambda a: pl.BlockSpec(a.shape, lambda i, *_: (0,) * a.ndim)
    return pl.pallas_call(
        functools.partial(_combine_body, nb=nb, rpb=rpb, g0=g0, ng=ng),
        grid_spec=pltpu.PrefetchScalarGridSpec(
            num_scalar_prefetch=3,
            grid=(ng,),
            in_specs=[grow(TOP_K), grow(TOP_K), row(d), row(d),
                      pl.BlockSpec((nb, 1, d), lambda i, *_: (i // tiles_per_batch, 0, 5)),
                      full(w_sg), full(w_su), full(w_sd), full(g_fin), pl.BlockSpec(memory_space=pl.ANY)],
            out_specs=row(d),
            scratch_shapes=[pltpu.VMEM((2, STAGE_ROWS, d), bf16), pltpu.VMEM((GROUP, d), f32),
                            pltpu.SemaphoreType.DMA((2,))]),
        out_shape=jax.ShapeDtypeStruct((t, d), f32),
        compiler_params=_cparams(("arbitrary",)),
        name="combine",
    )(*tables, lpos_rows, wts_rows, h2, x1, m3, w_sg, w_su, w_sd, g_fin, y)


_GLA_CHUNKS_PER_STEP = 4


def kernel(x_prompt, x_sample, state_gla, cache_fox_k, cache_fox_v, cache_fox_logf, c_prompt, c_sample, w_ada, b_ada, g_norm_mix, g_norm_ffn, w_in, w_gla_gk2, b_gla_gk, g_gla_out, b_fox_f, g_fox_out, w_out, w_router, b_router, w_exp_gate, w_exp_up, w_exp_down, w_sh_gate, w_sh_up, w_sh_down, g_final):
    assert w_ada.shape[0] == 1, "single-layer trunk"
    bp, lp, d = x_prompt.shape
    bs, ls, _ = x_sample.shape
    past = cache_fox_k.shape[2]
    tp, ts = bp * lp, bs * ls

    wi = w_in[0]
    o_glr = 2 * GLA_KEY_WIDTH + 2 * GLA_VALUE_WIDTH
    o_fq = o_glr + GLA_LOW_RANK
    o_ff = o_fq + 3 * FOX_WIDTH
    small = jnp.concatenate([wi[:, o_ff:o_ff + FOX_HEADS], wi[:, o_glr:o_fq],
                             jnp.zeros((d, 128 - FOX_HEADS - GLA_LOW_RANK), f32)], axis=1)
    w_cat = jnp.concatenate([wi[:, :o_glr], wi[:, o_fq:o_ff], small], axis=1).astype(bf16)
    w_t = jnp.concatenate([wi[:, o_fq + FOX_WIDTH:o_ff + FOX_HEADS], jnp.zeros((d, 16 - FOX_HEADS), f32)],
                          axis=1).T.astype(bf16)
    wgk_pad = jnp.zeros((128, GLA_KEY_WIDTH), f32).at[FOX_HEADS:FOX_HEADS + GLA_LOW_RANK].set(w_gla_gk2[0]).astype(bf16)
    bgk = b_gla_gk[0].reshape(1, GLA_KEY_WIDTH)
    bf_pad = jnp.zeros((1, 128), f32).at[0, :FOX_HEADS].set(b_fox_f[0])
    bf_col = jnp.zeros((16, 1), f32).at[:FOX_HEADS, 0].set(b_fox_f[0])
    g_mix = g_norm_mix[0].reshape(1, d)
    g_ffn = g_norm_ffn[0].reshape(1, d)
    g_gla4 = jnp.tile(g_gla_out[0], GLA_HEADS).reshape(1, GLA_VALUE_WIDTH)
    g_fox = g_fox_out[0].reshape(1, FOX_WIDTH)
    g_fin = g_final.reshape(1, d)
    w_out_b = w_out[0].astype(bf16)
    w_rt = w_router[0].T
    w_rt_hi = w_rt.astype(bf16)
    w_rt_lo = (w_rt - w_rt_hi.astype(f32)).astype(bf16)
    b_r = b_router[0].reshape(N_GROUPS, N_EXPERTS // N_GROUPS, 1)
    w_sg, w_su, w_sd = w_sh_gate[0].astype(bf16), w_sh_up[0].astype(bf16), w_sh_down[0].astype(bf16)

    m_all = _ada(jnp.concatenate([c_prompt, c_sample], axis=0), w_ada[0], b_ada[0])
    m3p = m_all[:bp].reshape(bp, 1, 6 * d)
    m3s = m_all[bp:].reshape(bs, 1, 6 * d)

    groups = []
    for (x, m3, bsz, length) in ((x_prompt, m3p, bp, lp), (x_sample, m3s, bs, ls)):
        if length >= GROUP:
            nb, rpb_in, tpb_in = 1, 512, length // 512
            rpb, tpb = GROUP, length // GROUP
        else:
            nb, rpb_in, tpb_in = GROUP // length, length, 1
            rpb, tpb = length, 1
        groups.append(dict(x2=x.reshape(bsz * length, d), m3=m3, bsz=bsz, length=length,
                           nb=nb, rpb_in=rpb_in, tpb_in=tpb_in, rpb=rpb, tpb=tpb))

    gp, gs = groups
    (gp["gq"], gp["gk"], gp["gv"], gp["gg"], gp["la"], gp["fq"], kt_p, vt_p, ktb_p, vtb_p, lft_p) = _inproj(
        gp["x2"], gp["m3"], g_mix, w_cat, wgk_pad, bgk, bf_pad, w_t, bf_col,
        nb=gp["nb"], rpb=gp["rpb_in"], tiles_per_batch=gp["tpb_in"], seq_minor=True)
    (gs["gq"], gs["gk"], gs["gv"], gs["gg"], gs["la"], gs["fq"], fk_s, fv_s, lf_s) = _inproj(
        gs["x2"], gs["m3"], g_mix, w_cat, wgk_pad, bgk, bf_pad, w_t, bf_col,
        nb=gs["nb"], rpb=gs["rpb_in"], tiles_per_batch=gs["tpb_in"], seq_minor=False)

    s0p = jnp.zeros((bp, GLA_HEADS, GLA_HEAD_K, GLA_HEAD_V), f32)
    gp["og"], sp_new = _gla(gp["gq"], gp["gk"], gp["gv"], gp["la"], s0p, bsz=bp, length=lp, chunk=GLA_CHUNK,
                            cps=_GLA_CHUNKS_PER_STEP)
    gs["og"], ss_new = _gla(gs["gq"], gs["gk"], gs["gv"], gs["la"], state_gla[0], bsz=bs, length=ls, chunk=ls, cps=1)

    c_p = _cumsum_lanes(lft_p)
    c_grp = c_p.reshape(bp, FOX_HEADS // _HG, _HG, lp)
    c_rows = c_grp.transpose(0, 1, 3, 2)
    t_att = gp["rpb_in"]
    c_cols = c_grp.reshape(bp, FOX_HEADS // _HG, _HG, lp // t_att, t_att).transpose(0, 1, 3, 2, 4)
    gp["of"] = _fox_prompt(gp["fq"], ktb_p, vtb_p, c_rows, c_cols, bsz=bp, length=lp, t=t_att)

    lf_s = lf_s.reshape(bs, ls, FOX_HEADS)
    lpad = -(-ls // 128) * 128
    c_s = _cumsum_lanes(jnp.pad(lf_s.transpose(0, 2, 1), ((0, 0), (0, 0), (0, lpad - ls))))[:, :, :ls]
    ck_past = _cumsum_lanes(cache_fox_logf[0].astype(f32).transpose(0, 2, 1), suffix=True)
    fq_s = gs["fq"].reshape(bs, ls, FOX_HEADS, FOX_HEAD_DIM)
    eye_h = jnp.eye(FOX_HEADS, dtype=bf16)
    qbd = jnp.einsum("bihd,hg->bhigd", fq_s, eye_h).reshape(bs, FOX_HEADS * ls, FOX_WIDTH)
    of_s = _fox_sample(cache_fox_k[0].transpose(0, 2, 3, 1), cache_fox_v[0].transpose(0, 2, 3, 1),
                       fk_s.reshape(bs, ls, FOX_WIDTH), fv_s.reshape(bs, ls, FOX_WIDTH), qbd,
                       ck_past, c_s, c_s.reshape(bs, FOX_HEADS * ls, 1))
    gs["of"] = of_s.reshape(ts, FOX_WIDTH)

    for gr in groups:
        gr["x1"], gr["hp"], gr["idx"], gr["wts"] = _outproj(
            gr["og"], gr["gg"], gr["of"], gr["x2"], gr["m3"], g_gla4, g_fox, g_ffn, w_out_b, w_rt_hi, w_rt_lo, b_r,
            nb=gr["nb"] if gr["length"] < GROUP else 1, rpb=gr["rpb"], tiles_per_batch=gr["tpb"])

    t_all = tp + ts
    n_groups = t_all // GROUP
    tiles_per_block = EXPERT_ROWS // SEG
    n_blocks = -(-(t_all * TOP_K + n_groups * N_EXPERTS * SEG) // EXPERT_ROWS) + N_EXPERTS
    idx_all = jnp.concatenate([gp["idx"], gs["idx"]], axis=1)
    lpos, lst, seg, dst, tot = _rank(idx_all, tn=GROUP)
    tables = tuple(a.reshape(n_groups * N_EXPERTS) for a in (lst, seg, dst))
    tot_e = tot[:, 0].astype(i32)
    blocks_per_e = (tot_e + tiles_per_block - 1) // tiles_per_block
    blk_end = jnp.cumsum(blocks_per_e)
    n_used = blk_end[-1:]
    blk = jnp.minimum(jnp.arange(n_blocks, dtype=i32), n_used[0] - 1)
    blk_e = jnp.minimum(jnp.sum((blk_end[None, :] <= blk[:, None]).astype(i32), axis=1), N_EXPERTS - 1)
    tail0 = (blk_end - blocks_per_e) * tiles_per_block + tot_e
    tailn = blocks_per_e * tiles_per_block - tot_e
    misc = jnp.stack([n_used[0], jnp.sum(tailn)]).astype(i32)
    xs = _dispatch(tables, tail0, tailn, misc, lpos, gp["hp"], gs["hp"], n_blocks=n_blocks)
    y = _experts(blk_e, n_used.astype(i32), xs, w_exp_gate[0], w_exp_up[0], w_exp_down[0], block_rows=EXPERT_ROWS)

    wts_all = jnp.concatenate([gp["wts"], gs["wts"]], axis=1)
    outs = []
    for gr, g0 in ((gp, 0), (gs, tp // GROUP)):
        outs.append(_combine(tables, lpos, wts_all, y, gr["hp"], gr["x1"], gr["m3"], w_sg, w_su, w_sd, g_fin,
                             nb=gr["nb"] if gr["length"] < GROUP else 1, rpb=gr["rpb"],
                             tiles_per_batch=gr["tpb"], g0=g0))
    y_prompt = outs[0].reshape(bp, lp, d)
    y_sample = outs[1].reshape(bs, ls, d)

    return (y_prompt, y_sample, sp_new[None], ss_new[None],
            kt_p.transpose(0, 3, 1, 2)[None], vt_p.transpose(0, 3, 1, 2)[None], lft_p.transpose(0, 2, 1)[None],
            fk_s.reshape(1, bs, ls, FOX_HEADS, FOX_HEAD_DIM), fv_s.reshape(1, bs, ls, FOX_HEADS, FOX_HEAD_DIM),
            lf_s[None])
```

```python
import functools

import jax
import jax.numpy as jnp
from jax import lax
from jax.experimental import pallas as pl
from jax.experimental.pallas import tpu as pltpu

f32, bf16, i32 = jnp.float32, jnp.bfloat16, jnp.int32

D_MODEL = 1024
GLA_HEADS, GLA_HEAD_K, GLA_HEAD_V = 4, 64, 128
GLA_KEY_WIDTH, GLA_VALUE_WIDTH, GLA_LOW_RANK = 256, 512, 16
GLA_GATE_NORMALIZER = 16.0
GLA_CHUNK = 64
FOX_HEADS, FOX_HEAD_DIM, FOX_WIDTH = 8, 64, 512
N_EXPERTS, N_GROUPS, TOPK_GROUPS, TOP_K = 64, 8, 4, 8
D_EXPERT = 256
ROUTED_SCALE = 2.5
NORM_EPS = 1e-6
EXPERT_ROWS = 1024
GROUP = 256
SEG = 16
PCHUNK = 256
STAGE_ROWS = -(-(TOP_K * GROUP + N_EXPERTS * SEG) // (2 * PCHUNK)) * (2 * PCHUNK)
TILES_PER_BLOCK = EXPERT_ROWS // SEG
VMEM_LIMIT = 56 * 1024 * 1024


def _cparams(sem):
    return pltpu.CompilerParams(dimension_semantics=sem, vmem_limit_bytes=VMEM_LIMIT)


def _log_sigmoid(x):
    return jnp.minimum(x, 0.0) - jnp.log1p(jnp.exp(-jnp.abs(x)))


def _silu(x):
    return x * jax.nn.sigmoid(x)


def _ada_body(c_ref, w_ref, b_ref, o_ref):
    a = _silu(c_ref[...]).astype(bf16)
    o_ref[...] = jnp.dot(a, w_ref[...].astype(bf16), preferred_element_type=f32) + b_ref[...]


def _ada(c_all, w_ada, b_ada):
    nb, d = c_all.shape
    n = w_ada.shape[1]
    tn = 768
    return pl.pallas_call(
        _ada_body,
        grid=(n // tn,),
        in_specs=[pl.BlockSpec((nb, d), lambda j: (0, 0)),
                  pl.BlockSpec((d, tn), lambda j: (0, j)),
                  pl.BlockSpec((1, tn), lambda j: (0, j))],
        out_specs=pl.BlockSpec((nb, tn), lambda j: (0, j)),
        out_shape=jax.ShapeDtypeStruct((nb, n), f32),
        compiler_params=_cparams(("arbitrary",)),
        name="ada",
    )(c_all, w_ada, b_ada.reshape(1, n))


_C_GQ, _C_GK, _C_GV, _C_GG, _C_FQ, _C_FK, _C_FV, _C_SM, _C_END = 0, 256, 512, 1024, 1536, 2048, 2560, 3072, 3200


_NT = (((1,), (1,)), ((), ()))
_TN = (((0,), (0,)), ((), ()))


def _inproj_body(x_ref, sh_ref, sc_ref, g_ref, w_ref, wgk_ref, bgk_ref, bf_ref, wt_ref, bfc_ref,
                 gq_ref, gk_ref, gv_ref, gg_ref, la_ref, fq_ref, k_ref, v_ref, *rest, nb, rpb, seq_minor):
    tm = nb * rpb
    x = x_ref[...]
    y = x * lax.rsqrt(jnp.mean(x * x, axis=-1, keepdims=True) + NORM_EPS) * g_ref[...]
    h = (y.reshape(nb, rpb, D_MODEL) * (1.0 + sc_ref[...]) + sh_ref[...]).reshape(tm, D_MODEL).astype(bf16)

    def proj(a, b):
        return jnp.dot(h, w_ref[:, a:b], preferred_element_type=f32)

    gq_ref[...] = proj(_C_GQ, _C_GK) * (GLA_HEAD_K ** -0.5)
    gk_ref[...] = proj(_C_GK, _C_GV)
    gv_ref[...] = proj(_C_GV, _C_GG)
    gg_ref[...] = proj(_C_GG, _C_FQ)
    fq_ref[...] = (proj(_C_FQ, _C_FK) * (FOX_HEAD_DIM ** -0.5)).astype(bf16)
    zs = proj(_C_SM, _C_END)
    ga = jnp.dot(zs.astype(bf16), wgk_ref[...], preferred_element_type=f32) + bgk_ref[...]
    la_ref[...] = _log_sigmoid(ga) * (1.0 / GLA_GATE_NORMALIZER)
    if seq_minor:
        kb_ref, vb_ref, lf_ref = rest
        shape4 = (FOX_HEADS, FOX_HEAD_DIM, tm)
        kt = lax.dot_general(wt_ref[0:FOX_WIDTH, :], h, _NT, preferred_element_type=f32)
        k_ref[0] = kt.reshape(shape4)
        kb_ref[0, 0] = kt.astype(bf16).reshape(shape4)
        vt = lax.dot_general(wt_ref[FOX_WIDTH:2 * FOX_WIDTH, :], h, _NT, preferred_element_type=f32)
        v_ref[0] = vt.reshape(shape4)
        vb_ref[0, 0] = vt.astype(bf16).reshape(shape4)
        zf = lax.dot_general(wt_ref[2 * FOX_WIDTH:, :], h, _NT, preferred_element_type=f32)
        lf_ref[0] = _log_sigmoid(zf + bfc_ref[...])[:FOX_HEADS, :]
    else:
        (lf_ref,) = rest
        k_ref[...] = proj(_C_FK, _C_FV)
        v_ref[...] = proj(_C_FV, _C_SM)
        lf_ref[...] = _log_sigmoid(zs + bf_ref[...])[:, :FOX_HEADS]


def _inproj(x2, m3, g_mix, w_cat, wgk_pad, bgk, bf_pad, w_t, bf_col, *, nb, rpb, tiles_per_batch, seq_minor):
    t, d = x2.shape
    tm = nb * rpb
    tpb = tiles_per_batch
    bsz = t // (tm * tpb) if seq_minor else None
    row = lambda w: pl.BlockSpec((tm, w), lambda i: (i, 0))
    mod = lambda c: pl.BlockSpec((nb, 1, d), lambda i, c=c: (i // tpb, 0, c))
    full = lambda a: pl.BlockSpec(a.shape, lambda i: (0,) * a.ndim)
    widths = [(GLA_KEY_WIDTH, f32), (GLA_KEY_WIDTH, f32), (GLA_VALUE_WIDTH, f32), (GLA_VALUE_WIDTH, f32),
              (GLA_KEY_WIDTH, f32), (FOX_WIDTH, bf16)]
    out_specs = [row(w) for w, _ in widths]
    out_shape = [jax.ShapeDtypeStruct((t, w), dt) for w, dt in widths]
    if seq_minor:
        length = tm * tpb
        hd = pl.BlockSpec((1, FOX_HEADS, FOX_HEAD_DIM, tm), lambda i: (i // tpb, 0, 0, i % tpb))
        hd_tiles = pl.BlockSpec((1, 1, FOX_HEADS, FOX_HEAD_DIM, tm), lambda i: (i // tpb, i % tpb, 0, 0, 0))
        out_specs += [hd, hd, hd_tiles, hd_tiles, pl.BlockSpec((1, FOX_HEADS, tm), lambda i: (i // tpb, 0, i % tpb))]
        out_shape += [jax.ShapeDtypeStruct((bsz, FOX_HEADS, FOX_HEAD_DIM, length), f32)] * 2
        out_shape += [jax.ShapeDtypeStruct((bsz, tpb, FOX_HEADS, FOX_HEAD_DIM, tm), bf16)] * 2
        out_shape += [jax.ShapeDtypeStruct((bsz, FOX_HEADS, length), f32)]
    else:
        out_specs += [row(FOX_WIDTH), row(FOX_WIDTH), row(FOX_HEADS)]
        out_shape += [jax.ShapeDtypeStruct((t, w), f32) for w in (FOX_WIDTH, FOX_WIDTH, FOX_HEADS)]
    return pl.pallas_call(
        functools.partial(_inproj_body, nb=nb, rpb=rpb, seq_minor=seq_minor),
        grid=(t // tm,),
        in_specs=[row(d), mod(0), mod(1), full(g_mix), full(w_cat), full(wgk_pad), full(bgk), full(bf_pad),
                  full(w_t), full(bf_col)],
        out_specs=out_specs,
        out_shape=out_shape,
        compiler_params=_cparams(("arbitrary",)),
        name="inproj",
    )(x2, m3, m3, g_mix, w_cat, wgk_pad, bgk, bf_pad, w_t, bf_col)


def _cumsum_body(x_ref, o_ref, *, chunk, suffix):
    n = x_ref.shape[-1]
    r = lax.broadcasted_iota(i32, (chunk, chunk), 0)
    c = lax.broadcasted_iota(i32, (chunk, chunk), 1)
    tri = (r <= c).astype(bf16)
    local = []
    for j in range(n // chunk):
        x = x_ref[0, :, j * chunk:(j + 1) * chunk]
        x1 = x.astype(bf16)
        r1 = x - x1.astype(f32)
        x2 = r1.astype(bf16)
        x3 = (r1 - x2.astype(f32)).astype(bf16)
        local.append(jnp.dot(x1, tri, preferred_element_type=f32) + jnp.dot(x2, tri, preferred_element_type=f32)
                     + jnp.dot(x3, tri, preferred_element_type=f32))
    offsets = [jnp.zeros((x_ref.shape[1], 1), f32)]
    for j in range(n // chunk):
        offsets.append(offsets[-1] + local[j][:, chunk - 1:chunk])
    total = offsets[-1] if suffix else 0.0
    for j in range(n // chunk):
        o_ref[0, :, j * chunk:(j + 1) * chunk] = local[j] + (offsets[j] - total)


def _cumsum_lanes(x, *, suffix=False):
    b, h, n = x.shape
    return pl.pallas_call(
        functools.partial(_cumsum_body, chunk=128, suffix=suffix),
        grid=(b,),
        in_specs=[pl.BlockSpec((1, h, n), lambda i: (i, 0, 0))],
        out_specs=pl.BlockSpec((1, h, n), lambda i: (i, 0, 0)),
        out_shape=jax.ShapeDtypeStruct((b, h, n), f32),
        compiler_params=_cparams(("arbitrary",)),
        name="cumsum",
    )(x)


def _gla_body(q_ref, k_ref, v_ref, la_ref, s0_ref, o_ref, sout_ref, s_scr, *, chunk, cps, n_steps):
    si = pl.program_id(1)

    @pl.when(si == 0)
    def _():
        s_scr[...] = s0_ref[0].reshape(GLA_KEY_WIDTH, GLA_HEAD_V)

    r = lax.broadcasted_iota(i32, (chunk, chunk), 0)
    c = lax.broadcasted_iota(i32, (chunk, chunk), 1)
    causal = c <= r
    tril = causal.astype(bf16)
    ones = jnp.ones((chunk, GLA_HEAD_V), bf16)
    lane_head = lax.broadcasted_iota(i32, (1, GLA_KEY_WIDTH), 1) // GLA_HEAD_K
    zero = jnp.zeros((), bf16)
    mid = chunk // 2
    state = s_scr[...]
    for ch in range(cps):
        t = slice(ch * chunk, (ch + 1) * chunk)
        g = la_ref[t, :]
        g1 = g.astype(bf16)
        r1 = g - g1.astype(f32)
        g2 = r1.astype(bf16)
        g3 = (r1 - g2.astype(f32)).astype(bf16)
        b = (jnp.dot(tril, g1, preferred_element_type=f32) + jnp.dot(tril, g2, preferred_element_type=f32)
             + jnp.dot(tril, g3, preferred_element_type=f32))
        b_mid = b[mid:mid + 1, :]
        b_last = b[chunk - 1:chunk, :]
        q = q_ref[t, :]
        k = k_ref[t, :]
        qs = (q * jnp.exp(b - b_mid)).astype(bf16)
        ks = (k * jnp.exp(b_mid - b)).astype(bf16)
        qe = (q * jnp.exp(b)).astype(bf16)
        kd = (k * jnp.exp(b_last - b)).astype(bf16)
        decay = jnp.exp(lax.dot_general(g1, ones, _TN, preferred_element_type=f32)
                        + lax.dot_general(g2, ones, _TN, preferred_element_type=f32)
                        + lax.dot_general(g3, ones, _TN, preferred_element_type=f32))
        state_b = state.astype(bf16)
        head_rows = lambda x: jnp.concatenate([jnp.where(lane_head == h, x, zero) for h in range(GLA_HEADS)], axis=0)
        a_all = lax.dot_general(head_rows(qs), ks, _NT, preferred_element_type=f32)
        o_state = jnp.dot(head_rows(qe), state_b, preferred_element_type=f32)
        v = v_ref[t, :].astype(bf16)
        upd_all = lax.dot_general(kd, v, _TN, preferred_element_type=f32)
        upd = []
        for h in range(GLA_HEADS):
            trow = slice(h * chunk, (h + 1) * chunk)
            rows = slice(h * GLA_HEAD_K, (h + 1) * GLA_HEAD_K)
            cols = slice(h * GLA_HEAD_V, (h + 1) * GLA_HEAD_V)
            a = jnp.where(causal, a_all[trow, :], 0.0).astype(bf16)
            o_ref[t, cols] = o_state[trow, :] + jnp.dot(a, v[:, cols], preferred_element_type=f32)
            upd.append(upd_all[rows, cols])
        state = state * decay + jnp.concatenate(upd, axis=0)
    s_scr[...] = state

    @pl.when(si == n_steps - 1)
    def _():
        sout_ref[0] = state.reshape(GLA_HEADS, GLA_HEAD_K, GLA_HEAD_V)


def _gla(gq, gk, gv, la, s0, *, bsz, length, chunk, cps):
    n_steps = length // (chunk * cps)
    rows = chunk * cps
    row = lambda w: pl.BlockSpec((rows, w), lambda b, s: (b * n_steps + s, 0))
    st = pl.BlockSpec((1, GLA_HEADS, GLA_HEAD_K, GLA_HEAD_V), lambda b, s: (b, 0, 0, 0))
    return pl.pallas_call(
        functools.partial(_gla_body, chunk=chunk, cps=cps, n_steps=n_steps),
        grid=(bsz, n_steps),
        in_specs=[row(GLA_KEY_WIDTH), row(GLA_KEY_WIDTH), row(GLA_VALUE_WIDTH), row(GLA_KEY_WIDTH), st],
        out_specs=[row(GLA_VALUE_WIDTH), st],
        out_shape=[jax.ShapeDtypeStruct((bsz * length, GLA_VALUE_WIDTH), f32),
                   jax.ShapeDtypeStruct((bsz, GLA_HEADS, GLA_HEAD_K, GLA_HEAD_V), f32)],
        scratch_shapes=[pltpu.VMEM((GLA_KEY_WIDTH, GLA_HEAD_V), f32)],
        compiler_params=_cparams(("arbitrary", "arbitrary")),
        name="gla",
    )(gq, gk, gv, la, s0)


_HG = 4


_HGW = _HG * FOX_HEAD_DIM
_FOX_RB, _FOX_CB = 256, 256


def _foxp_body(q_ref, k_ref, v_ref, cq_ref, ck_ref, o_ref, qm_scr, vone_scr, cqb_scr, m_scr, sh_scr, acc_scr, *, t):
    qi = pl.program_id(2)
    n_tiles = k_ref.shape[1]
    lane_head = lax.broadcasted_iota(i32, (1, _HGW), 1) // FOX_HEAD_DIM

    @pl.when(qi == 0)
    def _():
        row_head = lax.broadcasted_iota(i32, (_HGW, 1), 0) // FOX_HEAD_DIM
        for kt in range(n_tiles):
            vt = v_ref[0, kt].reshape(_HGW, t)
            for j in range(_HG):
                vone_scr[j, kt] = jnp.where(row_head == j, vt, jnp.ones((), bf16))

    q = q_ref[...]
    for j in range(_HG):
        qm_scr[j] = jnp.where(lane_head == j, q, jnp.zeros((), bf16))
        cqb_scr[j] = jnp.broadcast_to(cq_ref[0, 0, :, j:j + 1], (t, 128))
    m_scr[...] = jnp.full(m_scr.shape, -jnp.inf, f32)
    acc_scr[...] = jnp.zeros(acc_scr.shape, f32)

    def logits(ki, j, rb, cb, diagonal):
        r0, c0 = rb * _FOX_RB, cb * _FOX_CB
        if diagonal and c0 > r0 + _FOX_RB - 1:
            return None
        kt = k_ref[0, ki, :, :, c0:c0 + _FOX_CB].reshape(_HGW, _FOX_CB)
        u = jnp.dot(qm_scr[j, r0:r0 + _FOX_RB, :], kt, preferred_element_type=f32) - ck_ref[0, 0, ki, j:j + 1, c0:c0 + _FOX_CB]
        if diagonal and c0 + _FOX_CB - 1 > r0:
            qpos = r0 + lax.broadcasted_iota(i32, (_FOX_RB, _FOX_CB), 0)
            kpos = c0 + lax.broadcasted_iota(i32, (_FOX_RB, _FOX_CB), 1)
            u = jnp.where(kpos <= qpos, u, -jnp.inf)
        return u

    def max_pass(ki, diagonal):
        for j in range(_HG):
            for cb in range(t // _FOX_CB):
                for rb in range(t // _FOX_RB):
                    u = logits(ki, j, rb, cb, diagonal)
                    if u is not None:
                        rows = slice(rb * _FOX_RB, (rb + 1) * _FOX_RB)
                        m = m_scr[j, rows, :]
                        for c in range(_FOX_CB // 128):
                            m = jnp.maximum(m, u[:, c * 128:(c + 1) * 128])
                        m_scr[j, rows, :] = m

    def sum_pass(ki, diagonal):
        for j in range(_HG):
            for cb in range(t // _FOX_CB):
                vt = vone_scr[j, ki, :, cb * _FOX_CB:(cb + 1) * _FOX_CB]
                for rb in range(t // _FOX_RB):
                    u = logits(ki, j, rb, cb, diagonal)
                    if u is not None:
                        rows = slice(rb * _FOX_RB, (rb + 1) * _FOX_RB)
                        p = jnp.exp(u - jnp.concatenate([sh_scr[j, rows, :]] * (_FOX_CB // 128), axis=-1))
                        acc_scr[j, rows, :] += lax.dot_general(p.astype(bf16), vt, _NT, preferred_element_type=f32)

    lax.fori_loop(0, qi, lambda ki, c: (max_pass(ki, False), c)[1], 0)
    for j in range(_HG):
        for rb in range(t // _FOX_RB):
            rows = slice(rb * _FOX_RB, (rb + 1) * _FOX_RB)
            m = m_scr[j, rows, :]
            for cb in range(t // _FOX_CB):
                u = logits(qi, j, rb, cb, True)
                if u is not None:
                    for c in range(_FOX_CB // 128):
                        m = jnp.maximum(m, u[:, c * 128:(c + 1) * 128])
            cq = cqb_scr[j, rows, :]
            m_full = jnp.max(m, axis=-1, keepdims=True) + cq
            sh_scr[j, rows, :] = m_full - cq
    lax.fori_loop(0, qi, lambda ki, c: (sum_pass(ki, False), c)[1], 0)
    sum_pass(qi, True)

    out = jnp.zeros((t, _HGW), f32)
    for j in range(_HG):
        acc = acc_scr[j]
        other = acc[:, 128:256] if j < _HG // 2 else acc[:, 0:128]
        inv = 1.0 / other
        out = jnp.where(lane_head == j, acc * jnp.concatenate([inv, inv], axis=-1), out)
    o_ref[...] = out


def _fox_prompt(fq, kt, vt, c_rows, c_cols, *, bsz, length, t=512):
    n = length // t
    kv = pl.BlockSpec((1, n, _HG, FOX_HEAD_DIM, t), lambda b, g, qi: (b, 0, g, 0, 0))
    return pl.pallas_call(
        functools.partial(_foxp_body, t=t),
        grid=(bsz, FOX_HEADS // _HG, n),
        in_specs=[pl.BlockSpec((t, _HGW), lambda b, g, qi: (b * n + qi, g)), kv, kv,
                  pl.BlockSpec((1, 1, t, _HG), lambda b, g, qi: (b, g, qi, 0)),
                  pl.BlockSpec((1, 1, n, _HG, t), lambda b, g, qi: (b, g, 0, 0, 0))],
        out_specs=pl.BlockSpec((t, _HGW), lambda b, g, qi: (b * n + qi, g)),
        out_shape=jax.ShapeDtypeStruct((bsz * length, FOX_WIDTH), f32),
        scratch_shapes=[pltpu.VMEM((_HG, t, _HGW), bf16), pltpu.VMEM((_HG, n, _HGW, t), bf16),
                        pltpu.VMEM((_HG, t, 128), f32), pltpu.VMEM((_HG, t, 128), f32),
                        pltpu.VMEM((_HG, t, 128), f32), pltpu.VMEM((_HG, t, _HGW), f32)],
        compiler_params=_cparams(("arbitrary",) * 3),
        name="fox_prompt",
    )(fq, kt, vt, c_rows, c_cols)


def _foxs_body(kc_ref, vc_ref, kn_ref, vn_ref, qbd_ref, ckp_ref, ckn_ref, cq_ref, o_ref, s_scr,
               *, past, new, kchunk):
    nrow = FOX_HEADS * new
    qbd = qbd_ref[0]
    cq = cq_ref[0]

    def per_head_rows(c8):
        return jnp.concatenate([jnp.broadcast_to(c8[h:h + 1, :], (new, c8.shape[1])) for h in range(FOX_HEADS)], axis=0)

    n_chunks = past // kchunk
    m = jnp.full((nrow, 1), -jnp.inf, f32)
    for c in range(n_chunks):
        cols = slice(c * kchunk, (c + 1) * kchunk)
        kt = kc_ref[0, :, :, cols].reshape(FOX_WIDTH, kchunk).astype(bf16)
        s = jnp.dot(qbd, kt, preferred_element_type=f32) + cq - per_head_rows(ckp_ref[0, :, cols])
        s_scr[:, cols] = s
        m = jnp.maximum(m, jnp.max(s, axis=-1, keepdims=True))
    sn = lax.dot_general(qbd, kn_ref[0].astype(bf16), _NT, preferred_element_type=f32)
    sn = sn + cq - per_head_rows(ckn_ref[0])
    qry_i = lax.broadcasted_iota(i32, (nrow, new), 0) % new
    key_j = lax.broadcasted_iota(i32, (nrow, new), 1)
    sn = jnp.where(key_j <= qry_i, sn, -jnp.inf)
    m = jnp.maximum(m, jnp.max(sn, axis=-1, keepdims=True))
    pn = jnp.exp(sn - m)
    l = jnp.sum(pn, axis=-1, keepdims=True)
    for c in range(n_chunks):
        cols = slice(c * kchunk, (c + 1) * kchunk)
        p = jnp.exp(s_scr[:, cols] - m)
        s_scr[:, cols] = p
        l = l + jnp.sum(p, axis=-1, keepdims=True)
    inv = 1.0 / l
    acc = jnp.dot((pn * inv).astype(bf16), vn_ref[0].astype(bf16), preferred_element_type=f32)
    for c in range(n_chunks):
        cols = slice(c * kchunk, (c + 1) * kchunk)
        vt = vc_ref[0, :, :, cols].reshape(FOX_WIDTH, kchunk).astype(bf16)
        acc = acc + lax.dot_general((s_scr[:, cols] * inv).astype(bf16), vt, _NT, preferred_element_type=f32)
    for h in range(FOX_HEADS):
        o_ref[0, :, h * FOX_HEAD_DIM:(h + 1) * FOX_HEAD_DIM] = (
            acc[h * new:(h + 1) * new, h * FOX_HEAD_DIM:(h + 1) * FOX_HEAD_DIM])


def _fox_sample(kc_t, vc_t, kn, vn, qbd, ckp, ckn, cq, *, kchunk=512):
    bsz, _, _, past = kc_t.shape
    new, w = kn.shape[1], kn.shape[2]
    blk = lambda a: pl.BlockSpec((1,) + a.shape[1:], lambda b: (b,) + (0,) * (a.ndim - 1))
    return pl.pallas_call(
        functools.partial(_foxs_body, past=past, new=new, kchunk=kchunk),
        grid=(bsz,),
        in_specs=[blk(kc_t), blk(vc_t), blk(kn), blk(vn), blk(qbd), blk(ckp), blk(ckn), blk(cq)],
        out_specs=pl.BlockSpec((1, new, w), lambda b: (b, 0, 0)),
        out_shape=jax.ShapeDtypeStruct((bsz, new, w), f32),
        scratch_shapes=[pltpu.VMEM((FOX_HEADS * new, past), f32)],
        compiler_params=_cparams(("arbitrary",)),
        name="fox_sample",
    )(kc_t, vc_t, kn, vn, qbd, ckp, ckn, cq)


def _route(logits_t, b_router):
    ne, tm = logits_t.shape
    gsz = ne // N_GROUPS
    shape3 = (N_GROUPS, gsz, tm)
    scores = jax.nn.sigmoid(logits_t).reshape(shape3)
    choice = scores + b_router
    im = lax.broadcasted_iota(i32, shape3, 1)
    ig = lax.broadcasted_iota(i32, (N_GROUPS, 1, tm), 0)
    ie = lax.broadcasted_iota(i32, shape3, 0) * gsz + im
    neg = -jnp.inf
    m1 = jnp.max(choice, axis=1, keepdims=True)
    first = jnp.min(jnp.where(choice == m1, im, gsz), axis=1, keepdims=True)
    m2 = jnp.max(jnp.where(im == first, neg, choice), axis=1, keepdims=True)
    cur = m1 + m2
    gsel = jnp.zeros(cur.shape, i32)
    for _ in range(TOPK_GROUPS):
        mx = jnp.max(cur, axis=0, keepdims=True)
        hit = ig == jnp.min(jnp.where(cur == mx, ig, N_GROUPS), axis=0, keepdims=True)
        gsel = jnp.where(hit, 1, gsel)
        cur = jnp.where(hit, neg, cur)
    cur = jnp.where(jnp.broadcast_to(gsel, shape3) > 0, choice, neg)
    ids, ws = [], []
    for _ in range(TOP_K):
        mx = jnp.max(jnp.max(cur, axis=1, keepdims=True), axis=0, keepdims=True)
        cand = jnp.where(cur == mx, ie, ne)
        fi = jnp.min(jnp.min(cand, axis=1, keepdims=True), axis=0, keepdims=True)
        hit = ie == fi
        w = jnp.where(hit, scores, 0.0)
        ids.append(fi.reshape(1, tm))
        ws.append(jnp.sum(jnp.sum(w, axis=1, keepdims=True), axis=0, keepdims=True).reshape(1, tm))
        cur = jnp.where(hit, neg, cur)
    tot = ws[0]
    for w in ws[1:]:
        tot = tot + w
    scale = ROUTED_SCALE / (tot + 1e-20)
    return ids, [w * scale for w in ws]


def _outproj_body(og_ref, gg_ref, of_ref, x_ref, gt1_ref, sh2_ref, sc2_ref, ggla_ref, gfox_ref, gffn_ref,
                  wout_ref, wrh_ref, wrl_ref, br_ref, x1_ref, hp_ref, wts_ref, lpos_ref, lst_ref, seg_ref, *, nb, rpb):
    tm = nb * rpb
    og = og_ref[...]
    parts = []
    for h in range(GLA_HEADS):
        seg = og[:, h * GLA_HEAD_V:(h + 1) * GLA_HEAD_V]
        parts.append(seg * lax.rsqrt(jnp.mean(seg * seg, axis=-1, keepdims=True) + NORM_EPS))
    gla_n = jnp.concatenate(parts, axis=-1) * ggla_ref[...] * _silu(gg_ref[...])
    of = of_ref[...]
    fox_n = of * lax.rsqrt(jnp.mean(of * of, axis=-1, keepdims=True) + NORM_EPS) * gfox_ref[...]
    merged = jnp.concatenate([gla_n, fox_n], axis=-1).astype(bf16)
    mix = jnp.dot(merged, wout_ref[...], preferred_element_type=f32)
    x1 = (x_ref[...].reshape(nb, rpb, D_MODEL) + gt1_ref[...] * mix.reshape(nb, rpb, D_MODEL))
    x1_ref[...] = x1.reshape(tm, D_MODEL)
    x1f = x1.reshape(tm, D_MODEL)
    y = x1f * lax.rsqrt(jnp.mean(x1f * x1f, axis=-1, keepdims=True) + NORM_EPS) * gffn_ref[...]
    h2 = (y.reshape(nb, rpb, D_MODEL) * (1.0 + sc2_ref[...]) + sh2_ref[...]).reshape(tm, D_MODEL)
    h_hi = h2.astype(bf16)
    hp_ref[...] = h_hi
    h_lo = (h2 - h_hi.astype(f32)).astype(bf16)
    logits_t = (lax.dot_general(wrh_ref[...], h_hi, _NT, preferred_element_type=f32)
                + lax.dot_general(wrh_ref[...], h_lo, _NT, preferred_element_type=f32)
                + lax.dot_general(wrl_ref[...], h_hi, _NT, preferred_element_type=f32))
    ids, ws = _route(logits_t, br_ref[...])
    for j in range(TOP_K):
        wts_ref[j:j + 1, :] = ws[j]

    ie = lax.broadcasted_iota(i32, (N_EXPERTS, tm), 0)
    multihot = jnp.zeros((N_EXPERTS, tm), f32)
    for j in range(TOP_K):
        multihot = multihot + (ie == ids[j]).astype(f32)
    seg = jnp.maximum(jnp.ceil(jnp.sum(multihot, axis=1, keepdims=True) * (1.0 / SEG)), 1.0)
    r = lax.broadcasted_iota(i32, (N_EXPERTS, N_EXPERTS), 0)
    c = lax.broadcasted_iota(i32, (N_EXPERTS, N_EXPERTS), 1)
    lstart = jnp.dot((c < r).astype(bf16), jnp.broadcast_to(seg, (N_EXPERTS, 128)).astype(bf16),
                     preferred_element_type=f32)[:, :1]
    rr = lax.broadcasted_iota(i32, (tm, tm), 0)
    cc = lax.broadcasted_iota(i32, (tm, tm), 1)
    earlier = jnp.dot(multihot.astype(bf16), (rr < cc).astype(bf16), preferred_element_type=f32)
    pos = lstart * float(SEG) + earlier
    for j in range(TOP_K):
        lpos_ref[j:j + 1, :] = jnp.sum(jnp.where(ie == ids[j], pos, 0.0), axis=0, keepdims=True).astype(i32)
    lst_ref[0] = lstart.astype(i32)
    seg_ref[0] = seg.astype(i32)


def _outproj(og, gg, of, x2, m3, g_gla4, g_fox, g_ffn, w_out, w_rt_hi, w_rt_lo, b_r, *, nb, rpb, tiles_per_batch):
    t, d = x2.shape
    tm = nb * rpb
    assert tm == GROUP
    row = lambda w: pl.BlockSpec((tm, w), lambda i: (i, 0))
    col = pl.BlockSpec((TOP_K, tm), lambda i: (0, i))
    table = pl.BlockSpec((1, N_EXPERTS, 1), lambda i: (i, 0, 0))
    mod = lambda c: pl.BlockSpec((nb, 1, d), lambda i, c=c: (i // tiles_per_batch, 0, c))
    full = lambda a: pl.BlockSpec(a.shape, lambda i: (0,) * a.ndim)
    return pl.pallas_call(
        functools.partial(_outproj_body, nb=nb, rpb=rpb),
        grid=(t // tm,),
        in_specs=[row(GLA_VALUE_WIDTH), row(GLA_VALUE_WIDTH), row(FOX_WIDTH), row(d), mod(2), mod(3), mod(4),
                  full(g_gla4), full(g_fox), full(g_ffn), full(w_out), full(w_rt_hi), full(w_rt_lo), full(b_r)],
        out_specs=[row(d), row(d), col, col, table, table],
        out_shape=[jax.ShapeDtypeStruct((t, d), f32), jax.ShapeDtypeStruct((t, d), bf16),
                   jax.ShapeDtypeStruct((TOP_K, t), f32), jax.ShapeDtypeStruct((TOP_K, t), i32),
                   jax.ShapeDtypeStruct((t // tm, N_EXPERTS, 1), i32), jax.ShapeDtypeStruct((t // tm, N_EXPERTS, 1), i32)],
        compiler_params=_cparams(("arbitrary",)),
        name="outproj",
    )(og, gg, of, x2, m3, m3, m3, g_gla4, g_fox, g_ffn, w_out, w_rt_hi, w_rt_lo, b_r)


def _group_rows(lst_ref, seg_ref, g):
    k = g * N_EXPERTS + N_EXPERTS - 1
    return (lst_ref[k] + seg_ref[k]) * SEG


def _start_segment_copies(lst_ref, seg_ref, dst_ref, g, make_copy, experts=range(N_EXPERTS)):
    for e in experts:
        k = g * N_EXPERTS + e
        make_copy(pl.multiple_of(lst_ref[k] * SEG, SEG), pl.multiple_of(dst_ref[k] * SEG, SEG),
                  seg_ref[k] * SEG).start()


_CHUNKS_PER_TRIP = 2
_MIN_TRIPS = TOP_K * GROUP // (_CHUNKS_PER_TRIP * PCHUNK)
_EXPERTS_PER_TRIP = N_EXPERTS // _MIN_TRIPS


def _staged_trips(n_rows, pair, issue):
    for c in range(_MIN_TRIPS):
        pair(c)
        if issue is not None:
            issue(range(c * _EXPERTS_PER_TRIP, (c + 1) * _EXPERTS_PER_TRIP))
    lax.fori_loop(_MIN_TRIPS, _chunk_trips(n_rows), lambda c, carry: (pair(c), carry)[1], 0)


def _placement_rows(lpos, vals, r0):
    rid = lax.broadcasted_iota(i32, (PCHUNK, GROUP), 0).astype(f32).astype(bf16)
    rel = (lpos - r0).astype(f32).astype(bf16)
    p = jnp.zeros((PCHUNK, GROUP), bf16)
    for j in range(TOP_K):
        v = jnp.ones((), bf16) if vals is None else vals[j:j + 1, :]
        p = jnp.where(rel[j:j + 1, :] == rid, v, p)
    return p


def _chunk_trips(n_rows):
    return (n_rows + _CHUNKS_PER_TRIP * PCHUNK - 1) // (_CHUNKS_PER_TRIP * PCHUNK)


def _dispatch_body(lst_ref, seg_ref, dst_ref, tail0_ref, tailn_ref, misc_ref, lpos_ref, hp_ref, hs_ref, xs_hbm,
                   stage, zeros, sem, zsem, *, ngp, ng, n_blocks):
    s = pl.program_id(0)
    slot = s % 2
    h = jnp.where(s < ngp, hp_ref[...], hs_ref[...])
    lpos = lpos_ref[...]

    def pair(c):
        r0 = pl.multiple_of(c * _CHUNKS_PER_TRIP * PCHUNK, _CHUNKS_PER_TRIP * PCHUNK)
        p = jnp.concatenate([_placement_rows(lpos, None, r0 + k * PCHUNK) for k in range(_CHUNKS_PER_TRIP)], axis=0)
        stage[slot, pl.ds(r0, _CHUNKS_PER_TRIP * PCHUNK), :] = jnp.dot(p, h, preferred_element_type=f32).astype(bf16)

    def issue_previous(experts):
        _start_segment_copies(lst_ref, seg_ref, dst_ref, s - 1, lambda a, d, n: pltpu.make_async_copy(
            stage.at[1 - slot, pl.ds(a, n), :], xs_hbm.at[pl.ds(d, n), :], sem.at[1 - slot]), experts)

    def wait_group(gg, sl):
        n = _group_rows(lst_ref, seg_ref, gg)
        pltpu.make_async_copy(stage.at[sl, pl.ds(0, n), :], xs_hbm.at[pl.ds(0, n), :], sem.at[sl]).wait()

    @pl.when(s >= 2)
    def _():
        wait_group(s - 2, slot)

    @pl.when(s == 0)
    def _():
        _staged_trips(_group_rows(lst_ref, seg_ref, s), pair, None)

    @pl.when(jnp.logical_and(s > 0, s < ng))
    def _():
        _staged_trips(_group_rows(lst_ref, seg_ref, s), pair, issue_previous)

    @pl.when(s == ng)
    def _():
        issue_previous(range(N_EXPERTS))
        wait_group(s - 1, 1 - slot)
        zeros[...] = jnp.zeros(zeros.shape, bf16)

        def per_expert(e, carry):
            n = tailn_ref[e] * SEG

            @pl.when(n > 0)
            def _():
                pltpu.make_async_copy(zeros.at[pl.ds(0, n), :],
                                      xs_hbm.at[pl.ds(pl.multiple_of(tail0_ref[e] * SEG, SEG), n), :], zsem).start()

            return carry

        lax.fori_loop(0, N_EXPERTS, per_expert, 0)
        n_used = misc_ref[0]

        def per_block(b, carry):
            pltpu.make_async_copy(zeros, xs_hbm.at[pl.ds(pl.multiple_of(b * EXPERT_ROWS, EXPERT_ROWS), EXPERT_ROWS), :],
                                  zsem).start()
            return carry

        lax.fori_loop(n_used, n_blocks, per_block, 0)
        n_zero = misc_ref[1] * SEG + (n_blocks - n_used) * EXPERT_ROWS

        @pl.when(n_zero > 0)
        def _():
            pltpu.make_async_copy(xs_hbm.at[pl.ds(0, n_zero), :], xs_hbm.at[pl.ds(0, n_zero), :], zsem).wait()


def _dispatch(tables, tail0, tailn, misc, lpos, hp, hs, *, n_blocks):
    ngp, ngs = hp.shape[0] // GROUP, hs.shape[0] // GROUP
    ng = ngp + ngs
    d = hp.shape[1]
    return pl.pallas_call(
        functools.partial(_dispatch_body, ngp=ngp, ng=ng, n_blocks=n_blocks),
        grid_spec=pltpu.PrefetchScalarGridSpec(
            num_scalar_prefetch=6,
            grid=(ng + 1,),
            in_specs=[pl.BlockSpec((TOP_K, GROUP), lambda g, *_: (0, jnp.minimum(g, ng - 1))),
                      pl.BlockSpec((GROUP, d), lambda g, *_: (jnp.minimum(g, ngp - 1), 0)),
                      pl.BlockSpec((GROUP, d), lambda g, *_: (jnp.clip(g - ngp, 0, ngs - 1), 0))],
            out_specs=pl.BlockSpec(memory_space=pl.ANY),
            scratch_shapes=[pltpu.VMEM((2, STAGE_ROWS, d), bf16), pltpu.VMEM((EXPERT_ROWS, d), bf16),
                            pltpu.SemaphoreType.DMA((2,)), pltpu.SemaphoreType.DMA]),
        out_shape=jax.ShapeDtypeStruct((n_blocks * EXPERT_ROWS, d), bf16),
        compiler_params=_cparams(("arbitrary",)),
        name="dispatch",
    )(*tables, tail0, tailn, misc, lpos, hp, hs)


def _experts_body(be_ref, nu_ref, xs_ref, wg_ref, wu_ref, wd_ref, y_ref, wg_s, wu_s, wd_s):
    i = pl.program_id(0)
    changed = jnp.logical_or(i == 0, be_ref[i] != be_ref[jnp.maximum(i - 1, 0)])

    @pl.when(changed)
    def _():
        wg_s[...] = wg_ref[0].astype(bf16)
        wu_s[...] = wu_ref[0].astype(bf16)
        wd_s[...] = wd_ref[0].astype(bf16)

    @pl.when(i < nu_ref[0])
    def _():
        x = xs_ref[...]
        g = jnp.dot(x, wg_s[...], preferred_element_type=f32)
        u = jnp.dot(x, wu_s[...], preferred_element_type=f32)
        a = (_silu(g) * u).astype(bf16)
        y_ref[...] = jnp.dot(a, wd_s[...], preferred_element_type=f32).astype(bf16)

    @pl.when(i >= nu_ref[0])
    def _():
        y_ref[...] = jnp.zeros(y_ref.shape, bf16)


def _experts(blk_e, n_used, xs, w_gate, w_up, w_down, *, block_rows):
    n_rows, d = xs.shape
    n_blocks = n_rows // block_rows
    de = w_gate.shape[2]
    return pl.pallas_call(
        _experts_body,
        grid_spec=pltpu.PrefetchScalarGridSpec(
            num_scalar_prefetch=2,
            grid=(n_blocks,),
            in_specs=[pl.BlockSpec((block_rows, d), lambda i, be, nu: (jnp.minimum(i, nu[0] - 1), 0)),
                      pl.BlockSpec((1, d, de), lambda i, be, nu: (be[i], 0, 0)),
                      pl.BlockSpec((1, d, de), lambda i, be, nu: (be[i], 0, 0)),
                      pl.BlockSpec((1, de, d), lambda i, be, nu: (be[i], 0, 0))],
            out_specs=pl.BlockSpec((block_rows, d), lambda i, be, nu: (i, 0)),
            scratch_shapes=[pltpu.VMEM((d, de), bf16), pltpu.VMEM((d, de), bf16), pltpu.VMEM((de, d), bf16)]),
        out_shape=jax.ShapeDtypeStruct((n_rows, d), bf16),
        compiler_params=_cparams(("arbitrary",)),
        name="experts",
    )(blk_e, n_used, xs, w_gate, w_up, w_down)


def _combine_body(lst_ref, seg_ref, dst_ref, lpos_ref, wts_ref, h_ref, x1_ref, gt2_ref, wsg_ref, wsu_ref, wsd_ref,
                  gfin_ref, y_hbm, o_ref, ybuf, acc, sem, *, nb, rpb, g0, ng):
    i = pl.program_id(0)
    g = g0 + i
    slot = i % 2

    def start_gather(gg, sl, experts=range(N_EXPERTS)):
        _start_segment_copies(lst_ref, seg_ref, dst_ref, gg, lambda s, d, n: pltpu.make_async_copy(
            y_hbm.at[pl.ds(d, n), :], ybuf.at[sl, pl.ds(s, n), :], sem.at[sl]), experts)

    @pl.when(i == 0)
    def _():
        ybuf[...] = jnp.zeros(ybuf.shape, bf16)
        start_gather(g, slot)

    h = h_ref[...]
    gate = jnp.dot(h, wsg_ref[...], preferred_element_type=f32)
    up = jnp.dot(h, wsu_ref[...], preferred_element_type=f32)
    acc[...] = jnp.dot((_silu(gate) * up).astype(bf16), wsd_ref[...], preferred_element_type=f32)

    n_rows = _group_rows(lst_ref, seg_ref, g)
    pltpu.make_async_copy(y_hbm.at[pl.ds(0, n_rows), :], ybuf.at[slot, pl.ds(0, n_rows), :], sem.at[slot]).wait()

    lpos = lpos_ref[...]
    wts = wts_ref[...].astype(bf16)

    def pair(c):
        r0 = pl.multiple_of(c * _CHUNKS_PER_TRIP * PCHUNK, _CHUNKS_PER_TRIP * PCHUNK)
        wt = jnp.concatenate([_placement_rows(lpos, wts, r0 + k * PCHUNK) for k in range(_CHUNKS_PER_TRIP)], axis=0)
        acc[...] += lax.dot_general(wt, ybuf[slot, pl.ds(r0, _CHUNKS_PER_TRIP * PCHUNK), :], _TN,
                                    preferred_element_type=f32)

    @pl.when(i + 1 < ng)
    def _():
        _staged_trips(n_rows, pair, lambda experts: start_gather(g + 1, 1 - slot, experts))

    @pl.when(i + 1 == ng)
    def _():
        _staged_trips(n_rows, pair, None)

    out = (x1_ref[...].reshape(nb, rpb, D_MODEL) + gt2_ref[...] * acc[...].reshape(nb, rpb, D_MODEL))
    out = out.reshape(GROUP, D_MODEL)
    o_ref[...] = out * lax.rsqrt(jnp.mean(out * out, axis=-1, keepdims=True) + NORM_EPS) * gfin_ref[...]


def _combine(tables, lpos_rows, wts_rows, y, h2, x1, m3, w_sg, w_su, w_sd, g_fin, *, nb, rpb, tiles_per_batch, g0):
    t, d = x1.shape
    assert nb * rpb == GROUP
    ng = t // GROUP
    row = lambda w: pl.BlockSpec((GROUP, w), lambda i, *_: (i, 0))
    grow = lambda w: pl.BlockSpec((w, GROUP), lambda i, *_: (0, g0 + i))
    full = lambda a: pl.BlockSpec(a.shape, lambda i, *_: (0,) * a.ndim)
    return pl.pallas_call(
        functools.partial(_combine_body, nb=nb, rpb=rpb, g0=g0, ng=ng),
        grid_spec=pltpu.PrefetchScalarGridSpec(
            num_scalar_prefetch=3,
            grid=(ng,),
            in_specs=[grow(TOP_K), grow(TOP_K), row(d), row(d),
                      pl.BlockSpec((nb, 1, d), lambda i, *_: (i // tiles_per_batch, 0, 5)),
                      full(w_sg), full(w_su), full(w_sd), full(g_fin), pl.BlockSpec(memory_space=pl.ANY)],
            out_specs=row(d),
            scratch_shapes=[pltpu.VMEM((2, STAGE_ROWS, d), bf16), pltpu.VMEM((GROUP, d), f32),
                            pltpu.SemaphoreType.DMA((2,))]),
        out_shape=jax.ShapeDtypeStruct((t, d), f32),
        compiler_params=_cparams(("arbitrary",)),
        name="combine",
    )(*tables, lpos_rows, wts_rows, h2, x1, m3, w_sg, w_su, w_sd, g_fin, y)


_GLA_CHUNKS_PER_STEP = 8


def kernel(x_prompt, x_sample, state_gla, cache_fox_k, cache_fox_v, cache_fox_logf, c_prompt, c_sample, w_ada, b_ada, g_norm_mix, g_norm_ffn, w_in, w_gla_gk2, b_gla_gk, g_gla_out, b_fox_f, g_fox_out, w_out, w_router, b_router, w_exp_gate, w_exp_up, w_exp_down, w_sh_gate, w_sh_up, w_sh_down, g_final):
    assert w_ada.shape[0] == 1, "single-layer trunk"
    bp, lp, d = x_prompt.shape
    bs, ls, _ = x_sample.shape
    past = cache_fox_k.shape[2]
    tp, ts = bp * lp, bs * ls

    wi = w_in[0]
    o_glr = 2 * GLA_KEY_WIDTH + 2 * GLA_VALUE_WIDTH
    o_fq = o_glr + GLA_LOW_RANK
    o_ff = o_fq + 3 * FOX_WIDTH
    small = jnp.concatenate([wi[:, o_ff:o_ff + FOX_HEADS], wi[:, o_glr:o_fq],
                             jnp.zeros((d, 128 - FOX_HEADS - GLA_LOW_RANK), f32)], axis=1)
    w_cat = jnp.concatenate([wi[:, :o_glr], wi[:, o_fq:o_ff], small], axis=1).astype(bf16)
    w_t = jnp.concatenate([wi[:, o_fq + FOX_WIDTH:o_ff + FOX_HEADS], jnp.zeros((d, 16 - FOX_HEADS), f32)],
                          axis=1).T.astype(bf16)
    wgk_pad = jnp.zeros((128, GLA_KEY_WIDTH), f32).at[FOX_HEADS:FOX_HEADS + GLA_LOW_RANK].set(w_gla_gk2[0]).astype(bf16)
    bgk = b_gla_gk[0].reshape(1, GLA_KEY_WIDTH)
    bf_pad = jnp.zeros((1, 128), f32).at[0, :FOX_HEADS].set(b_fox_f[0])
    bf_col = jnp.zeros((16, 1), f32).at[:FOX_HEADS, 0].set(b_fox_f[0])
    g_mix = g_norm_mix[0].reshape(1, d)
    g_ffn = g_norm_ffn[0].reshape(1, d)
    g_gla4 = jnp.tile(g_gla_out[0], GLA_HEADS).reshape(1, GLA_VALUE_WIDTH)
    g_fox = g_fox_out[0].reshape(1, FOX_WIDTH)
    g_fin = g_final.reshape(1, d)
    w_out_b = w_out[0].astype(bf16)
    w_rt = w_router[0].T
    w_rt_hi = w_rt.astype(bf16)
    w_rt_lo = (w_rt - w_rt_hi.astype(f32)).astype(bf16)
    b_r = b_router[0].reshape(N_GROUPS, N_EXPERTS // N_GROUPS, 1)
    w_sg, w_su, w_sd = w_sh_gate[0].astype(bf16), w_sh_up[0].astype(bf16), w_sh_down[0].astype(bf16)

    m_all = _ada(jnp.concatenate([c_prompt, c_sample], axis=0), w_ada[0], b_ada[0])
    m3p = m_all[:bp].reshape(bp, 1, 6 * d)
    m3s = m_all[bp:].reshape(bs, 1, 6 * d)

    groups = []
    for (x, m3, bsz, length) in ((x_prompt, m3p, bp, lp), (x_sample, m3s, bs, ls)):
        if length >= GROUP:
            nb, rpb_in, tpb_in = 1, 512, length // 512
            rpb, tpb = GROUP, length // GROUP
        else:
            nb, rpb_in, tpb_in = GROUP // length, length, 1
            rpb, tpb = length, 1
        groups.append(dict(x2=x.reshape(bsz * length, d), m3=m3, bsz=bsz, length=length,
                           nb=nb, rpb_in=rpb_in, tpb_in=tpb_in, rpb=rpb, tpb=tpb))

    gp, gs = groups
    (gp["gq"], gp["gk"], gp["gv"], gp["gg"], gp["la"], gp["fq"], kt_p, vt_p, ktb_p, vtb_p, lft_p) = _inproj(
        gp["x2"], gp["m3"], g_mix, w_cat, wgk_pad, bgk, bf_pad, w_t, bf_col,
        nb=gp["nb"], rpb=gp["rpb_in"], tiles_per_batch=gp["tpb_in"], seq_minor=True)
    (gs["gq"], gs["gk"], gs["gv"], gs["gg"], gs["la"], gs["fq"], fk_s, fv_s, lf_s) = _inproj(
        gs["x2"], gs["m3"], g_mix, w_cat, wgk_pad, bgk, bf_pad, w_t, bf_col,
        nb=gs["nb"], rpb=gs["rpb_in"], tiles_per_batch=gs["tpb_in"], seq_minor=False)

    s0p = jnp.zeros((bp, GLA_HEADS, GLA_HEAD_K, GLA_HEAD_V), f32)
    gp["og"], sp_new = _gla(gp["gq"], gp["gk"], gp["gv"], gp["la"], s0p, bsz=bp, length=lp, chunk=GLA_CHUNK,
                            cps=_GLA_CHUNKS_PER_STEP)
    gs["og"], ss_new = _gla(gs["gq"], gs["gk"], gs["gv"], gs["la"], state_gla[0], bsz=bs, length=ls, chunk=ls, cps=1)

    c_p = _cumsum_lanes(lft_p)
    c_grp = c_p.reshape(bp, FOX_HEADS // _HG, _HG, lp)
    c_rows = c_grp.transpose(0, 1, 3, 2)
    t_att = gp["rpb_in"]
    c_cols = c_grp.reshape(bp, FOX_HEADS // _HG, _HG, lp // t_att, t_att).transpose(0, 1, 3, 2, 4)
    gp["of"] = _fox_prompt(gp["fq"], ktb_p, vtb_p, c_rows, c_cols, bsz=bp, length=lp, t=t_att)

    lf_s = lf_s.reshape(bs, ls, FOX_HEADS)
    lpad = -(-ls // 128) * 128
    c_s = _cumsum_lanes(jnp.pad(lf_s.transpose(0, 2, 1), ((0, 0), (0, 0), (0, lpad - ls))))[:, :, :ls]
    ck_past = _cumsum_lanes(cache_fox_logf[0].astype(f32).transpose(0, 2, 1), suffix=True)
    fq_s = gs["fq"].reshape(bs, ls, FOX_HEADS, FOX_HEAD_DIM)
    eye_h = jnp.eye(FOX_HEADS, dtype=bf16)
    qbd = jnp.einsum("bihd,hg->bhigd", fq_s, eye_h).reshape(bs, FOX_HEADS * ls, FOX_WIDTH)
    of_s = _fox_sample(cache_fox_k[0].transpose(0, 2, 3, 1), cache_fox_v[0].transpose(0, 2, 3, 1),
                       fk_s.reshape(bs, ls, FOX_WIDTH), fv_s.reshape(bs, ls, FOX_WIDTH), qbd,
                       ck_past, c_s, c_s.reshape(bs, FOX_HEADS * ls, 1))
    gs["of"] = of_s.reshape(ts, FOX_WIDTH)

    for gr in groups:
        gr["x1"], gr["hp"], gr["wts"], gr["lpos"], gr["lst"], gr["seg"] = _outproj(
            gr["og"], gr["gg"], gr["of"], gr["x2"], gr["m3"], g_gla4, g_fox, g_ffn, w_out_b, w_rt_hi, w_rt_lo, b_r,
            nb=gr["nb"] if gr["length"] < GROUP else 1, rpb=gr["rpb"], tiles_per_batch=gr["tpb"])

    t_all = tp + ts
    n_groups = t_all // GROUP
    tiles_per_block = EXPERT_ROWS // SEG
    n_blocks = -(-(t_all * TOP_K + n_groups * N_EXPERTS * SEG) // EXPERT_ROWS) + N_EXPERTS
    lpos = jnp.concatenate([gp["lpos"], gs["lpos"]], axis=1)
    lst = jnp.concatenate([gp["lst"], gs["lst"]], axis=0)[:, :, 0]
    seg = jnp.concatenate([gp["seg"], gs["seg"]], axis=0)[:, :, 0]
    tot_e = jnp.sum(seg, axis=0)
    blocks_per_e = (tot_e + tiles_per_block - 1) // tiles_per_block
    blk_end = jnp.cumsum(blocks_per_e)
    dst = (blk_end - blocks_per_e)[None, :] * tiles_per_block + jnp.cumsum(seg, axis=0) - seg
    tables = tuple(a.reshape(n_groups * N_EXPERTS) for a in (lst, seg, dst))
    n_used = blk_end[-1:]
    blk = jnp.minimum(jnp.arange(n_blocks, dtype=i32), n_used[0] - 1)
    blk_e = jnp.minimum(jnp.sum((blk_end[None, :] <= blk[:, None]).astype(i32), axis=1), N_EXPERTS - 1)
    tail0 = (blk_end - blocks_per_e) * tiles_per_block + tot_e
    tailn = blocks_per_e * tiles_per_block - tot_e
    misc = jnp.stack([n_used[0], jnp.sum(tailn)]).astype(i32)
    xs = _dispatch(tables, tail0, tailn, misc, lpos, gp["hp"], gs["hp"], n_blocks=n_blocks)
    y = _experts(blk_e, n_used.astype(i32), xs, w_exp_gate[0], w_exp_up[0], w_exp_down[0], block_rows=EXPERT_ROWS)

    wts_all = jnp.concatenate([gp["wts"], gs["wts"]], axis=1)
    outs = []
    for gr, g0 in ((gp, 0), (gs, tp // GROUP)):
        outs.append(_combine(tables, lpos, wts_all, y, gr["hp"], gr["x1"], gr["m3"], w_sg, w_su, w_sd, g_fin,
                             nb=gr["nb"] if gr["length"] < GROUP else 1, rpb=gr["rpb"],
                             tiles_per_batch=gr["tpb"], g0=g0))
    y_prompt = outs[0].reshape(bp, lp, d)
    y_sample = outs[1].reshape(bs, ls, d)

    return (y_prompt, y_sample, sp_new[None], ss_new[None],
            kt_p.transpose(0, 3, 1, 2)[None], vt_p.transpose(0, 3, 1, 2)[None], lft_p.transpose(0, 2, 1)[None],
            fk_s.reshape(1, bs, ls, FOX_HEADS, FOX_HEAD_DIM), fv_s.reshape(1, bs, ls, FOX_HEADS, FOX_HEAD_DIM),
            lf_s[None])
```

```python
import functools

import jax
import jax.numpy as jnp
from jax import lax
from jax.experimental import pallas as pl
from jax.experimental.pallas import tpu as pltpu

f32, bf16, i32 = jnp.float32, jnp.bfloat16, jnp.int32

D_MODEL = 1024
GLA_HEADS, GLA_HEAD_K, GLA_HEAD_V = 4, 64, 128
GLA_KEY_WIDTH, GLA_VALUE_WIDTH, GLA_LOW_RANK = 256, 512, 16
GLA_GATE_NORMALIZER = 16.0
GLA_CHUNK = 64
FOX_HEADS, FOX_HEAD_DIM, FOX_WIDTH = 8, 64, 512
N_EXPERTS, N_GROUPS, TOPK_GROUPS, TOP_K = 64, 8, 4, 8
D_EXPERT = 256
ROUTED_SCALE = 2.5
NORM_EPS = 1e-6
EXPERT_ROWS = 1024
GROUP = 256
SEG = 16
PCHUNK = 256
STAGE_ROWS = -(-(TOP_K * GROUP + N_EXPERTS * SEG) // (2 * PCHUNK)) * (2 * PCHUNK)
TILES_PER_BLOCK = EXPERT_ROWS // SEG
VMEM_LIMIT = 56 * 1024 * 1024


def _cparams(sem):
    return pltpu.CompilerParams(dimension_semantics=sem, vmem_limit_bytes=VMEM_LIMIT)


def _log_sigmoid(x):
    return jnp.minimum(x, 0.0) - jnp.log1p(jnp.exp(-jnp.abs(x)))


def _silu(x):
    return x * jax.nn.sigmoid(x)


def _ada_body(c_ref, w_ref, b_ref, o_ref):
    a = _silu(c_ref[...]).astype(bf16)
    o_ref[...] = jnp.dot(a, w_ref[...].astype(bf16), preferred_element_type=f32) + b_ref[...]


def _ada(c_all, w_ada, b_ada):
    nb, d = c_all.shape
    n = w_ada.shape[1]
    tn = 768
    return pl.pallas_call(
        _ada_body,
        grid=(n // tn,),
        in_specs=[pl.BlockSpec((nb, d), lambda j: (0, 0)),
                  pl.BlockSpec((d, tn), lambda j: (0, j)),
                  pl.BlockSpec((1, tn), lambda j: (0, j))],
        out_specs=pl.BlockSpec((nb, tn), lambda j: (0, j)),
        out_shape=jax.ShapeDtypeStruct((nb, n), f32),
        compiler_params=_cparams(("arbitrary",)),
        name="ada",
    )(c_all, w_ada, b_ada.reshape(1, n))


_C_GQ, _C_GK, _C_GV, _C_GG, _C_FQ, _C_FK, _C_FV, _C_SM, _C_END = 0, 256, 512, 1024, 1536, 2048, 2560, 3072, 3200


_NT = (((1,), (1,)), ((), ()))
_TN = (((0,), (0,)), ((), ()))


def _inproj_body(x_ref, sh_ref, sc_ref, g_ref, w_ref, wgk_ref, bgk_ref, bf_ref, wt_ref, bfc_ref,
                 gq_ref, gk_ref, gv_ref, gg_ref, la_ref, fq_ref, k_ref, v_ref, *rest, nb, rpb, seq_minor):
    tm = nb * rpb
    x = x_ref[...]
    y = x * lax.rsqrt(jnp.mean(x * x, axis=-1, keepdims=True) + NORM_EPS) * g_ref[...]
    h = (y.reshape(nb, rpb, D_MODEL) * (1.0 + sc_ref[...]) + sh_ref[...]).reshape(tm, D_MODEL).astype(bf16)

    def proj(a, b):
        return jnp.dot(h, w_ref[:, a:b], preferred_element_type=f32)

    gq_ref[...] = proj(_C_GQ, _C_GK) * (GLA_HEAD_K ** -0.5)
    gk_ref[...] = proj(_C_GK, _C_GV)
    gv_ref[...] = proj(_C_GV, _C_GG)
    gg_ref[...] = proj(_C_GG, _C_FQ)
    fq_ref[...] = (proj(_C_FQ, _C_FK) * (FOX_HEAD_DIM ** -0.5)).astype(bf16)
    zs = proj(_C_SM, _C_END)
    ga = jnp.dot(zs.astype(bf16), wgk_ref[...], preferred_element_type=f32) + bgk_ref[...]
    la_ref[...] = _log_sigmoid(ga) * (1.0 / GLA_GATE_NORMALIZER)
    if seq_minor:
        kb_ref, vb_ref, lf_ref = rest
        shape4 = (FOX_HEADS, FOX_HEAD_DIM, tm)
        kt = lax.dot_general(wt_ref[0:FOX_WIDTH, :], h, _NT, preferred_element_type=f32)
        k_ref[0] = kt.reshape(shape4)
        kb_ref[0] = kt.astype(bf16).reshape(shape4)
        vt = lax.dot_general(wt_ref[FOX_WIDTH:2 * FOX_WIDTH, :], h, _NT, preferred_element_type=f32)
        v_ref[0] = vt.reshape(shape4)
        vb_ref[0] = vt.astype(bf16).reshape(shape4)
        zf = lax.dot_general(wt_ref[2 * FOX_WIDTH:, :], h, _NT, preferred_element_type=f32)
        lf_ref[0] = _log_sigmoid(zf + bfc_ref[...])[:FOX_HEADS, :]
    else:
        (lf_ref,) = rest
        k_ref[...] = proj(_C_FK, _C_FV)
        v_ref[...] = proj(_C_FV, _C_SM)
        lf_ref[...] = _log_sigmoid(zs + bf_ref[...])[:, :FOX_HEADS]


def _inproj(x2, m3, g_mix, w_cat, wgk_pad, bgk, bf_pad, w_t, bf_col, *, nb, rpb, tiles_per_batch, seq_minor):
    t, d = x2.shape
    tm = nb * rpb
    tpb = tiles_per_batch
    bsz = t // (tm * tpb) if seq_minor else None
    row = lambda w: pl.BlockSpec((tm, w), lambda i: (i, 0))
    mod = lambda c: pl.BlockSpec((nb, 1, d), lambda i, c=c: (i // tpb, 0, c))
    full = lambda a: pl.BlockSpec(a.shape, lambda i: (0,) * a.ndim)
    widths = [(GLA_KEY_WIDTH, f32), (GLA_KEY_WIDTH, f32), (GLA_VALUE_WIDTH, f32), (GLA_VALUE_WIDTH, f32),
              (GLA_KEY_WIDTH, f32), (FOX_WIDTH, bf16)]
    out_specs = [row(w) for w, _ in widths]
    out_shape = [jax.ShapeDtypeStruct((t, w), dt) for w, dt in widths]
    if seq_minor:
        length = tm * tpb
        hd = pl.BlockSpec((1, FOX_HEADS, FOX_HEAD_DIM, tm), lambda i: (i // tpb, 0, 0, i % tpb))
        out_specs += [hd, hd, hd, hd, pl.BlockSpec((1, FOX_HEADS, tm), lambda i: (i // tpb, 0, i % tpb))]
        out_shape += [jax.ShapeDtypeStruct((bsz, FOX_HEADS, FOX_HEAD_DIM, length), f32)] * 2
        out_shape += [jax.ShapeDtypeStruct((bsz, FOX_HEADS, FOX_HEAD_DIM, length), bf16)] * 2
        out_shape += [jax.ShapeDtypeStruct((bsz, FOX_HEADS, length), f32)]
    else:
        out_specs += [row(FOX_WIDTH), row(FOX_WIDTH), row(FOX_HEADS)]
        out_shape += [jax.ShapeDtypeStruct((t, w), f32) for w in (FOX_WIDTH, FOX_WIDTH, FOX_HEADS)]
    return pl.pallas_call(
        functools.partial(_inproj_body, nb=nb, rpb=rpb, seq_minor=seq_minor),
        grid=(t // tm,),
        in_specs=[row(d), mod(0), mod(1), full(g_mix), full(w_cat), full(wgk_pad), full(bgk), full(bf_pad),
                  full(w_t), full(bf_col)],
        out_specs=out_specs,
        out_shape=out_shape,
        compiler_params=_cparams(("arbitrary",)),
        name="inproj",
    )(x2, m3, m3, g_mix, w_cat, wgk_pad, bgk, bf_pad, w_t, bf_col)


def _cumsum_body(x_ref, o_ref, *, chunk, suffix):
    n = x_ref.shape[-1]
    r = lax.broadcasted_iota(i32, (chunk, chunk), 0)
    c = lax.broadcasted_iota(i32, (chunk, chunk), 1)
    tri = (r <= c).astype(bf16)
    local = []
    for j in range(n // chunk):
        x = x_ref[0, :, j * chunk:(j + 1) * chunk]
        x1 = x.astype(bf16)
        r1 = x - x1.astype(f32)
        x2 = r1.astype(bf16)
        x3 = (r1 - x2.astype(f32)).astype(bf16)
        local.append(jnp.dot(x1, tri, preferred_element_type=f32) + jnp.dot(x2, tri, preferred_element_type=f32)
                     + jnp.dot(x3, tri, preferred_element_type=f32))
    offsets = [jnp.zeros((x_ref.shape[1], 1), f32)]
    for j in range(n // chunk):
        offsets.append(offsets[-1] + local[j][:, chunk - 1:chunk])
    total = offsets[-1] if suffix else 0.0
    for j in range(n // chunk):
        o_ref[0, :, j * chunk:(j + 1) * chunk] = local[j] + (offsets[j] - total)


def _cumsum_lanes(x, *, suffix=False):
    b, h, n = x.shape
    return pl.pallas_call(
        functools.partial(_cumsum_body, chunk=128, suffix=suffix),
        grid=(b,),
        in_specs=[pl.BlockSpec((1, h, n), lambda i: (i, 0, 0))],
        out_specs=pl.BlockSpec((1, h, n), lambda i: (i, 0, 0)),
        out_shape=jax.ShapeDtypeStruct((b, h, n), f32),
        compiler_params=_cparams(("arbitrary",)),
        name="cumsum",
    )(x)


def _gla_body(q_ref, k_ref, v_ref, la_ref, s0_ref, o_ref, sout_ref, s_scr, *, chunk, cps, n_steps):
    si = pl.program_id(1)

    @pl.when(si == 0)
    def _():
        s_scr[...] = s0_ref[0].reshape(GLA_KEY_WIDTH, GLA_HEAD_V)

    r = lax.broadcasted_iota(i32, (chunk, chunk), 0)
    c = lax.broadcasted_iota(i32, (chunk, chunk), 1)
    causal = c <= r
    tril = causal.astype(bf16)
    ones = jnp.ones((chunk, GLA_HEAD_V), bf16)
    lane_head = lax.broadcasted_iota(i32, (1, GLA_KEY_WIDTH), 1) // GLA_HEAD_K
    zero = jnp.zeros((), bf16)
    mid = chunk // 2
    state = s_scr[...]
    for ch in range(cps):
        t = slice(ch * chunk, (ch + 1) * chunk)
        g = la_ref[t, :]
        g1 = g.astype(bf16)
        r1 = g - g1.astype(f32)
        g2 = r1.astype(bf16)
        g3 = (r1 - g2.astype(f32)).astype(bf16)
        b = (jnp.dot(tril, g1, preferred_element_type=f32) + jnp.dot(tril, g2, preferred_element_type=f32)
             + jnp.dot(tril, g3, preferred_element_type=f32))
        b_mid = b[mid:mid + 1, :]
        b_last = b[chunk - 1:chunk, :]
        q = q_ref[t, :]
        k = k_ref[t, :]
        qs = (q * jnp.exp(b - b_mid)).astype(bf16)
        ks = (k * jnp.exp(b_mid - b)).astype(bf16)
        qe = (q * jnp.exp(b)).astype(bf16)
        kd = (k * jnp.exp(b_last - b)).astype(bf16)
        decay = jnp.exp(lax.dot_general(g1, ones, _TN, preferred_element_type=f32)
                        + lax.dot_general(g2, ones, _TN, preferred_element_type=f32)
                        + lax.dot_general(g3, ones, _TN, preferred_element_type=f32))
        state_b = state.astype(bf16)
        head_rows = lambda x: jnp.concatenate([jnp.where(lane_head == h, x, zero) for h in range(GLA_HEADS)], axis=0)
        a_all = lax.dot_general(head_rows(qs), ks, _NT, preferred_element_type=f32)
        o_state = jnp.dot(head_rows(qe), state_b, preferred_element_type=f32)
        v = v_ref[t, :].astype(bf16)
        upd_all = lax.dot_general(kd, v, _TN, preferred_element_type=f32)
        upd = []
        for h in range(GLA_HEADS):
            trow = slice(h * chunk, (h + 1) * chunk)
            rows = slice(h * GLA_HEAD_K, (h + 1) * GLA_HEAD_K)
            cols = slice(h * GLA_HEAD_V, (h + 1) * GLA_HEAD_V)
            a = jnp.where(causal, a_all[trow, :], 0.0).astype(bf16)
            o_ref[t, cols] = o_state[trow, :] + jnp.dot(a, v[:, cols], preferred_element_type=f32)
            upd.append(upd_all[rows, cols])
        state = state * decay + jnp.concatenate(upd, axis=0)
    s_scr[...] = state

    @pl.when(si == n_steps - 1)
    def _():
        sout_ref[0] = state.reshape(GLA_HEADS, GLA_HEAD_K, GLA_HEAD_V)


def _gla(gq, gk, gv, la, s0, *, bsz, length, chunk, cps):
    n_steps = length // (chunk * cps)
    rows = chunk * cps
    row = lambda w: pl.BlockSpec((rows, w), lambda b, s: (b * n_steps + s, 0))
    st = pl.BlockSpec((1, GLA_HEADS, GLA_HEAD_K, GLA_HEAD_V), lambda b, s: (b, 0, 0, 0))
    return pl.pallas_call(
        functools.partial(_gla_body, chunk=chunk, cps=cps, n_steps=n_steps),
        grid=(bsz, n_steps),
        in_specs=[row(GLA_KEY_WIDTH), row(GLA_KEY_WIDTH), row(GLA_VALUE_WIDTH), row(GLA_KEY_WIDTH), st],
        out_specs=[row(GLA_VALUE_WIDTH), st],
        out_shape=[jax.ShapeDtypeStruct((bsz * length, GLA_VALUE_WIDTH), f32),
                   jax.ShapeDtypeStruct((bsz, GLA_HEADS, GLA_HEAD_K, GLA_HEAD_V), f32)],
        scratch_shapes=[pltpu.VMEM((GLA_KEY_WIDTH, GLA_HEAD_V), f32)],
        compiler_params=_cparams(("arbitrary", "arbitrary")),
        name="gla",
    )(gq, gk, gv, la, s0)


_HG = 4


_HGW = _HG * FOX_HEAD_DIM
_FOX_RB, _FOX_CB = 256, 256


def _foxp_body(q_ref, k_ref, v_ref, cq_ref, ck_ref, o_ref, qm_scr, vone_scr, cqb_scr, m_scr, sh_scr, acc_scr, *, t):
    qi = pl.program_id(2)
    n_tiles = k_ref.shape[1]
    lane_head = lax.broadcasted_iota(i32, (1, _HGW), 1) // FOX_HEAD_DIM

    @pl.when(qi == 0)
    def _():
        row_head = lax.broadcasted_iota(i32, (_HGW, 1), 0) // FOX_HEAD_DIM
        for kt in range(n_tiles):
            vt = v_ref[0, kt].reshape(_HGW, t)
            for j in range(_HG):
                vone_scr[j, kt] = jnp.where(row_head == j, vt, jnp.ones((), bf16))

    q = q_ref[...]
    for j in range(_HG):
        qm_scr[j] = jnp.where(lane_head == j, q, jnp.zeros((), bf16))
        cqb_scr[j] = jnp.broadcast_to(cq_ref[0, 0, :, j:j + 1], (t, 128))
    m_scr[...] = jnp.full(m_scr.shape, -jnp.inf, f32)
    acc_scr[...] = jnp.zeros(acc_scr.shape, f32)

    def logits(ki, j, rb, cb, diagonal):
        r0, c0 = rb * _FOX_RB, cb * _FOX_CB
        if diagonal and c0 > r0 + _FOX_RB - 1:
            return None
        kt = k_ref[0, ki, :, :, c0:c0 + _FOX_CB].reshape(_HGW, _FOX_CB)
        u = jnp.dot(qm_scr[j, r0:r0 + _FOX_RB, :], kt, preferred_element_type=f32) - ck_ref[0, 0, ki, j:j + 1, c0:c0 + _FOX_CB]
        if diagonal and c0 + _FOX_CB - 1 > r0:
            qpos = r0 + lax.broadcasted_iota(i32, (_FOX_RB, _FOX_CB), 0)
            kpos = c0 + lax.broadcasted_iota(i32, (_FOX_RB, _FOX_CB), 1)
            u = jnp.where(kpos <= qpos, u, -jnp.inf)
        return u

    def max_pass(ki, diagonal):
        for j in range(_HG):
            for cb in range(t // _FOX_CB):
                for rb in range(t // _FOX_RB):
                    u = logits(ki, j, rb, cb, diagonal)
                    if u is not None:
                        rows = slice(rb * _FOX_RB, (rb + 1) * _FOX_RB)
                        m = m_scr[j, rows, :]
                        for c in range(_FOX_CB // 128):
                            m = jnp.maximum(m, u[:, c * 128:(c + 1) * 128])
                        m_scr[j, rows, :] = m

    def sum_pass(ki, diagonal):
        for j in range(_HG):
            for cb in range(t // _FOX_CB):
                vt = vone_scr[j, ki, :, cb * _FOX_CB:(cb + 1) * _FOX_CB]
                for rb in range(t // _FOX_RB):
                    u = logits(ki, j, rb, cb, diagonal)
                    if u is not None:
                        rows = slice(rb * _FOX_RB, (rb + 1) * _FOX_RB)
                        p = jnp.exp(u - jnp.concatenate([sh_scr[j, rows, :]] * (_FOX_CB // 128), axis=-1))
                        acc_scr[j, rows, :] += lax.dot_general(p.astype(bf16), vt, _NT, preferred_element_type=f32)

    lax.fori_loop(0, qi, lambda ki, c: (max_pass(ki, False), c)[1], 0)
    for j in range(_HG):
        for rb in range(t // _FOX_RB):
            rows = slice(rb * _FOX_RB, (rb + 1) * _FOX_RB)
            m = m_scr[j, rows, :]
            for cb in range(t // _FOX_CB):
                u = logits(qi, j, rb, cb, True)
                if u is not None:
                    for c in range(_FOX_CB // 128):
                        m = jnp.maximum(m, u[:, c * 128:(c + 1) * 128])
            cq = cqb_scr[j, rows, :]
            m_full = jnp.max(m, axis=-1, keepdims=True) + cq
            sh_scr[j, rows, :] = m_full - cq
    lax.fori_loop(0, qi, lambda ki, c: (sum_pass(ki, False), c)[1], 0)
    sum_pass(qi, True)

    out = jnp.zeros((t, _HGW), f32)
    for j in range(_HG):
        acc = acc_scr[j]
        other = acc[:, 128:256] if j < _HG // 2 else acc[:, 0:128]
        inv = 1.0 / other
        out = jnp.where(lane_head == j, acc * jnp.concatenate([inv, inv], axis=-1), out)
    o_ref[...] = out


def _fox_prompt(fq, kt, vt, c_rows, c_cols, *, bsz, length, t=512):
    n = length // t
    kv = pl.BlockSpec((1, n, _HG, FOX_HEAD_DIM, t), lambda b, g, qi: (b, 0, g, 0, 0))
    return pl.pallas_call(
        functools.partial(_foxp_body, t=t),
        grid=(bsz, FOX_HEADS // _HG, n),
        in_specs=[pl.BlockSpec((t, _HGW), lambda b, g, qi: (b * n + qi, g)), kv, kv,
                  pl.BlockSpec((1, 1, t, _HG), lambda b, g, qi: (b, g, qi, 0)),
                  pl.BlockSpec((1, 1, n, _HG, t), lambda b, g, qi: (b, g, 0, 0, 0))],
        out_specs=pl.BlockSpec((t, _HGW), lambda b, g, qi: (b * n + qi, g)),
        out_shape=jax.ShapeDtypeStruct((bsz * length, FOX_WIDTH), f32),
        scratch_shapes=[pltpu.VMEM((_HG, t, _HGW), bf16), pltpu.VMEM((_HG, n, _HGW, t), bf16),
                        pltpu.VMEM((_HG, t, 128), f32), pltpu.VMEM((_HG, t, 128), f32),
                        pltpu.VMEM((_HG, t, 128), f32), pltpu.VMEM((_HG, t, _HGW), f32)],
        compiler_params=_cparams(("arbitrary",) * 3),
        name="fox_prompt",
    )(fq, kt, vt, c_rows, c_cols)


def _foxs_body(kc_ref, vc_ref, kn_ref, vn_ref, qbd_ref, ckp_ref, ckn_ref, cq_ref, o_ref, s_scr,
               *, past, new, kchunk):
    nrow = FOX_HEADS * new
    qbd = qbd_ref[0]
    cq = cq_ref[0]

    def per_head_rows(c8):
        return jnp.concatenate([jnp.broadcast_to(c8[h:h + 1, :], (new, c8.shape[1])) for h in range(FOX_HEADS)], axis=0)

    n_chunks = past // kchunk
    m = jnp.full((nrow, 1), -jnp.inf, f32)
    for c in range(n_chunks):
        cols = slice(c * kchunk, (c + 1) * kchunk)
        kt = kc_ref[0, :, :, cols].reshape(FOX_WIDTH, kchunk).astype(bf16)
        s = jnp.dot(qbd, kt, preferred_element_type=f32) + cq - per_head_rows(ckp_ref[0, :, cols])
        s_scr[:, cols] = s
        m = jnp.maximum(m, jnp.max(s, axis=-1, keepdims=True))
    sn = lax.dot_general(qbd, kn_ref[0].astype(bf16), _NT, preferred_element_type=f32)
    sn = sn + cq - per_head_rows(ckn_ref[0])
    qry_i = lax.broadcasted_iota(i32, (nrow, new), 0) % new
    key_j = lax.broadcasted_iota(i32, (nrow, new), 1)
    sn = jnp.where(key_j <= qry_i, sn, -jnp.inf)
    m = jnp.maximum(m, jnp.max(sn, axis=-1, keepdims=True))
    pn = jnp.exp(sn - m)
    l = jnp.sum(pn, axis=-1, keepdims=True)
    for c in range(n_chunks):
        cols = slice(c * kchunk, (c + 1) * kchunk)
        p = jnp.exp(s_scr[:, cols] - m)
        s_scr[:, cols] = p
        l = l + jnp.sum(p, axis=-1, keepdims=True)
    inv = 1.0 / l
    acc = jnp.dot((pn * inv).astype(bf16), vn_ref[0].astype(bf16), preferred_element_type=f32)
    for c in range(n_chunks):
        cols = slice(c * kchunk, (c + 1) * kchunk)
        vt = vc_ref[0, :, :, cols].reshape(FOX_WIDTH, kchunk).astype(bf16)
        acc = acc + lax.dot_general((s_scr[:, cols] * inv).astype(bf16), vt, _NT, preferred_element_type=f32)
    for h in range(FOX_HEADS):
        o_ref[0, :, h * FOX_HEAD_DIM:(h + 1) * FOX_HEAD_DIM] = (
            acc[h * new:(h + 1) * new, h * FOX_HEAD_DIM:(h + 1) * FOX_HEAD_DIM])


def _fox_sample(kc_t, vc_t, kn, vn, qbd, ckp, ckn, cq, *, kchunk=512):
    bsz, _, _, past = kc_t.shape
    new, w = kn.shape[1], kn.shape[2]
    blk = lambda a: pl.BlockSpec((1,) + a.shape[1:], lambda b: (b,) + (0,) * (a.ndim - 1))
    return pl.pallas_call(
        functools.partial(_foxs_body, past=past, new=new, kchunk=kchunk),
        grid=(bsz,),
        in_specs=[blk(kc_t), blk(vc_t), blk(kn), blk(vn), blk(qbd), blk(ckp), blk(ckn), blk(cq)],
        out_specs=pl.BlockSpec((1, new, w), lambda b: (b, 0, 0)),
        out_shape=jax.ShapeDtypeStruct((bsz, new, w), f32),
        scratch_shapes=[pltpu.VMEM((FOX_HEADS * new, past), f32)],
        compiler_params=_cparams(("arbitrary",)),
        name="fox_sample",
    )(kc_t, vc_t, kn, vn, qbd, ckp, ckn, cq)


def _route(logits_t, b_router):
    ne, tm = logits_t.shape
    gsz = ne // N_GROUPS
    shape3 = (N_GROUPS, gsz, tm)
    scores = jax.nn.sigmoid(logits_t).reshape(shape3)
    choice = scores + b_router
    im = lax.broadcasted_iota(i32, shape3, 1)
    ig = lax.broadcasted_iota(i32, (N_GROUPS, 1, tm), 0)
    ie = lax.broadcasted_iota(i32, shape3, 0) * gsz + im
    neg = -jnp.inf
    m1 = jnp.max(choice, axis=1, keepdims=True)
    first = jnp.min(jnp.where(choice == m1, im, gsz), axis=1, keepdims=True)
    m2 = jnp.max(jnp.where(im == first, neg, choice), axis=1, keepdims=True)
    cur = m1 + m2
    gsel = jnp.zeros(cur.shape, i32)
    for _ in range(TOPK_GROUPS):
        mx = jnp.max(cur, axis=0, keepdims=True)
        hit = ig == jnp.min(jnp.where(cur == mx, ig, N_GROUPS), axis=0, keepdims=True)
        gsel = jnp.where(hit, 1, gsel)
        cur = jnp.where(hit, neg, cur)
    cur = jnp.where(jnp.broadcast_to(gsel, shape3) > 0, choice, neg)
    ids, ws = [], []
    for _ in range(TOP_K):
        mx = jnp.max(jnp.max(cur, axis=1, keepdims=True), axis=0, keepdims=True)
        cand = jnp.where(cur == mx, ie, ne)
        fi = jnp.min(jnp.min(cand, axis=1, keepdims=True), axis=0, keepdims=True)
        hit = ie == fi
        w = jnp.where(hit, scores, 0.0)
        ids.append(fi.reshape(1, tm))
        ws.append(jnp.sum(jnp.sum(w, axis=1, keepdims=True), axis=0, keepdims=True).reshape(1, tm))
        cur = jnp.where(hit, neg, cur)
    tot = ws[0]
    for w in ws[1:]:
        tot = tot + w
    scale = ROUTED_SCALE / (tot + 1e-20)
    return ids, [w * scale for w in ws]


def _outproj_body(og_ref, gg_ref, of_ref, x_ref, gt1_ref, sh2_ref, sc2_ref, ggla_ref, gfox_ref, gffn_ref,
                  wout_ref, wrh_ref, wrl_ref, br_ref, x1_ref, hp_ref, wts_ref, lpos_ref, lst_ref, seg_ref, *, nb, rpb):
    tm = nb * rpb
    og = og_ref[...]
    parts = []
    for h in range(GLA_HEADS):
        seg = og[:, h * GLA_HEAD_V:(h + 1) * GLA_HEAD_V]
        parts.append(seg * lax.rsqrt(jnp.mean(seg * seg, axis=-1, keepdims=True) + NORM_EPS))
    gla_n = jnp.concatenate(parts, axis=-1) * ggla_ref[...] * _silu(gg_ref[...])
    of = of_ref[...]
    fox_n = of * lax.rsqrt(jnp.mean(of * of, axis=-1, keepdims=True) + NORM_EPS) * gfox_ref[...]
    merged = jnp.concatenate([gla_n, fox_n], axis=-1).astype(bf16)
    mix = jnp.dot(merged, wout_ref[...], preferred_element_type=f32)
    x1 = (x_ref[...].reshape(nb, rpb, D_MODEL) + gt1_ref[...] * mix.reshape(nb, rpb, D_MODEL))
    x1_ref[...] = x1.reshape(tm, D_MODEL)
    x1f = x1.reshape(tm, D_MODEL)
    y = x1f * lax.rsqrt(jnp.mean(x1f * x1f, axis=-1, keepdims=True) + NORM_EPS) * gffn_ref[...]
    h2 = (y.reshape(nb, rpb, D_MODEL) * (1.0 + sc2_ref[...]) + sh2_ref[...]).reshape(tm, D_MODEL)
    h_hi = h2.astype(bf16)
    hp_ref[...] = h_hi
    h_lo = (h2 - h_hi.astype(f32)).astype(bf16)
    logits_t = (lax.dot_general(wrh_ref[...], h_hi, _NT, preferred_element_type=f32)
                + lax.dot_general(wrh_ref[...], h_lo, _NT, preferred_element_type=f32)
                + lax.dot_general(wrl_ref[...], h_hi, _NT, preferred_element_type=f32))
    ids, ws = _route(logits_t, br_ref[...])
    for j in range(TOP_K):
        wts_ref[j:j + 1, :] = ws[j]

    ie = lax.broadcasted_iota(i32, (N_EXPERTS, tm), 0)
    multihot = jnp.zeros((N_EXPERTS, tm), f32)
    for j in range(TOP_K):
        multihot = multihot + (ie == ids[j]).astype(f32)
    seg = jnp.maximum(jnp.ceil(jnp.sum(multihot, axis=1, keepdims=True) * (1.0 / SEG)), 1.0)
    r = lax.broadcasted_iota(i32, (N_EXPERTS, N_EXPERTS), 0)
    c = lax.broadcasted_iota(i32, (N_EXPERTS, N_EXPERTS), 1)
    lstart = jnp.dot((c < r).astype(bf16), jnp.broadcast_to(seg, (N_EXPERTS, 128)).astype(bf16),
                     preferred_element_type=f32)[:, :1]
    rr = lax.broadcasted_iota(i32, (tm, tm), 0)
    cc = lax.broadcasted_iota(i32, (tm, tm), 1)
    earlier = jnp.dot(multihot.astype(bf16), (rr < cc).astype(bf16), preferred_element_type=f32)
    pos = lstart * float(SEG) + earlier
    for j in range(TOP_K):
        lpos_ref[j:j + 1, :] = jnp.sum(jnp.where(ie == ids[j], pos, 0.0), axis=0, keepdims=True).astype(i32)
    lst_ref[0] = lstart.astype(i32)
    seg_ref[0] = seg.astype(i32)


def _outproj(og, gg, of, x2, m3, g_gla4, g_fox, g_ffn, w_out, w_rt_hi, w_rt_lo, b_r, *, nb, rpb, tiles_per_batch):
    t, d = x2.shape
    tm = nb * rpb
    assert tm == GROUP
    row = lambda w: pl.BlockSpec((tm, w), lambda i: (i, 0))
    col = pl.BlockSpec((TOP_K, tm), lambda i: (0, i))
    table = pl.BlockSpec((1, N_EXPERTS, 1), lambda i: (i, 0, 0))
    mod = lambda c: pl.BlockSpec((nb, 1, d), lambda i, c=c: (i // tiles_per_batch, 0, c))
    full = lambda a: pl.BlockSpec(a.shape, lambda i: (0,) * a.ndim)
    return pl.pallas_call(
        functools.partial(_outproj_body, nb=nb, rpb=rpb),
        grid=(t // tm,),
        in_specs=[row(GLA_VALUE_WIDTH), row(GLA_VALUE_WIDTH), row(FOX_WIDTH), row(d), mod(2), mod(3), mod(4),
                  full(g_gla4), full(g_fox), full(g_ffn), full(w_out), full(w_rt_hi), full(w_rt_lo), full(b_r)],
        out_specs=[row(d), row(d), col, col, table, table],
        out_shape=[jax.ShapeDtypeStruct((t, d), f32), jax.ShapeDtypeStruct((t, d), bf16),
                   jax.ShapeDtypeStruct((TOP_K, t), f32), jax.ShapeDtypeStruct((TOP_K, t), i32),
                   jax.ShapeDtypeStruct((t // tm, N_EXPERTS, 1), i32), jax.ShapeDtypeStruct((t // tm, N_EXPERTS, 1), i32)],
        compiler_params=_cparams(("arbitrary",)),
        name="outproj",
    )(og, gg, of, x2, m3, m3, m3, g_gla4, g_fox, g_ffn, w_out, w_rt_hi, w_rt_lo, b_r)


def _group_rows(lst_ref, seg_ref, g):
    k = g * N_EXPERTS + N_EXPERTS - 1
    return (lst_ref[k] + seg_ref[k]) * SEG


def _start_segment_copies(lst_ref, seg_ref, dst_ref, g, make_copy, experts=range(N_EXPERTS)):
    for e in experts:
        k = g * N_EXPERTS + e
        make_copy(pl.multiple_of(lst_ref[k] * SEG, SEG), pl.multiple_of(dst_ref[k] * SEG, SEG),
                  seg_ref[k] * SEG).start()


_CHUNKS_PER_TRIP = 2
_MIN_TRIPS = TOP_K * GROUP // (_CHUNKS_PER_TRIP * PCHUNK)
_EXPERTS_PER_TRIP = N_EXPERTS // _MIN_TRIPS
_SLOTS = 3


def _staged_trips(n_rows, pair, issue):
    for c in range(_MIN_TRIPS):
        pair(c)
        if issue is not None:
            issue(range(c * _EXPERTS_PER_TRIP, (c + 1) * _EXPERTS_PER_TRIP))
    lax.fori_loop(_MIN_TRIPS, _chunk_trips(n_rows), lambda c, carry: (pair(c), carry)[1], 0)


def _placement_rows(lpos, vals, r0):
    rid = lax.broadcasted_iota(i32, (PCHUNK, GROUP), 0).astype(f32).astype(bf16)
    rel = (lpos - r0).astype(f32).astype(bf16)
    p = jnp.zeros((PCHUNK, GROUP), bf16)
    for j in range(TOP_K):
        v = jnp.ones((), bf16) if vals is None else vals[j:j + 1, :]
        p = jnp.where(rel[j:j + 1, :] == rid, v, p)
    return p


def _chunk_trips(n_rows):
    return (n_rows + _CHUNKS_PER_TRIP * PCHUNK - 1) // (_CHUNKS_PER_TRIP * PCHUNK)


def _dispatch_body(lst_ref, seg_ref, dst_ref, tail0_ref, tailn_ref, misc_ref, lpos_ref, hp_ref, hs_ref, xs_hbm,
                   stage, zeros, sem, zsem, *, ngp, ng, n_blocks):
    s = pl.program_id(0)
    slot = s % _SLOTS
    prev = (s + _SLOTS - 1) % _SLOTS
    h = jnp.where(s < ngp, hp_ref[...], hs_ref[...])
    lpos = lpos_ref[...]

    def pair(c):
        r0 = pl.multiple_of(c * _CHUNKS_PER_TRIP * PCHUNK, _CHUNKS_PER_TRIP * PCHUNK)
        p = jnp.concatenate([_placement_rows(lpos, None, r0 + k * PCHUNK) for k in range(_CHUNKS_PER_TRIP)], axis=0)
        stage[slot, pl.ds(r0, _CHUNKS_PER_TRIP * PCHUNK), :] = jnp.dot(p, h, preferred_element_type=f32).astype(bf16)

    def issue_previous(experts):
        _start_segment_copies(lst_ref, seg_ref, dst_ref, s - 1, lambda a, d, n: pltpu.make_async_copy(
            stage.at[prev, pl.ds(a, n), :], xs_hbm.at[pl.ds(d, n), :], sem.at[prev]), experts)

    def wait_group(gg, sl):
        n = _group_rows(lst_ref, seg_ref, gg)
        pltpu.make_async_copy(stage.at[sl, pl.ds(0, n), :], xs_hbm.at[pl.ds(0, n), :], sem.at[sl]).wait()

    @pl.when(s >= _SLOTS)
    def _():
        wait_group(s - _SLOTS, slot)

    n_zero = misc_ref[1] * SEG + (n_blocks - misc_ref[0]) * EXPERT_ROWS

    @pl.when(s == 0)
    def _():
        zeros[...] = jnp.zeros(zeros.shape, bf16)

        def per_expert(e, carry):
            n = tailn_ref[e] * SEG

            @pl.when(n > 0)
            def _():
                pltpu.make_async_copy(zeros.at[pl.ds(0, n), :],
                                      xs_hbm.at[pl.ds(pl.multiple_of(tail0_ref[e] * SEG, SEG), n), :], zsem).start()

            return carry

        lax.fori_loop(0, N_EXPERTS, per_expert, 0)
        n_used = misc_ref[0]

        def per_block(b, carry):
            pltpu.make_async_copy(zeros, xs_hbm.at[pl.ds(pl.multiple_of(b * EXPERT_ROWS, EXPERT_ROWS), EXPERT_ROWS), :],
                                  zsem).start()
            return carry

        lax.fori_loop(n_used, n_blocks, per_block, 0)
        _staged_trips(_group_rows(lst_ref, seg_ref, s), pair, None)

    @pl.when(jnp.logical_and(s > 0, s < ng))
    def _():
        _staged_trips(_group_rows(lst_ref, seg_ref, s), pair, issue_previous)

    @pl.when(s == ng)
    def _():
        issue_previous(range(N_EXPERTS))
        for back in range(_SLOTS - 1, 0, -1):
            wait_group(s - back, (s + _SLOTS - back) % _SLOTS)

        @pl.when(n_zero > 0)
        def _():
            pltpu.make_async_copy(xs_hbm.at[pl.ds(0, n_zero), :], xs_hbm.at[pl.ds(0, n_zero), :], zsem).wait()


def _dispatch(tables, tail0, tailn, misc, lpos, hp, hs, *, n_blocks):
    ngp, ngs = hp.shape[0] // GROUP, hs.shape[0] // GROUP
    ng = ngp + ngs
    d = hp.shape[1]
    return pl.pallas_call(
        functools.partial(_dispatch_body, ngp=ngp, ng=ng, n_blocks=n_blocks),
        grid_spec=pltpu.PrefetchScalarGridSpec(
            num_scalar_prefetch=6,
            grid=(ng + 1,),
            in_specs=[pl.BlockSpec((TOP_K, GROUP), lambda g, *_: (0, jnp.minimum(g, ng - 1))),
                      pl.BlockSpec((GROUP, d), lambda g, *_: (jnp.minimum(g, ngp - 1), 0)),
                      pl.BlockSpec((GROUP, d), lambda g, *_: (jnp.clip(g - ngp, 0, ngs - 1), 0))],
            out_specs=pl.BlockSpec(memory_space=pl.ANY),
            scratch_shapes=[pltpu.VMEM((_SLOTS, STAGE_ROWS, d), bf16), pltpu.VMEM((EXPERT_ROWS, d), bf16),
                            pltpu.SemaphoreType.DMA((_SLOTS,)), pltpu.SemaphoreType.DMA]),
        out_shape=jax.ShapeDtypeStruct((n_blocks * EXPERT_ROWS, d), bf16),
        compiler_params=_cparams(("arbitrary",)),
        name="dispatch",
    )(*tables, tail0, tailn, misc, lpos, hp, hs)


def _experts_body(be_ref, nu_ref, xs_ref, wg_ref, wu_ref, wd_ref, y_ref, wg_s, wu_s, wd_s):
    i = pl.program_id(0)
    changed = jnp.logical_or(i == 0, be_ref[i] != be_ref[jnp.maximum(i - 1, 0)])

    @pl.when(changed)
    def _():
        wg_s[...] = wg_ref[0].astype(bf16)
        wu_s[...] = wu_ref[0].astype(bf16)
        wd_s[...] = wd_ref[0].astype(bf16)

    @pl.when(i < nu_ref[0])
    def _():
        x = xs_ref[...]
        g = jnp.dot(x, wg_s[...], preferred_element_type=f32)
        u = jnp.dot(x, wu_s[...], preferred_element_type=f32)
        a = (_silu(g) * u).astype(bf16)
        y_ref[...] = jnp.dot(a, wd_s[...], preferred_element_type=f32).astype(bf16)

    @pl.when(i >= nu_ref[0])
    def _():
        y_ref[...] = jnp.zeros(y_ref.shape, bf16)


def _experts(blk_e, n_used, xs, w_gate, w_up, w_down, *, block_rows):
    n_rows, d = xs.shape
    n_blocks = n_rows // block_rows
    de = w_gate.shape[2]
    return pl.pallas_call(
        _experts_body,
        grid_spec=pltpu.PrefetchScalarGridSpec(
            num_scalar_prefetch=2,
            grid=(n_blocks,),
            in_specs=[pl.BlockSpec((block_rows, d), lambda i, be, nu: (jnp.minimum(i, nu[0] - 1), 0)),
                      pl.BlockSpec((1, d, de), lambda i, be, nu: (be[i], 0, 0)),
                      pl.BlockSpec((1, d, de), lambda i, be, nu: (be[i], 0, 0)),
                      pl.BlockSpec((1, de, d), lambda i, be, nu: (be[i], 0, 0))],
            out_specs=pl.BlockSpec((block_rows, d), lambda i, be, nu: (i, 0)),
            scratch_shapes=[pltpu.VMEM((d, de), bf16), pltpu.VMEM((d, de), bf16), pltpu.VMEM((de, d), bf16)]),
        out_shape=jax.ShapeDtypeStruct((n_rows, d), bf16),
        compiler_params=_cparams(("arbitrary",)),
        name="experts",
    )(blk_e, n_used, xs, w_gate, w_up, w_down)


def _combine_body(lst_ref, seg_ref, dst_ref, lpos_ref, wts_ref, h_ref, x1_ref, gt2_ref, wsg_ref, wsu_ref, wsd_ref,
                  gfin_ref, y_hbm, o_ref, ybuf, acc, sem, *, nb, rpb, g0, ng):
    i = pl.program_id(0)
    g = g0 + i
    n_slots = ybuf.shape[0]
    slot = i % n_slots
    ahead = n_slots - 1

    def start_gather(gg, sl, experts=range(N_EXPERTS)):
        _start_segment_copies(lst_ref, seg_ref, dst_ref, gg, lambda s, d, n: pltpu.make_async_copy(
            y_hbm.at[pl.ds(d, n), :], ybuf.at[sl, pl.ds(s, n), :], sem.at[sl]), experts)

    @pl.when(i == 0)
    def _():
        ybuf[...] = jnp.zeros(ybuf.shape, bf16)
        for k in range(min(ahead, ng)):
            start_gather(g + k, k)

    h = h_ref[...]
    gate = jnp.dot(h, wsg_ref[...], preferred_element_type=f32)
    up = jnp.dot(h, wsu_ref[...], preferred_element_type=f32)
    acc[...] = jnp.dot((_silu(gate) * up).astype(bf16), wsd_ref[...], preferred_element_type=f32)

    n_rows = _group_rows(lst_ref, seg_ref, g)
    pltpu.make_async_copy(y_hbm.at[pl.ds(0, n_rows), :], ybuf.at[slot, pl.ds(0, n_rows), :], sem.at[slot]).wait()

    lpos = lpos_ref[...]
    wts = wts_ref[...].astype(bf16)

    def pair(c):
        r0 = pl.multiple_of(c * _CHUNKS_PER_TRIP * PCHUNK, _CHUNKS_PER_TRIP * PCHUNK)
        wt = jnp.concatenate([_placement_rows(lpos, wts, r0 + k * PCHUNK) for k in range(_CHUNKS_PER_TRIP)], axis=0)
        acc[...] += lax.dot_general(wt, ybuf[slot, pl.ds(r0, _CHUNKS_PER_TRIP * PCHUNK), :], _TN,
                                    preferred_element_type=f32)

    @pl.when(i + ahead < ng)
    def _():
        _staged_trips(n_rows, pair, lambda experts: start_gather(g + ahead, (i + ahead) % n_slots, experts))

    @pl.when(i + ahead >= ng)
    def _():
        _staged_trips(n_rows, pair, None)

    out = (x1_ref[...].reshape(nb, rpb, D_MODEL) + gt2_ref[...] * acc[...].reshape(nb, rpb, D_MODEL))
    out = out.reshape(GROUP, D_MODEL)
    o_ref[...] = out * lax.rsqrt(jnp.mean(out * out, axis=-1, keepdims=True) + NORM_EPS) * gfin_ref[...]


def _combine(tables, lpos_rows, wts_rows, y, h2, x1, m3, w_sg, w_su, w_sd, g_fin, *, nb, rpb, tiles_per_batch, g0):
    t, d = x1.shape
    assert nb * rpb == GROUP
    ng = t // GROUP
    row = lambda w: pl.BlockSpec((GROUP, w), lambda i, *_: (i, 0))
    grow = lambda w: pl.BlockSpec((w, GROUP), lambda i, *_: (0, g0 + i))
    full = lambda a: pl.BlockSpec(a.shape, lambda i, *_: (0,) * a.ndim)
    return pl.pallas_call(
        functools.partial(_combine_body, nb=nb, rpb=rpb, g0=g0, ng=ng),
        grid_spec=pltpu.PrefetchScalarGridSpec(
            num_scalar_prefetch=3,
            grid=(ng,),
            in_specs=[grow(TOP_K), grow(TOP_K), row(d), row(d),
                      pl.BlockSpec((nb, 1, d), lambda i, *_: (i // tiles_per_batch, 0, 5)),
                      full(w_sg), full(w_su), full(w_sd), full(g_fin), pl.BlockSpec(memory_space=pl.ANY)],
            out_specs=row(d),
            scratch_shapes=[pltpu.VMEM((2, STAGE_ROWS, d), bf16), pltpu.VMEM((GROUP, d), f32),
                            pltpu.SemaphoreType.DMA((2,))]),
        out_shape=jax.ShapeDtypeStruct((t, d), f32),
        compiler_params=_cparams(("arbitrary",)),
        name="combine",
    )(*tables, lpos_rows, wts_rows, h2, x1, m3, w_sg, w_su, w_sd, g_fin, y)


_GLA_CHUNKS_PER_STEP = 8


def kernel(x_prompt, x_sample, state_gla, cache_fox_k, cache_fox_v, cache_fox_logf, c_prompt, c_sample, w_ada, b_ada, g_norm_mix, g_norm_ffn, w_in, w_gla_gk2, b_gla_gk, g_gla_out, b_fox_f, g_fox_out, w_out, w_router, b_router, w_exp_gate, w_exp_up, w_exp_down, w_sh_gate, w_sh_up, w_sh_down, g_final):
    assert w_ada.shape[0] == 1, "single-layer trunk"
    bp, lp, d = x_prompt.shape
    bs, ls, _ = x_sample.shape
    past = cache_fox_k.shape[2]
    tp, ts = bp * lp, bs * ls

    wi = w_in[0]
    o_glr = 2 * GLA_KEY_WIDTH + 2 * GLA_VALUE_WIDTH
    o_fq = o_glr + GLA_LOW_RANK
    o_ff = o_fq + 3 * FOX_WIDTH
    small = jnp.concatenate([wi[:, o_ff:o_ff + FOX_HEADS], wi[:, o_glr:o_fq],
                             jnp.zeros((d, 128 - FOX_HEADS - GLA_LOW_RANK), f32)], axis=1)
    w_cat = jnp.concatenate([wi[:, :o_glr], wi[:, o_fq:o_ff], small], axis=1).astype(bf16)
    w_t = jnp.concatenate([wi[:, o_fq + FOX_WIDTH:o_ff + FOX_HEADS], jnp.zeros((d, 16 - FOX_HEADS), f32)],
                          axis=1).T.astype(bf16)
    wgk_pad = jnp.zeros((128, GLA_KEY_WIDTH), f32).at[FOX_HEADS:FOX_HEADS + GLA_LOW_RANK].set(w_gla_gk2[0]).astype(bf16)
    bgk = b_gla_gk[0].reshape(1, GLA_KEY_WIDTH)
    bf_pad = jnp.zeros((1, 128), f32).at[0, :FOX_HEADS].set(b_fox_f[0])
    bf_col = jnp.zeros((16, 1), f32).at[:FOX_HEADS, 0].set(b_fox_f[0])
    g_mix = g_norm_mix[0].reshape(1, d)
    g_ffn = g_norm_ffn[0].reshape(1, d)
    g_gla4 = jnp.tile(g_gla_out[0], GLA_HEADS).reshape(1, GLA_VALUE_WIDTH)
    g_fox = g_fox_out[0].reshape(1, FOX_WIDTH)
    g_fin = g_final.reshape(1, d)
    w_out_b = w_out[0].astype(bf16)
    w_rt = w_router[0].T
    w_rt_hi = w_rt.astype(bf16)
    w_rt_lo = (w_rt - w_rt_hi.astype(f32)).astype(bf16)
    b_r = b_router[0].reshape(N_GROUPS, N_EXPERTS // N_GROUPS, 1)
    w_sg, w_su, w_sd = w_sh_gate[0].astype(bf16), w_sh_up[0].astype(bf16), w_sh_down[0].astype(bf16)

    m_all = _ada(jnp.concatenate([c_prompt, c_sample], axis=0), w_ada[0], b_ada[0])
    m3p = m_all[:bp].reshape(bp, 1, 6 * d)
    m3s = m_all[bp:].reshape(bs, 1, 6 * d)

    groups = []
    for (x, m3, bsz, length) in ((x_prompt, m3p, bp, lp), (x_sample, m3s, bs, ls)):
        if length >= GROUP:
            nb, rpb_in, tpb_in = 1, 1024, length // 1024
            rpb, tpb = GROUP, length // GROUP
        else:
            nb, rpb_in, tpb_in = GROUP // length, length, 1
            rpb, tpb = length, 1
        groups.append(dict(x2=x.reshape(bsz * length, d), m3=m3, bsz=bsz, length=length,
                           nb=nb, rpb_in=rpb_in, tpb_in=tpb_in, rpb=rpb, tpb=tpb))

    gp, gs = groups
    (gp["gq"], gp["gk"], gp["gv"], gp["gg"], gp["la"], gp["fq"], kt_p, vt_p, ktb_p, vtb_p, lft_p) = _inproj(
        gp["x2"], gp["m3"], g_mix, w_cat, wgk_pad, bgk, bf_pad, w_t, bf_col,
        nb=gp["nb"], rpb=gp["rpb_in"], tiles_per_batch=gp["tpb_in"], seq_minor=True)
    (gs["gq"], gs["gk"], gs["gv"], gs["gg"], gs["la"], gs["fq"], fk_s, fv_s, lf_s) = _inproj(
        gs["x2"], gs["m3"], g_mix, w_cat, wgk_pad, bgk, bf_pad, w_t, bf_col,
        nb=gs["nb"], rpb=gs["rpb_in"], tiles_per_batch=gs["tpb_in"], seq_minor=False)

    s0p = jnp.zeros((bp, GLA_HEADS, GLA_HEAD_K, GLA_HEAD_V), f32)
    gp["og"], sp_new = _gla(gp["gq"], gp["gk"], gp["gv"], gp["la"], s0p, bsz=bp, length=lp, chunk=GLA_CHUNK,
                            cps=_GLA_CHUNKS_PER_STEP)
    gs["og"], ss_new = _gla(gs["gq"], gs["gk"], gs["gv"], gs["la"], state_gla[0], bsz=bs, length=ls, chunk=ls, cps=1)

    c_p = _cumsum_lanes(lft_p)
    c_grp = c_p.reshape(bp, FOX_HEADS // _HG, _HG, lp)
    c_rows = c_grp.transpose(0, 1, 3, 2)
    c_cols = c_grp.reshape(bp, FOX_HEADS // _HG, 1, _HG, lp)
    tile_major = lambda a: a.reshape(bp, 1, FOX_HEADS, FOX_HEAD_DIM, lp)
    gp["of"] = _fox_prompt(gp["fq"], tile_major(ktb_p), tile_major(vtb_p), c_rows, c_cols, bsz=bp, length=lp, t=lp)

    lf_s = lf_s.reshape(bs, ls, FOX_HEADS)
    lpad = -(-ls // 128) * 128
    c_s = _cumsum_lanes(jnp.pad(lf_s.transpose(0, 2, 1), ((0, 0), (0, 0), (0, lpad - ls))))[:, :, :ls]
    ck_past = _cumsum_lanes(cache_fox_logf[0].astype(f32).transpose(0, 2, 1), suffix=True)
    fq_s = gs["fq"].reshape(bs, ls, FOX_HEADS, FOX_HEAD_DIM)
    eye_h = jnp.eye(FOX_HEADS, dtype=bf16)
    qbd = jnp.einsum("bihd,hg->bhigd", fq_s, eye_h).reshape(bs, FOX_HEADS * ls, FOX_WIDTH)
    of_s = _fox_sample(cache_fox_k[0].transpose(0, 2, 3, 1), cache_fox_v[0].transpose(0, 2, 3, 1),
                       fk_s.reshape(bs, ls, FOX_WIDTH), fv_s.reshape(bs, ls, FOX_WIDTH), qbd,
                       ck_past, c_s, c_s.reshape(bs, FOX_HEADS * ls, 1))
    gs["of"] = of_s.reshape(ts, FOX_WIDTH)

    for gr in groups:
        gr["x1"], gr["hp"], gr["wts"], gr["lpos"], gr["lst"], gr["seg"] = _outproj(
            gr["og"], gr["gg"], gr["of"], gr["x2"], gr["m3"], g_gla4, g_fox, g_ffn, w_out_b, w_rt_hi, w_rt_lo, b_r,
            nb=gr["nb"] if gr["length"] < GROUP else 1, rpb=gr["rpb"], tiles_per_batch=gr["tpb"])

    t_all = tp + ts
    n_groups = t_all // GROUP
    tiles_per_block = EXPERT_ROWS // SEG
    n_blocks = -(-(t_all * TOP_K + n_groups * N_EXPERTS * SEG) // EXPERT_ROWS) + N_EXPERTS
    lpos = jnp.concatenate([gp["lpos"], gs["lpos"]], axis=1)
    lst = jnp.concatenate([gp["lst"], gs["lst"]], axis=0)[:, :, 0]
    seg = jnp.concatenate([gp["seg"], gs["seg"]], axis=0)[:, :, 0]
    tot_e = jnp.sum(seg, axis=0)
    blocks_per_e = (tot_e + tiles_per_block - 1) // tiles_per_block
    blk_end = jnp.cumsum(blocks_per_e)
    dst = (blk_end - blocks_per_e)[None, :] * tiles_per_block + jnp.cumsum(seg, axis=0) - seg
    tables = tuple(a.reshape(n_groups * N_EXPERTS) for a in (lst, seg, dst))
    n_used = blk_end[-1:]
    blk = jnp.minimum(jnp.arange(n_blocks, dtype=i32), n_used[0] - 1)
    blk_e = jnp.minimum(jnp.sum((blk_end[None, :] <= blk[:, None]).astype(i32), axis=1), N_EXPERTS - 1)
    tail0 = (blk_end - blocks_per_e) * tiles_per_block + tot_e
    tailn = blocks_per_e * tiles_per_block - tot_e
    misc = jnp.stack([n_used[0], jnp.sum(tailn)]).astype(i32)
    xs = _dispatch(tables, tail0, tailn, misc, lpos, gp["hp"], gs["hp"], n_blocks=n_blocks)
    y = _experts(blk_e, n_used.astype(i32), xs, w_exp_gate[0], w_exp_up[0], w_exp_down[0], block_rows=EXPERT_ROWS)

    wts_all = jnp.concatenate([gp["wts"], gs["wts"]], axis=1)
    outs = []
    for gr, g0 in ((gp, 0), (gs, tp // GROUP)):
        outs.append(_combine(tables, lpos, wts_all, y, gr["hp"], gr["x1"], gr["m3"], w_sg, w_su, w_sd, g_fin,
                             nb=gr["nb"] if gr["length"] < GROUP else 1, rpb=gr["rpb"],
                             tiles_per_batch=gr["tpb"], g0=g0))
    y_prompt = outs[0].reshape(bp, lp, d)
    y_sample = outs[1].reshape(bs, ls, d)

    return (y_prompt, y_sample, sp_new[None], ss_new[None],
            kt_p.transpose(0, 3, 1, 2)[None], vt_p.transpose(0, 3, 1, 2)[None], lft_p.transpose(0, 2, 1)[None],
            fk_s.reshape(1, bs, ls, FOX_HEADS, FOX_HEAD_DIM), fv_s.reshape(1, bs, ls, FOX_HEADS, FOX_HEAD_DIM),
            lf_s[None])
```

```python
import functools

import jax
import jax.numpy as jnp
from jax import lax
from jax.experimental import pallas as pl
from jax.experimental.pallas import tpu as pltpu

f32, bf16, i32 = jnp.float32, jnp.bfloat16, jnp.int32

D_MODEL = 1024
GLA_HEADS, GLA_HEAD_K, GLA_HEAD_V = 4, 64, 128
GLA_KEY_WIDTH, GLA_VALUE_WIDTH, GLA_LOW_RANK = 256, 512, 16
GLA_GATE_NORMALIZER = 16.0
GLA_CHUNK = 64
FOX_HEADS, FOX_HEAD_DIM, FOX_WIDTH = 8, 64, 512
N_EXPERTS, N_GROUPS, TOPK_GROUPS, TOP_K = 64, 8, 4, 8
D_EXPERT = 256
ROUTED_SCALE = 2.5
NORM_EPS = 1e-6
EXPERT_ROWS = 1024
GROUP = 256
SEG = 16
PCHUNK = 256
STAGE_ROWS = -(-(TOP_K * GROUP + N_EXPERTS * SEG) // (2 * PCHUNK)) * (2 * PCHUNK)
TILES_PER_BLOCK = EXPERT_ROWS // SEG
VMEM_LIMIT = 56 * 1024 * 1024


def _cparams(sem):
    return pltpu.CompilerParams(dimension_semantics=sem, vmem_limit_bytes=VMEM_LIMIT)


def _log_sigmoid(x):
    return jnp.minimum(x, 0.0) - jnp.log1p(jnp.exp(-jnp.abs(x)))


def _silu(x):
    return x * jax.nn.sigmoid(x)


def _ada_body(c_ref, w_ref, b_ref, o_ref):
    a = _silu(c_ref[...]).astype(bf16)
    o_ref[...] = jnp.dot(a, w_ref[...].astype(bf16), preferred_element_type=f32) + b_ref[...]


def _ada(c_all, w_ada, b_ada):
    nb, d = c_all.shape
    n = w_ada.shape[1]
    tn = 768
    return pl.pallas_call(
        _ada_body,
        grid=(n // tn,),
        in_specs=[pl.BlockSpec((nb, d), lambda j: (0, 0)),
                  pl.BlockSpec((d, tn), lambda j: (0, j)),
                  pl.BlockSpec((1, tn), lambda j: (0, j))],
        out_specs=pl.BlockSpec((nb, tn), lambda j: (0, j)),
        out_shape=jax.ShapeDtypeStruct((nb, n), f32),
        compiler_params=_cparams(("arbitrary",)),
        name="ada",
    )(c_all, w_ada, b_ada.reshape(1, n))


_C_GQ, _C_GK, _C_GV, _C_GG, _C_FQ, _C_FK, _C_FV, _C_SM, _C_END = 0, 256, 512, 1024, 1536, 2048, 2560, 3072, 3200


_NT = (((1,), (1,)), ((), ()))
_TN = (((0,), (0,)), ((), ()))


def _inproj_body(x_ref, sh_ref, sc_ref, g_ref, w_ref, wgk_ref, bgk_ref, bf_ref, wt_ref, bfc_ref,
                 gq_ref, gk_ref, gv_ref, gg_ref, la_ref, fq_ref, k_ref, v_ref, *rest, nb, rpb, seq_minor):
    tm = nb * rpb
    x = x_ref[...]
    y = x * lax.rsqrt(jnp.mean(x * x, axis=-1, keepdims=True) + NORM_EPS) * g_ref[...]
    h = (y.reshape(nb, rpb, D_MODEL) * (1.0 + sc_ref[...]) + sh_ref[...]).reshape(tm, D_MODEL).astype(bf16)

    def proj(a, b):
        return jnp.dot(h, w_ref[:, a:b], preferred_element_type=f32)

    gq_ref[...] = proj(_C_GQ, _C_GK) * (GLA_HEAD_K ** -0.5)
    gk_ref[...] = proj(_C_GK, _C_GV)
    gv_ref[...] = proj(_C_GV, _C_GG)
    gg_ref[...] = proj(_C_GG, _C_FQ)
    fq_ref[...] = (proj(_C_FQ, _C_FK) * (FOX_HEAD_DIM ** -0.5)).astype(bf16)
    zs = proj(_C_SM, _C_END)
    ga = jnp.dot(zs.astype(bf16), wgk_ref[...], preferred_element_type=f32) + bgk_ref[...]
    la_ref[...] = _log_sigmoid(ga) * (1.0 / GLA_GATE_NORMALIZER)
    if seq_minor:
        kb_ref, vb_ref, lf_ref = rest
        shape4 = (FOX_HEADS, FOX_HEAD_DIM, tm)
        kt = lax.dot_general(wt_ref[0:FOX_WIDTH, :], h, _NT, preferred_element_type=f32)
        k_ref[0] = kt.reshape(shape4)
        kb_ref[0] = kt.astype(bf16).reshape(shape4)
        vt = lax.dot_general(wt_ref[FOX_WIDTH:2 * FOX_WIDTH, :], h, _NT, preferred_element_type=f32)
        v_ref[0] = vt.reshape(shape4)
        vb_ref[0] = vt.astype(bf16).reshape(shape4)
        zf = lax.dot_general(wt_ref[2 * FOX_WIDTH:, :], h, _NT, preferred_element_type=f32)
        lf_ref[0] = _log_sigmoid(zf + bfc_ref[...])[:FOX_HEADS, :]
    else:
        (lf_ref,) = rest
        k_ref[...] = proj(_C_FK, _C_FV)
        v_ref[...] = proj(_C_FV, _C_SM)
        lf_ref[...] = _log_sigmoid(zs + bf_ref[...])[:, :FOX_HEADS]


def _inproj(x2, m3, g_mix, w_cat, wgk_pad, bgk, bf_pad, w_t, bf_col, *, nb, rpb, tiles_per_batch, seq_minor):
    t, d = x2.shape
    tm = nb * rpb
    tpb = tiles_per_batch
    bsz = t // (tm * tpb) if seq_minor else None
    row = lambda w: pl.BlockSpec((tm, w), lambda i: (i, 0))
    mod = lambda c: pl.BlockSpec((nb, 1, d), lambda i, c=c: (i // tpb, 0, c))
    full = lambda a: pl.BlockSpec(a.shape, lambda i: (0,) * a.ndim)
    widths = [(GLA_KEY_WIDTH, f32), (GLA_KEY_WIDTH, f32), (GLA_VALUE_WIDTH, f32), (GLA_VALUE_WIDTH, f32),
              (GLA_KEY_WIDTH, f32), (FOX_WIDTH, bf16)]
    out_specs = [row(w) for w, _ in widths]
    out_shape = [jax.ShapeDtypeStruct((t, w), dt) for w, dt in widths]
    if seq_minor:
        length = tm * tpb
        hd = pl.BlockSpec((1, FOX_HEADS, FOX_HEAD_DIM, tm), lambda i: (i // tpb, 0, 0, i % tpb))
        out_specs += [hd, hd, hd, hd, pl.BlockSpec((1, FOX_HEADS, tm), lambda i: (i // tpb, 0, i % tpb))]
        out_shape += [jax.ShapeDtypeStruct((bsz, FOX_HEADS, FOX_HEAD_DIM, length), f32)] * 2
        out_shape += [jax.ShapeDtypeStruct((bsz, FOX_HEADS, FOX_HEAD_DIM, length), bf16)] * 2
        out_shape += [jax.ShapeDtypeStruct((bsz, FOX_HEADS, length), f32)]
    else:
        out_specs += [row(FOX_WIDTH), row(FOX_WIDTH), row(FOX_HEADS)]
        out_shape += [jax.ShapeDtypeStruct((t, w), f32) for w in (FOX_WIDTH, FOX_WIDTH, FOX_HEADS)]
    return pl.pallas_call(
        functools.partial(_inproj_body, nb=nb, rpb=rpb, seq_minor=seq_minor),
        grid=(t // tm,),
        in_specs=[row(d), mod(0), mod(1), full(g_mix), full(w_cat), full(wgk_pad), full(bgk), full(bf_pad),
                  full(w_t), full(bf_col)],
        out_specs=out_specs,
        out_shape=out_shape,
        compiler_params=_cparams(("arbitrary",)),
        name="inproj",
    )(x2, m3, m3, g_mix, w_cat, wgk_pad, bgk, bf_pad, w_t, bf_col)


def _cumsum_chunks(read, rows, n, *, chunk=128, suffix=False):
    r = lax.broadcasted_iota(i32, (chunk, chunk), 0)
    c = lax.broadcasted_iota(i32, (chunk, chunk), 1)
    tri = (r <= c).astype(bf16)
    local = []
    for j in range(n // chunk):
        x = read(slice(j * chunk, (j + 1) * chunk))
        x1 = x.astype(bf16)
        r1 = x - x1.astype(f32)
        x2 = r1.astype(bf16)
        x3 = (r1 - x2.astype(f32)).astype(bf16)
        local.append(jnp.dot(x1, tri, preferred_element_type=f32) + jnp.dot(x2, tri, preferred_element_type=f32)
                     + jnp.dot(x3, tri, preferred_element_type=f32))
    offsets = [jnp.zeros((rows, 1), f32)]
    for j in range(n // chunk):
        offsets.append(offsets[-1] + local[j][:, chunk - 1:chunk])
    total = offsets[-1] if suffix else 0.0
    return [local[j] + (offsets[j] - total) for j in range(n // chunk)]


def _cumsum_body(x_ref, o_ref, *, chunk, suffix):
    n = x_ref.shape[-1]
    out = _cumsum_chunks(lambda sl: x_ref[0, :, sl], x_ref.shape[1], n, chunk=chunk, suffix=suffix)
    for j in range(n // chunk):
        o_ref[0, :, j * chunk:(j + 1) * chunk] = out[j]


def _cumsum_lanes(x, *, suffix=False):
    b, h, n = x.shape
    return pl.pallas_call(
        functools.partial(_cumsum_body, chunk=128, suffix=suffix),
        grid=(b,),
        in_specs=[pl.BlockSpec((1, h, n), lambda i: (i, 0, 0))],
        out_specs=pl.BlockSpec((1, h, n), lambda i: (i, 0, 0)),
        out_shape=jax.ShapeDtypeStruct((b, h, n), f32),
        compiler_params=_cparams(("arbitrary",)),
        name="cumsum",
    )(x)


def _gla_body(q_ref, k_ref, v_ref, la_ref, s0_ref, o_ref, sout_ref, s_scr, *, chunk, cps, n_steps):
    si = pl.program_id(1)

    @pl.when(si == 0)
    def _():
        s_scr[...] = s0_ref[0].reshape(GLA_KEY_WIDTH, GLA_HEAD_V)

    r = lax.broadcasted_iota(i32, (chunk, chunk), 0)
    c = lax.broadcasted_iota(i32, (chunk, chunk), 1)
    causal = c <= r
    tril = causal.astype(bf16)
    ones = jnp.ones((chunk, GLA_HEAD_V), bf16)
    lane_head = lax.broadcasted_iota(i32, (1, GLA_KEY_WIDTH), 1) // GLA_HEAD_K
    zero = jnp.zeros((), bf16)
    mid = chunk // 2
    state = s_scr[...]
    for ch in range(cps):
        t = slice(ch * chunk, (ch + 1) * chunk)
        g = la_ref[t, :]
        g1 = g.astype(bf16)
        r1 = g - g1.astype(f32)
        g2 = r1.astype(bf16)
        g3 = (r1 - g2.astype(f32)).astype(bf16)
        b = (jnp.dot(tril, g1, preferred_element_type=f32) + jnp.dot(tril, g2, preferred_element_type=f32)
             + jnp.dot(tril, g3, preferred_element_type=f32))
        b_mid = b[mid:mid + 1, :]
        b_last = b[chunk - 1:chunk, :]
        q = q_ref[t, :]
        k = k_ref[t, :]
        qs = (q * jnp.exp(b - b_mid)).astype(bf16)
        ks = (k * jnp.exp(b_mid - b)).astype(bf16)
        qe = (q * jnp.exp(b)).astype(bf16)
        kd = (k * jnp.exp(b_last - b)).astype(bf16)
        decay = jnp.exp(lax.dot_general(g1, ones, _TN, preferred_element_type=f32)
                        + lax.dot_general(g2, ones, _TN, preferred_element_type=f32)
                        + lax.dot_general(g3, ones, _TN, preferred_element_type=f32))
        state_b = state.astype(bf16)
        head_rows = lambda x: jnp.concatenate([jnp.where(lane_head == h, x, zero) for h in range(GLA_HEADS)], axis=0)
        a_all = lax.dot_general(head_rows(qs), ks, _NT, preferred_element_type=f32)
        o_state = jnp.dot(head_rows(qe), state_b, preferred_element_type=f32)
        v = v_ref[t, :].astype(bf16)
        upd_all = lax.dot_general(kd, v, _TN, preferred_element_type=f32)
        upd = []
        for h in range(GLA_HEADS):
            trow = slice(h * chunk, (h + 1) * chunk)
            rows = slice(h * GLA_HEAD_K, (h + 1) * GLA_HEAD_K)
            cols = slice(h * GLA_HEAD_V, (h + 1) * GLA_HEAD_V)
            a = jnp.where(causal, a_all[trow, :], 0.0).astype(bf16)
            o_ref[t, cols] = o_state[trow, :] + jnp.dot(a, v[:, cols], preferred_element_type=f32)
            upd.append(upd_all[rows, cols])
        state = state * decay + jnp.concatenate(upd, axis=0)
    s_scr[...] = state

    @pl.when(si == n_steps - 1)
    def _():
        sout_ref[0] = state.reshape(GLA_HEADS, GLA_HEAD_K, GLA_HEAD_V)


def _gla(gq, gk, gv, la, s0, *, bsz, length, chunk, cps):
    n_steps = length // (chunk * cps)
    rows = chunk * cps
    row = lambda w: pl.BlockSpec((rows, w), lambda b, s: (b * n_steps + s, 0))
    st = pl.BlockSpec((1, GLA_HEADS, GLA_HEAD_K, GLA_HEAD_V), lambda b, s: (b, 0, 0, 0))
    return pl.pallas_call(
        functools.partial(_gla_body, chunk=chunk, cps=cps, n_steps=n_steps),
        grid=(bsz, n_steps),
        in_specs=[row(GLA_KEY_WIDTH), row(GLA_KEY_WIDTH), row(GLA_VALUE_WIDTH), row(GLA_KEY_WIDTH), st],
        out_specs=[row(GLA_VALUE_WIDTH), st],
        out_shape=[jax.ShapeDtypeStruct((bsz * length, GLA_VALUE_WIDTH), f32),
                   jax.ShapeDtypeStruct((bsz, GLA_HEADS, GLA_HEAD_K, GLA_HEAD_V), f32)],
        scratch_shapes=[pltpu.VMEM((GLA_KEY_WIDTH, GLA_HEAD_V), f32)],
        compiler_params=_cparams(("arbitrary", "arbitrary")),
        name="gla",
    )(gq, gk, gv, la, s0)


_HG = 4


_HGW = _HG * FOX_HEAD_DIM
_FOX_RB, _FOX_CB = 256, 256


def _foxp_body(q_ref, k_ref, v_ref, cq_ref, ck_ref, o_ref, qm_scr, vone_scr, cqb_scr, m_scr, sh_scr, acc_scr, *, t):
    qi = pl.program_id(2)
    n_tiles = k_ref.shape[1]
    lane_head = lax.broadcasted_iota(i32, (1, _HGW), 1) // FOX_HEAD_DIM

    @pl.when(qi == 0)
    def _():
        row_head = lax.broadcasted_iota(i32, (_HGW, 1), 0) // FOX_HEAD_DIM
        for kt in range(n_tiles):
            vt = v_ref[0, kt].reshape(_HGW, t)
            for j in range(_HG):
                vone_scr[j, kt] = jnp.where(row_head == j, vt, jnp.ones((), bf16))

    q = q_ref[...]
    for j in range(_HG):
        qm_scr[j] = jnp.where(lane_head == j, q, jnp.zeros((), bf16))
        cqb_scr[j] = jnp.broadcast_to(cq_ref[0, 0, :, j:j + 1], (t, 128))
    m_scr[...] = jnp.full(m_scr.shape, -jnp.inf, f32)
    acc_scr[...] = jnp.zeros(acc_scr.shape, f32)

    def logits(ki, j, rb, cb, diagonal):
        r0, c0 = rb * _FOX_RB, cb * _FOX_CB
        if diagonal and c0 > r0 + _FOX_RB - 1:
            return None
        kt = k_ref[0, ki, :, :, c0:c0 + _FOX_CB].reshape(_HGW, _FOX_CB)
        u = jnp.dot(qm_scr[j, r0:r0 + _FOX_RB, :], kt, preferred_element_type=f32) - ck_ref[0, 0, ki, j:j + 1, c0:c0 + _FOX_CB]
        if diagonal and c0 + _FOX_CB - 1 > r0:
            qpos = r0 + lax.broadcasted_iota(i32, (_FOX_RB, _FOX_CB), 0)
            kpos = c0 + lax.broadcasted_iota(i32, (_FOX_RB, _FOX_CB), 1)
            u = jnp.where(kpos <= qpos, u, -jnp.inf)
        return u

    def max_pass(ki, diagonal):
        for j in range(_HG):
            for cb in range(t // _FOX_CB):
                for rb in range(t // _FOX_RB):
                    u = logits(ki, j, rb, cb, diagonal)
                    if u is not None:
                        rows = slice(rb * _FOX_RB, (rb + 1) * _FOX_RB)
                        m = m_scr[j, rows, :]
                        for c in range(_FOX_CB // 128):
                            m = jnp.maximum(m, u[:, c * 128:(c + 1) * 128])
                        m_scr[j, rows, :] = m

    def sum_pass(ki, diagonal):
        for j in range(_HG):
            for cb in range(t // _FOX_CB):
                vt = vone_scr[j, ki, :, cb * _FOX_CB:(cb + 1) * _FOX_CB]
                for rb in range(t // _FOX_RB):
                    u = logits(ki, j, rb, cb, diagonal)
                    if u is not None:
                        rows = slice(rb * _FOX_RB, (rb + 1) * _FOX_RB)
                        p = jnp.exp(u - jnp.concatenate([sh_scr[j, rows, :]] * (_FOX_CB // 128), axis=-1))
                        acc_scr[j, rows, :] += lax.dot_general(p.astype(bf16), vt, _NT, preferred_element_type=f32)

    lax.fori_loop(0, qi, lambda ki, c: (max_pass(ki, False), c)[1], 0)
    for j in range(_HG):
        for rb in range(t // _FOX_RB):
            rows = slice(rb * _FOX_RB, (rb + 1) * _FOX_RB)
            m = m_scr[j, rows, :]
            for cb in range(t // _FOX_CB):
                u = logits(qi, j, rb, cb, True)
                if u is not None:
                    for c in range(_FOX_CB // 128):
                        m = jnp.maximum(m, u[:, c * 128:(c + 1) * 128])
            cq = cqb_scr[j, rows, :]
            m_full = jnp.max(m, axis=-1, keepdims=True) + cq
            sh_scr[j, rows, :] = m_full - cq
    lax.fori_loop(0, qi, lambda ki, c: (sum_pass(ki, False), c)[1], 0)
    sum_pass(qi, True)

    out = jnp.zeros((t, _HGW), f32)
    for j in range(_HG):
        acc = acc_scr[j]
        other = acc[:, 128:256] if j < _HG // 2 else acc[:, 0:128]
        inv = 1.0 / other
        out = jnp.where(lane_head == j, acc * jnp.concatenate([inv, inv], axis=-1), out)
    o_ref[...] = out


def _fox_prompt(fq, kt, vt, c_rows, c_cols, *, bsz, length, t=512):
    n = length // t
    kv = pl.BlockSpec((1, n, _HG, FOX_HEAD_DIM, t), lambda b, g, qi: (b, 0, g, 0, 0))
    return pl.pallas_call(
        functools.partial(_foxp_body, t=t),
        grid=(bsz, FOX_HEADS // _HG, n),
        in_specs=[pl.BlockSpec((t, _HGW), lambda b, g, qi: (b * n + qi, g)), kv, kv,
                  pl.BlockSpec((1, 1, t, _HG), lambda b, g, qi: (b, g, qi, 0)),
                  pl.BlockSpec((1, 1, n, _HG, t), lambda b, g, qi: (b, g, 0, 0, 0))],
        out_specs=pl.BlockSpec((t, _HGW), lambda b, g, qi: (b * n + qi, g)),
        out_shape=jax.ShapeDtypeStruct((bsz * length, FOX_WIDTH), f32),
        scratch_shapes=[pltpu.VMEM((_HG, t, _HGW), bf16), pltpu.VMEM((_HG, n, _HGW, t), bf16),
                        pltpu.VMEM((_HG, t, 128), f32), pltpu.VMEM((_HG, t, 128), f32),
                        pltpu.VMEM((_HG, t, 128), f32), pltpu.VMEM((_HG, t, _HGW), f32)],
        compiler_params=_cparams(("arbitrary",) * 3),
        name="fox_prompt",
    )(fq, kt, vt, c_rows, c_cols)


def _foxs_body(kc_ref, vc_ref, kn_ref, vn_ref, q_ref, lfp_ref, ckn_ref, cq_ref, o_ref, s_scr,
               *, past, new, kchunk):
    nrow = FOX_HEADS * new
    q = q_ref[0]
    lane_head = lax.broadcasted_iota(i32, (1, FOX_WIDTH), 1) // FOX_HEAD_DIM
    qbd = jnp.concatenate([jnp.where(lane_head == h, q, jnp.zeros((), bf16)) for h in range(FOX_HEADS)], axis=0)
    cq = cq_ref[0]

    def per_head_rows(c8):
        return jnp.concatenate([jnp.broadcast_to(c8[h:h + 1, :], (new, c8.shape[1])) for h in range(FOX_HEADS)], axis=0)

    ck_chunk = 128
    ckp = _cumsum_chunks(lambda sl: lfp_ref[0, :, sl], FOX_HEADS, past, chunk=ck_chunk, suffix=True)
    n_chunks = past // kchunk
    m = jnp.full((nrow, 1), -jnp.inf, f32)
    for c in range(n_chunks):
        cols = slice(c * kchunk, (c + 1) * kchunk)
        kt = kc_ref[0, :, :, cols].reshape(FOX_WIDTH, kchunk).astype(bf16)
        ck = jnp.concatenate(ckp[c * (kchunk // ck_chunk):(c + 1) * (kchunk // ck_chunk)], axis=-1)
        s = jnp.dot(qbd, kt, preferred_element_type=f32) + cq - per_head_rows(ck)
        s_scr[:, cols] = s
        m = jnp.maximum(m, jnp.max(s, axis=-1, keepdims=True))
    sn = lax.dot_general(qbd, kn_ref[0].astype(bf16), _NT, preferred_element_type=f32)
    sn = sn + cq - per_head_rows(ckn_ref[0])
    qry_i = lax.broadcasted_iota(i32, (nrow, new), 0) % new
    key_j = lax.broadcasted_iota(i32, (nrow, new), 1)
    sn = jnp.where(key_j <= qry_i, sn, -jnp.inf)
    m = jnp.maximum(m, jnp.max(sn, axis=-1, keepdims=True))
    pn = jnp.exp(sn - m)
    l = jnp.sum(pn, axis=-1, keepdims=True)
    for c in range(n_chunks):
        cols = slice(c * kchunk, (c + 1) * kchunk)
        p = jnp.exp(s_scr[:, cols] - m)
        s_scr[:, cols] = p
        l = l + jnp.sum(p, axis=-1, keepdims=True)
    inv = 1.0 / l
    acc = jnp.dot((pn * inv).astype(bf16), vn_ref[0].astype(bf16), preferred_element_type=f32)
    for c in range(n_chunks):
        cols = slice(c * kchunk, (c + 1) * kchunk)
        vt = vc_ref[0, :, :, cols].reshape(FOX_WIDTH, kchunk).astype(bf16)
        acc = acc + lax.dot_general((s_scr[:, cols] * inv).astype(bf16), vt, _NT, preferred_element_type=f32)
    for h in range(FOX_HEADS):
        o_ref[0, :, h * FOX_HEAD_DIM:(h + 1) * FOX_HEAD_DIM] = (
            acc[h * new:(h + 1) * new, h * FOX_HEAD_DIM:(h + 1) * FOX_HEAD_DIM])


def _fox_sample(kc_t, vc_t, kn, vn, q, lfp, ckn, cq, *, kchunk=512):
    bsz, _, _, past = kc_t.shape
    new, w = kn.shape[1], kn.shape[2]
    blk = lambda a: pl.BlockSpec((1,) + a.shape[1:], lambda b: (b,) + (0,) * (a.ndim - 1))
    return pl.pallas_call(
        functools.partial(_foxs_body, past=past, new=new, kchunk=kchunk),
        grid=(bsz,),
        in_specs=[blk(kc_t), blk(vc_t), blk(kn), blk(vn), blk(q), blk(lfp), blk(ckn), blk(cq)],
        out_specs=pl.BlockSpec((1, new, w), lambda b: (b, 0, 0)),
        out_shape=jax.ShapeDtypeStruct((bsz, new, w), f32),
        scratch_shapes=[pltpu.VMEM((FOX_HEADS * new, past), f32)],
        compiler_params=_cparams(("arbitrary",)),
        name="fox_sample",
    )(kc_t, vc_t, kn, vn, q, lfp, ckn, cq)


def _route(logits_t, b_router):
    ne, tm = logits_t.shape
    gsz = ne // N_GROUPS
    shape3 = (N_GROUPS, gsz, tm)
    scores = jax.nn.sigmoid(logits_t).reshape(shape3)
    choice = scores + b_router
    im = lax.broadcasted_iota(i32, shape3, 1)
    ig = lax.broadcasted_iota(i32, (N_GROUPS, 1, tm), 0)
    ie = lax.broadcasted_iota(i32, shape3, 0) * gsz + im
    neg = -jnp.inf
    m1 = jnp.max(choice, axis=1, keepdims=True)
    first = jnp.min(jnp.where(choice == m1, im, gsz), axis=1, keepdims=True)
    m2 = jnp.max(jnp.where(im == first, neg, choice), axis=1, keepdims=True)
    cur = m1 + m2
    gsel = jnp.zeros(cur.shape, i32)
    for _ in range(TOPK_GROUPS):
        mx = jnp.max(cur, axis=0, keepdims=True)
        hit = ig == jnp.min(jnp.where(cur == mx, ig, N_GROUPS), axis=0, keepdims=True)
        gsel = jnp.where(hit, 1, gsel)
        cur = jnp.where(hit, neg, cur)
    cur = jnp.where(jnp.broadcast_to(gsel, shape3) > 0, choice, neg)
    ids, ws = [], []
    for _ in range(TOP_K):
        mx = jnp.max(jnp.max(cur, axis=1, keepdims=True), axis=0, keepdims=True)
        cand = jnp.where(cur == mx, ie, ne)
        fi = jnp.min(jnp.min(cand, axis=1, keepdims=True), axis=0, keepdims=True)
        hit = ie == fi
        w = jnp.where(hit, scores, 0.0)
        ids.append(fi.reshape(1, tm))
        ws.append(jnp.sum(jnp.sum(w, axis=1, keepdims=True), axis=0, keepdims=True).reshape(1, tm))
        cur = jnp.where(hit, neg, cur)
    tot = ws[0]
    for w in ws[1:]:
        tot = tot + w
    scale = ROUTED_SCALE / (tot + 1e-20)
    return ids, [w * scale for w in ws]


def _outproj_body(og_ref, gg_ref, of_ref, x_ref, gt1_ref, sh2_ref, sc2_ref, ggla_ref, gfox_ref, gffn_ref,
                  wout_ref, wrh_ref, wrl_ref, br_ref, x1_ref, hp_ref, wts_ref, lpos_ref, lst_ref, seg_ref, *, nb, rpb):
    tm = nb * rpb
    og = og_ref[...]
    parts = []
    for h in range(GLA_HEADS):
        seg = og[:, h * GLA_HEAD_V:(h + 1) * GLA_HEAD_V]
        parts.append(seg * lax.rsqrt(jnp.mean(seg * seg, axis=-1, keepdims=True) + NORM_EPS))
    gla_n = jnp.concatenate(parts, axis=-1) * ggla_ref[...] * _silu(gg_ref[...])
    of = of_ref[...]
    fox_n = of * lax.rsqrt(jnp.mean(of * of, axis=-1, keepdims=True) + NORM_EPS) * gfox_ref[...]
    merged = jnp.concatenate([gla_n, fox_n], axis=-1).astype(bf16)
    mix = jnp.dot(merged, wout_ref[...], preferred_element_type=f32)
    x1 = (x_ref[...].reshape(nb, rpb, D_MODEL) + gt1_ref[...] * mix.reshape(nb, rpb, D_MODEL))
    x1_ref[...] = x1.reshape(tm, D_MODEL)
    x1f = x1.reshape(tm, D_MODEL)
    y = x1f * lax.rsqrt(jnp.mean(x1f * x1f, axis=-1, keepdims=True) + NORM_EPS) * gffn_ref[...]
    h2 = (y.reshape(nb, rpb, D_MODEL) * (1.0 + sc2_ref[...]) + sh2_ref[...]).reshape(tm, D_MODEL)
    h_hi = h2.astype(bf16)
    hp_ref[...] = h_hi
    h_lo = (h2 - h_hi.astype(f32)).astype(bf16)
    logits_t = (lax.dot_general(wrh_ref[...], h_hi, _NT, preferred_element_type=f32)
                + lax.dot_general(wrh_ref[...], h_lo, _NT, preferred_element_type=f32)
                + lax.dot_general(wrl_ref[...], h_hi, _NT, preferred_element_type=f32))
    ids, ws = _route(logits_t, br_ref[...])
    for j in range(TOP_K):
        wts_ref[j:j + 1, :] = ws[j]

    ie = lax.broadcasted_iota(i32, (N_EXPERTS, tm), 0)
    multihot = jnp.zeros((N_EXPERTS, tm), f32)
    for j in range(TOP_K):
        multihot = multihot + (ie == ids[j]).astype(f32)
    seg = jnp.maximum(jnp.ceil(jnp.sum(multihot, axis=1, keepdims=True) * (1.0 / SEG)), 1.0)
    r = lax.broadcasted_iota(i32, (N_EXPERTS, N_EXPERTS), 0)
    c = lax.broadcasted_iota(i32, (N_EXPERTS, N_EXPERTS), 1)
    lstart = jnp.dot((c < r).astype(bf16), jnp.broadcast_to(seg, (N_EXPERTS, 128)).astype(bf16),
                     preferred_element_type=f32)[:, :1]
    rr = lax.broadcasted_iota(i32, (tm, tm), 0)
    cc = lax.broadcasted_iota(i32, (tm, tm), 1)
    earlier = jnp.dot(multihot.astype(bf16), (rr < cc).astype(bf16), preferred_element_type=f32)
    pos = lstart * float(SEG) + earlier
    for j in range(TOP_K):
        lpos_ref[j:j + 1, :] = jnp.sum(jnp.where(ie == ids[j], pos, 0.0), axis=0, keepdims=True).astype(i32)
    lst_ref[0] = lstart.astype(i32)
    seg_ref[0] = seg.astype(i32)


def _outproj(og, gg, of, x2, m3, g_gla4, g_fox, g_ffn, w_out, w_rt_hi, w_rt_lo, b_r, *, nb, rpb, tiles_per_batch):
    t, d = x2.shape
    tm = nb * rpb
    assert tm == GROUP
    row = lambda w: pl.BlockSpec((tm, w), lambda i: (i, 0))
    col = pl.BlockSpec((TOP_K, tm), lambda i: (0, i))
    table = pl.BlockSpec((1, N_EXPERTS, 1), lambda i: (i, 0, 0))
    mod = lambda c: pl.BlockSpec((nb, 1, d), lambda i, c=c: (i // tiles_per_batch, 0, c))
    full = lambda a: pl.BlockSpec(a.shape, lambda i: (0,) * a.ndim)
    return pl.pallas_call(
        functools.partial(_outproj_body, nb=nb, rpb=rpb),
        grid=(t // tm,),
        in_specs=[row(GLA_VALUE_WIDTH), row(GLA_VALUE_WIDTH), row(FOX_WIDTH), row(d), mod(2), mod(3), mod(4),
                  full(g_gla4), full(g_fox), full(g_ffn), full(w_out), full(w_rt_hi), full(w_rt_lo), full(b_r)],
        out_specs=[row(d), row(d), col, col, table, table],
        out_shape=[jax.ShapeDtypeStruct((t, d), f32), jax.ShapeDtypeStruct((t, d), bf16),
                   jax.ShapeDtypeStruct((TOP_K, t), f32), jax.ShapeDtypeStruct((TOP_K, t), i32),
                   jax.ShapeDtypeStruct((t // tm, N_EXPERTS, 1), i32), jax.ShapeDtypeStruct((t // tm, N_EXPERTS, 1), i32)],
        compiler_params=_cparams(("arbitrary",)),
        name="outproj",
    )(og, gg, of, x2, m3, m3, m3, g_gla4, g_fox, g_ffn, w_out, w_rt_hi, w_rt_lo, b_r)


def _group_rows(lst_ref, seg_ref, g):
    k = g * N_EXPERTS + N_EXPERTS - 1
    return (lst_ref[k] + seg_ref[k]) * SEG


def _start_segment_copies(lst_ref, seg_ref, dst_ref, g, make_copy, experts=range(N_EXPERTS)):
    for e in experts:
        k = g * N_EXPERTS + e
        make_copy(pl.multiple_of(lst_ref[k] * SEG, SEG), pl.multiple_of(dst_ref[k] * SEG, SEG),
                  seg_ref[k] * SEG).start()


_CHUNKS_PER_TRIP = 2
_MIN_TRIPS = TOP_K * GROUP // (_CHUNKS_PER_TRIP * PCHUNK)
_EXPERTS_PER_TRIP = N_EXPERTS // _MIN_TRIPS
_SLOTS = 3


def _staged_trips(n_rows, pair, issue):
    for c in range(_MIN_TRIPS):
        pair(c)
        if issue is not None:
            issue(range(c * _EXPERTS_PER_TRIP, (c + 1) * _EXPERTS_PER_TRIP))
    lax.fori_loop(_MIN_TRIPS, _chunk_trips(n_rows), lambda c, carry: (pair(c), carry)[1], 0)


def _placement_rows(lpos, vals, r0):
    rid = lax.broadcasted_iota(i32, (PCHUNK, GROUP), 0).astype(f32).astype(bf16)
    rel = (lpos - r0).astype(f32).astype(bf16)
    p = jnp.zeros((PCHUNK, GROUP), bf16)
    for j in range(TOP_K):
        v = jnp.ones((), bf16) if vals is None else vals[j:j + 1, :]
        p = jnp.where(rel[j:j + 1, :] == rid, v, p)
    return p


def _chunk_trips(n_rows):
    return (n_rows + _CHUNKS_PER_TRIP * PCHUNK - 1) // (_CHUNKS_PER_TRIP * PCHUNK)


def _dispatch_body(lst_ref, seg_ref, dst_ref, tail0_ref, tailn_ref, misc_ref, lpos_ref, hp_ref, hs_ref, xs_hbm,
                   stage, zeros, sem, zsem, *, ngp, ng, n_blocks):
    s = pl.program_id(0)
    slot = s % _SLOTS
    prev = (s + _SLOTS - 1) % _SLOTS
    h = jnp.where(s < ngp, hp_ref[...], hs_ref[...])
    lpos = lpos_ref[...]

    def pair(c):
        r0 = pl.multiple_of(c * _CHUNKS_PER_TRIP * PCHUNK, _CHUNKS_PER_TRIP * PCHUNK)
        p = jnp.concatenate([_placement_rows(lpos, None, r0 + k * PCHUNK) for k in range(_CHUNKS_PER_TRIP)], axis=0)
        stage[slot, pl.ds(r0, _CHUNKS_PER_TRIP * PCHUNK), :] = jnp.dot(p, h, preferred_element_type=f32).astype(bf16)

    def issue_previous(experts):
        _start_segment_copies(lst_ref, seg_ref, dst_ref, s - 1, lambda a, d, n: pltpu.make_async_copy(
            stage.at[prev, pl.ds(a, n), :], xs_hbm.at[pl.ds(d, n), :], sem.at[prev]), experts)

    def wait_group(gg, sl):
        n = _group_rows(lst_ref, seg_ref, gg)
        pltpu.make_async_copy(stage.at[sl, pl.ds(0, n), :], xs_hbm.at[pl.ds(0, n), :], sem.at[sl]).wait()

    @pl.when(s >= _SLOTS)
    def _():
        wait_group(s - _SLOTS, slot)

    n_zero = misc_ref[1] * SEG + (n_blocks - misc_ref[0]) * EXPERT_ROWS

    @pl.when(s == 0)
    def _():
        zeros[...] = jnp.zeros(zeros.shape, bf16)

        def per_expert(e, carry):
            n = tailn_ref[e] * SEG

            @pl.when(n > 0)
            def _():
                pltpu.make_async_copy(zeros.at[pl.ds(0, n), :],
                                      xs_hbm.at[pl.ds(pl.multiple_of(tail0_ref[e] * SEG, SEG), n), :], zsem).start()

            return carry

        lax.fori_loop(0, N_EXPERTS, per_expert, 0)
        n_used = misc_ref[0]

        def per_block(b, carry):
            pltpu.make_async_copy(zeros, xs_hbm.at[pl.ds(pl.multiple_of(b * EXPERT_ROWS, EXPERT_ROWS), EXPERT_ROWS), :],
                                  zsem).start()
            return carry

        lax.fori_loop(n_used, n_blocks, per_block, 0)
        _staged_trips(_group_rows(lst_ref, seg_ref, s), pair, None)

    @pl.when(jnp.logical_and(s > 0, s < ng))
    def _():
        _staged_trips(_group_rows(lst_ref, seg_ref, s), pair, issue_previous)

    @pl.when(s == ng)
    def _():
        issue_previous(range(N_EXPERTS))
        for back in range(_SLOTS - 1, 0, -1):
            wait_group(s - back, (s + _SLOTS - back) % _SLOTS)

        @pl.when(n_zero > 0)
        def _():
            pltpu.make_async_copy(xs_hbm.at[pl.ds(0, n_zero), :], xs_hbm.at[pl.ds(0, n_zero), :], zsem).wait()


def _dispatch(tables, tail0, tailn, misc, lpos, hp, hs, *, n_blocks):
    ngp, ngs = hp.shape[0] // GROUP, hs.shape[0] // GROUP
    ng = ngp + ngs
    d = hp.shape[1]
    return pl.pallas_call(
        functools.partial(_dispatch_body, ngp=ngp, ng=ng, n_blocks=n_blocks),
        grid_spec=pltpu.PrefetchScalarGridSpec(
            num_scalar_prefetch=6,
            grid=(ng + 1,),
            in_specs=[pl.BlockSpec((TOP_K, GROUP), lambda g, *_: (0, jnp.minimum(g, ng - 1))),
                      pl.BlockSpec((GROUP, d), lambda g, *_: (jnp.minimum(g, ngp - 1), 0)),
                      pl.BlockSpec((GROUP, d), lambda g, *_: (jnp.clip(g - ngp, 0, ngs - 1), 0))],
            out_specs=pl.BlockSpec(memory_space=pl.ANY),
            scratch_shapes=[pltpu.VMEM((_SLOTS, STAGE_ROWS, d), bf16), pltpu.VMEM((EXPERT_ROWS, d), bf16),
                            pltpu.SemaphoreType.DMA((_SLOTS,)), pltpu.SemaphoreType.DMA]),
        out_shape=jax.ShapeDtypeStruct((n_blocks * EXPERT_ROWS, d), bf16),
        compiler_params=_cparams(("arbitrary",)),
        name="dispatch",
    )(*tables, tail0, tailn, misc, lpos, hp, hs)


def _experts_body(be_ref, nu_ref, xs_ref, wg_ref, wu_ref, wd_ref, y_ref, wg_s, wu_s, wd_s):
    i = pl.program_id(0)
    changed = jnp.logical_or(i == 0, be_ref[i] != be_ref[jnp.maximum(i - 1, 0)])

    @pl.when(changed)
    def _():
        wg_s[...] = wg_ref[0].astype(bf16)
        wu_s[...] = wu_ref[0].astype(bf16)
        wd_s[...] = wd_ref[0].astype(bf16)

    @pl.when(i < nu_ref[0])
    def _():
        x = xs_ref[...]
        g = jnp.dot(x, wg_s[...], preferred_element_type=f32)
        u = jnp.dot(x, wu_s[...], preferred_element_type=f32)
        a = (_silu(g) * u).astype(bf16)
        y_ref[...] = jnp.dot(a, wd_s[...], preferred_element_type=f32).astype(bf16)


def _experts(blk_e, n_used, xs, w_gate, w_up, w_down, *, block_rows):
    n_rows, d = xs.shape
    n_blocks = n_rows // block_rows
    de = w_gate.shape[2]
    used = lambda i, be, nu: (jnp.minimum(i, nu[0] - 1), 0)
    return pl.pallas_call(
        _experts_body,
        grid_spec=pltpu.PrefetchScalarGridSpec(
            num_scalar_prefetch=2,
            grid=(n_blocks,),
            in_specs=[pl.BlockSpec((block_rows, d), used),
                      pl.BlockSpec((1, d, de), lambda i, be, nu: (be[i], 0, 0)),
                      pl.BlockSpec((1, d, de), lambda i, be, nu: (be[i], 0, 0)),
                      pl.BlockSpec((1, de, d), lambda i, be, nu: (be[i], 0, 0))],
            out_specs=pl.BlockSpec((block_rows, d), used),
            scratch_shapes=[pltpu.VMEM((d, de), bf16), pltpu.VMEM((d, de), bf16), pltpu.VMEM((de, d), bf16)]),
        out_shape=jax.ShapeDtypeStruct((n_rows, d), bf16),
        input_output_aliases={2: 0},
        compiler_params=_cparams(("arbitrary",)),
        name="experts",
    )(blk_e, n_used, xs, w_gate, w_up, w_down)


def _combine_body(lst_ref, seg_ref, dst_ref, lpos_ref, wts_ref, h_ref, x1_ref, gt2_ref, wsg_ref, wsu_ref, wsd_ref,
                  gfin_ref, y_hbm, o_ref, ybuf, acc, sem, *, nb, rpb, g0, ng):
    i = pl.program_id(0)
    g = g0 + i
    n_slots = ybuf.shape[0]
    slot = i % n_slots
    ahead = n_slots - 1

    def start_gather(gg, sl, experts=range(N_EXPERTS)):
        _start_segment_copies(lst_ref, seg_ref, dst_ref, gg, lambda s, d, n: pltpu.make_async_copy(
            y_hbm.at[pl.ds(d, n), :], ybuf.at[sl, pl.ds(s, n), :], sem.at[sl]), experts)

    @pl.when(i == 0)
    def _():
        ybuf[...] = jnp.zeros(ybuf.shape, bf16)
        for k in range(min(ahead, ng)):
            start_gather(g + k, k)

    h = h_ref[...]
    gate = jnp.dot(h, wsg_ref[...], preferred_element_type=f32)
    up = jnp.dot(h, wsu_ref[...], preferred_element_type=f32)
    acc[...] = jnp.dot((_silu(gate) * up).astype(bf16), wsd_ref[...], preferred_element_type=f32)

    n_rows = _group_rows(lst_ref, seg_ref, g)
    pltpu.make_async_copy(y_hbm.at[pl.ds(0, n_rows), :], ybuf.at[slot, pl.ds(0, n_rows), :], sem.at[slot]).wait()

    lpos = lpos_ref[...]
    wts = wts_ref[...].astype(bf16)

    def pair(c):
        r0 = pl.multiple_of(c * _CHUNKS_PER_TRIP * PCHUNK, _CHUNKS_PER_TRIP * PCHUNK)
        wt = jnp.concatenate([_placement_rows(lpos, wts, r0 + k * PCHUNK) for k in range(_CHUNKS_PER_TRIP)], axis=0)
        acc[...] += lax.dot_general(wt, ybuf[slot, pl.ds(r0, _CHUNKS_PER_TRIP * PCHUNK), :], _TN,
                                    preferred_element_type=f32)

    @pl.when(i + ahead < ng)
    def _():
        _staged_trips(n_rows, pair, lambda experts: start_gather(g + ahead, (i + ahead) % n_slots, experts))

    @pl.when(i + ahead >= ng)
    def _():
        _staged_trips(n_rows, pair, None)

    out = (x1_ref[...].reshape(nb, rpb, D_MODEL) + gt2_ref[...] * acc[...].reshape(nb, rpb, D_MODEL))
    out = out.reshape(GROUP, D_MODEL)
    o_ref[...] = out * lax.rsqrt(jnp.mean(out * out, axis=-1, keepdims=True) + NORM_EPS) * gfin_ref[...]


def _combine(tables, lpos_rows, wts_rows, y, h2, x1, m3, w_sg, w_su, w_sd, g_fin, *, nb, rpb, tiles_per_batch, g0):
    t, d = x1.shape
    assert nb * rpb == GROUP
    ng = t // GROUP
    row = lambda w: pl.BlockSpec((GROUP, w), lambda i, *_: (i, 0))
    grow = lambda w: pl.BlockSpec((w, GROUP), lambda i, *_: (0, g0 + i))
    full = lambda a: pl.BlockSpec(a.shape, lambda i, *_: (0,) * a.ndim)
    return pl.pallas_call(
        functools.partial(_combine_body, nb=nb, rpb=rpb, g0=g0, ng=ng),
        grid_spec=pltpu.PrefetchScalarGridSpec(
            num_scalar_prefetch=3,
            grid=(ng,),
            in_specs=[grow(TOP_K), grow(TOP_K), row(d), row(d),
                      pl.BlockSpec((nb, 1, d), lambda i, *_: (i // tiles_per_batch, 0, 5)),
                      full(w_sg), full(w_su), full(w_sd), full(g_fin), pl.BlockSpec(memory_space=pl.ANY)],
            out_specs=row(d),
            scratch_shapes=[pltpu.VMEM((2, STAGE_ROWS, d), bf16), pltpu.VMEM((GROUP, d), f32),
                            pltpu.SemaphoreType.DMA((2,))]),
        out_shape=jax.ShapeDtypeStruct((t, d), f32),
        compiler_params=_cparams(("arbitrary",)),
        name="combine",
    )(*tables, lpos_rows, wts_rows, h2, x1, m3, w_sg, w_su, w_sd, g_fin, y)


_GLA_CHUNKS_PER_STEP = 8


def kernel(x_prompt, x_sample, state_gla, cache_fox_k, cache_fox_v, cache_fox_logf, c_prompt, c_sample, w_ada, b_ada, g_norm_mix, g_norm_ffn, w_in, w_gla_gk2, b_gla_gk, g_gla_out, b_fox_f, g_fox_out, w_out, w_router, b_router, w_exp_gate, w_exp_up, w_exp_down, w_sh_gate, w_sh_up, w_sh_down, g_final):
    assert w_ada.shape[0] == 1, "single-layer trunk"
    bp, lp, d = x_prompt.shape
    bs, ls, _ = x_sample.shape
    past = cache_fox_k.shape[2]
    tp, ts = bp * lp, bs * ls

    wi = w_in[0]
    o_glr = 2 * GLA_KEY_WIDTH + 2 * GLA_VALUE_WIDTH
    o_fq = o_glr + GLA_LOW_RANK
    o_ff = o_fq + 3 * FOX_WIDTH
    small = jnp.concatenate([wi[:, o_ff:o_ff + FOX_HEADS], wi[:, o_glr:o_fq],
                             jnp.zeros((d, 128 - FOX_HEADS - GLA_LOW_RANK), f32)], axis=1)
    w_cat = jnp.concatenate([wi[:, :o_glr], wi[:, o_fq:o_ff], small], axis=1).astype(bf16)
    w_t = jnp.concatenate([wi[:, o_fq + FOX_WIDTH:o_ff + FOX_HEADS], jnp.zeros((d, 16 - FOX_HEADS), f32)],
                          axis=1).T.astype(bf16)
    wgk_pad = jnp.zeros((128, GLA_KEY_WIDTH), f32).at[FOX_HEADS:FOX_HEADS + GLA_LOW_RANK].set(w_gla_gk2[0]).astype(bf16)
    bgk = b_gla_gk[0].reshape(1, GLA_KEY_WIDTH)
    bf_pad = jnp.zeros((1, 128), f32).at[0, :FOX_HEADS].set(b_fox_f[0])
    bf_col = jnp.zeros((16, 1), f32).at[:FOX_HEADS, 0].set(b_fox_f[0])
    g_mix = g_norm_mix[0].reshape(1, d)
    g_ffn = g_norm_ffn[0].reshape(1, d)
    g_gla4 = jnp.tile(g_gla_out[0], GLA_HEADS).reshape(1, GLA_VALUE_WIDTH)
    g_fox = g_fox_out[0].reshape(1, FOX_WIDTH)
    g_fin = g_final.reshape(1, d)
    w_out_b = w_out[0].astype(bf16)
    w_rt = w_router[0].T
    w_rt_hi = w_rt.astype(bf16)
    w_rt_lo = (w_rt - w_rt_hi.astype(f32)).astype(bf16)
    b_r = b_router[0].reshape(N_GROUPS, N_EXPERTS // N_GROUPS, 1)
    w_sg, w_su, w_sd = w_sh_gate[0].astype(bf16), w_sh_up[0].astype(bf16), w_sh_down[0].astype(bf16)

    m_all = _ada(jnp.concatenate([c_prompt, c_sample], axis=0), w_ada[0], b_ada[0])
    m3p = m_all[:bp].reshape(bp, 1, 6 * d)
    m3s = m_all[bp:].reshape(bs, 1, 6 * d)

    groups = []
    for (x, m3, bsz, length) in ((x_prompt, m3p, bp, lp), (x_sample, m3s, bs, ls)):
        if length >= GROUP:
            nb, rpb_in, tpb_in = 1, 1024, length // 1024
            rpb, tpb = GROUP, length // GROUP
        else:
            nb, rpb_in, tpb_in = GROUP // length, length, 1
            rpb, tpb = length, 1
        groups.append(dict(x2=x.reshape(bsz * length, d), m3=m3, bsz=bsz, length=length,
                           nb=nb, rpb_in=rpb_in, tpb_in=tpb_in, rpb=rpb, tpb=tpb))

    gp, gs = groups
    (gp["gq"], gp["gk"], gp["gv"], gp["gg"], gp["la"], gp["fq"], kt_p, vt_p, ktb_p, vtb_p, lft_p) = _inproj(
        gp["x2"], gp["m3"], g_mix, w_cat, wgk_pad, bgk, bf_pad, w_t, bf_col,
        nb=gp["nb"], rpb=gp["rpb_in"], tiles_per_batch=gp["tpb_in"], seq_minor=True)
    (gs["gq"], gs["gk"], gs["gv"], gs["gg"], gs["la"], gs["fq"], fk_s, fv_s, lf_s) = _inproj(
        gs["x2"], gs["m3"], g_mix, w_cat, wgk_pad, bgk, bf_pad, w_t, bf_col,
        nb=gs["nb"], rpb=gs["rpb_in"], tiles_per_batch=gs["tpb_in"], seq_minor=False)

    s0p = jnp.zeros((bp, GLA_HEADS, GLA_HEAD_K, GLA_HEAD_V), f32)
    gp["og"], sp_new = _gla(gp["gq"], gp["gk"], gp["gv"], gp["la"], s0p, bsz=bp, length=lp, chunk=GLA_CHUNK,
                            cps=_GLA_CHUNKS_PER_STEP)
    gs["og"], ss_new = _gla(gs["gq"], gs["gk"], gs["gv"], gs["la"], state_gla[0], bsz=bs, length=ls, chunk=ls, cps=1)

    c_p = _cumsum_lanes(lft_p)
    c_grp = c_p.reshape(bp, FOX_HEADS // _HG, _HG, lp)
    c_rows = c_grp.transpose(0, 1, 3, 2)
    c_cols = c_grp.reshape(bp, FOX_HEADS // _HG, 1, _HG, lp)
    tile_major = lambda a: a.reshape(bp, 1, FOX_HEADS, FOX_HEAD_DIM, lp)
    gp["of"] = _fox_prompt(gp["fq"], tile_major(ktb_p), tile_major(vtb_p), c_rows, c_cols, bsz=bp, length=lp, t=lp)

    lf_s = lf_s.reshape(bs, ls, FOX_HEADS)
    lpad = -(-ls // 128) * 128
    c_s = _cumsum_lanes(jnp.pad(lf_s.transpose(0, 2, 1), ((0, 0), (0, 0), (0, lpad - ls))))[:, :, :ls]
    of_s = _fox_sample(cache_fox_k[0].transpose(0, 2, 3, 1), cache_fox_v[0].transpose(0, 2, 3, 1),
                       fk_s.reshape(bs, ls, FOX_WIDTH), fv_s.reshape(bs, ls, FOX_WIDTH),
                       gs["fq"].reshape(bs, ls, FOX_WIDTH), cache_fox_logf[0].astype(f32).transpose(0, 2, 1),
                       c_s, c_s.reshape(bs, FOX_HEADS * ls, 1))
    gs["of"] = of_s.reshape(ts, FOX_WIDTH)

    for gr in groups:
        gr["x1"], gr["hp"], gr["wts"], gr["lpos"], gr["lst"], gr["seg"] = _outproj(
            gr["og"], gr["gg"], gr["of"], gr["x2"], gr["m3"], g_gla4, g_fox, g_ffn, w_out_b, w_rt_hi, w_rt_lo, b_r,
            nb=gr["nb"] if gr["length"] < GROUP else 1, rpb=gr["rpb"], tiles_per_batch=gr["tpb"])

    t_all = tp + ts
    n_groups = t_all // GROUP
    tiles_per_block = EXPERT_ROWS // SEG
    n_blocks = -(-(t_all * TOP_K + n_groups * N_EXPERTS * SEG) // EXPERT_ROWS) + N_EXPERTS
    lpos = jnp.concatenate([gp["lpos"], gs["lpos"]], axis=1)
    lst = jnp.concatenate([gp["lst"], gs["lst"]], axis=0)[:, :, 0]
    seg = jnp.concatenate([gp["seg"], gs["seg"]], axis=0)[:, :, 0]
    tot_e = jnp.sum(seg, axis=0)
    blocks_per_e = (tot_e + tiles_per_block - 1) // tiles_per_block
    blk_end = jnp.cumsum(blocks_per_e)
    dst = (blk_end - blocks_per_e)[None, :] * tiles_per_block + jnp.cumsum(seg, axis=0) - seg
    tables = tuple(a.reshape(n_groups * N_EXPERTS) for a in (lst, seg, dst))
    n_used = blk_end[-1:]
    blk = jnp.minimum(jnp.arange(n_blocks, dtype=i32), n_used[0] - 1)
    blk_e = jnp.minimum(jnp.sum((blk_end[None, :] <= blk[:, None]).astype(i32), axis=1), N_EXPERTS - 1)
    tail0 = (blk_end - blocks_per_e) * tiles_per_block + tot_e
    tailn = blocks_per_e * tiles_per_block - tot_e
    misc = jnp.stack([n_used[0], jnp.sum(tailn)]).astype(i32)
    xs = _dispatch(tables, tail0, tailn, misc, lpos, gp["hp"], gs["hp"], n_blocks=n_blocks)
    y = _experts(blk_e, n_used.astype(i32), xs, w_exp_gate[0], w_exp_up[0], w_exp_down[0], block_rows=EXPERT_ROWS)

    wts_all = jnp.concatenate([gp["wts"], gs["wts"]], axis=1)
    outs = []
    for gr, g0 in ((gp, 0), (gs, tp // GROUP)):
        outs.append(_combine(tables, lpos, wts_all, y, gr["hp"], gr["x1"], gr["m3"], w_sg, w_su, w_sd, g_fin,
                             nb=gr["nb"] if gr["length"] < GROUP else 1, rpb=gr["rpb"],
                             tiles_per_batch=gr["tpb"], g0=g0))
    y_prompt = outs[0].reshape(bp, lp, d)
    y_sample = outs[1].reshape(bs, ls, d)

    return (y_prompt, y_sample, sp_new[None], ss_new[None],
            kt_p.transpose(0, 3, 1, 2)[None], vt_p.transpose(0, 3, 1, 2)[None], lft_p.transpose(0, 2, 1)[None],
            fk_s.reshape(1, bs, ls, FOX_HEADS, FOX_HEAD_DIM), fv_s.reshape(1, bs, ls, FOX_HEADS, FOX_HEAD_DIM),
            lf_s[None])
```

```python
import functools

import jax
import jax.numpy as jnp
from jax import lax
from jax.experimental import pallas as pl
from jax.experimental.pallas import tpu as pltpu

f32, bf16, i32 = jnp.float32, jnp.bfloat16, jnp.int32

D_MODEL = 1024
GLA_HEADS, GLA_HEAD_K, GLA_HEAD_V = 4, 64, 128
GLA_KEY_WIDTH, GLA_VALUE_WIDTH, GLA_LOW_RANK = 256, 512, 16
GLA_GATE_NORMALIZER = 16.0
GLA_CHUNK = 64
FOX_HEADS, FOX_HEAD_DIM, FOX_WIDTH = 8, 64, 512
N_EXPERTS, N_GROUPS, TOPK_GROUPS, TOP_K = 64, 8, 4, 8
D_EXPERT = 256
ROUTED_SCALE = 2.5
NORM_EPS = 1e-6
EXPERT_ROWS = 1024
GROUP = 256
SEG = 16
PCHUNK = 256
STAGE_ROWS = -(-(TOP_K * GROUP + N_EXPERTS * SEG) // (2 * PCHUNK)) * (2 * PCHUNK)
TILES_PER_BLOCK = EXPERT_ROWS // SEG
VMEM_LIMIT = 56 * 1024 * 1024


def _cparams(sem):
    return pltpu.CompilerParams(dimension_semantics=sem, vmem_limit_bytes=VMEM_LIMIT)


def _log_sigmoid(x):
    return jnp.minimum(x, 0.0) - jnp.log1p(jnp.exp(-jnp.abs(x)))


def _silu(x):
    return x * jax.nn.sigmoid(x)


def _ada_body(c_ref, w_ref, b_ref, o_ref):
    a = _silu(c_ref[...]).astype(bf16)
    o_ref[...] = jnp.dot(a, w_ref[...].astype(bf16), preferred_element_type=f32) + b_ref[...]


def _ada(c_all, w_ada, b_ada):
    nb, d = c_all.shape
    n = w_ada.shape[1]
    tn = 768
    return pl.pallas_call(
        _ada_body,
        grid=(n // tn,),
        in_specs=[pl.BlockSpec((nb, d), lambda j: (0, 0)),
                  pl.BlockSpec((d, tn), lambda j: (0, j)),
                  pl.BlockSpec((1, tn), lambda j: (0, j))],
        out_specs=pl.BlockSpec((nb, tn), lambda j: (0, j)),
        out_shape=jax.ShapeDtypeStruct((nb, n), f32),
        compiler_params=_cparams(("arbitrary",)),
        name="ada",
    )(c_all, w_ada, b_ada.reshape(1, n))


_C_GQ, _C_GK, _C_GV, _C_GG, _C_FQ, _C_FK, _C_FV, _C_SM, _C_END = 0, 256, 512, 1024, 1536, 2048, 2560, 3072, 3200


_NT = (((1,), (1,)), ((), ()))
_TN = (((0,), (0,)), ((), ()))


def _inproj_body(x_ref, sh_ref, sc_ref, g_ref, w_ref, wgk_ref, bgk_ref, bf_ref, wt_ref, bfc_ref,
                 gq_ref, gk_ref, gv_ref, gg_ref, la_ref, fq_ref, k_ref, v_ref, *rest, nb, rpb, seq_minor):
    tm = nb * rpb
    x = x_ref[...]
    y = x * lax.rsqrt(jnp.mean(x * x, axis=-1, keepdims=True) + NORM_EPS) * g_ref[...]
    h = (y.reshape(nb, rpb, D_MODEL) * (1.0 + sc_ref[...]) + sh_ref[...]).reshape(tm, D_MODEL).astype(bf16)

    def proj(a, b):
        return jnp.dot(h, w_ref[:, a:b], preferred_element_type=f32)

    gq_ref[...] = proj(_C_GQ, _C_GK) * (GLA_HEAD_K ** -0.5)
    gk_ref[...] = proj(_C_GK, _C_GV)
    gv_ref[...] = proj(_C_GV, _C_GG)
    gg_ref[...] = proj(_C_GG, _C_FQ)
    fq_ref[...] = (proj(_C_FQ, _C_FK) * (FOX_HEAD_DIM ** -0.5)).astype(bf16)
    zs = proj(_C_SM, _C_END)
    ga = jnp.dot(zs.astype(bf16), wgk_ref[...], preferred_element_type=f32) + bgk_ref[...]
    la_ref[...] = _log_sigmoid(ga) * (1.0 / GLA_GATE_NORMALIZER)
    if seq_minor:
        kb_ref, vb_ref, lf_ref = rest
        shape4 = (FOX_HEADS, FOX_HEAD_DIM, tm)
        kt = lax.dot_general(wt_ref[0:FOX_WIDTH, :], h, _NT, preferred_element_type=f32)
        k_ref[0] = kt.reshape(shape4)
        kb_ref[0] = kt.astype(bf16).reshape(shape4)
        vt = lax.dot_general(wt_ref[FOX_WIDTH:2 * FOX_WIDTH, :], h, _NT, preferred_element_type=f32)
        v_ref[0] = vt.reshape(shape4)
        vb_ref[0] = vt.astype(bf16).reshape(shape4)
        zf = lax.dot_general(wt_ref[2 * FOX_WIDTH:, :], h, _NT, preferred_element_type=f32)
        lf_ref[0] = _log_sigmoid(zf + bfc_ref[...])[:FOX_HEADS, :]
    else:
        (lf_ref,) = rest
        k_ref[...] = proj(_C_FK, _C_FV)
        v_ref[...] = proj(_C_FV, _C_SM)
        lf_ref[...] = _log_sigmoid(zs + bf_ref[...])[:, :FOX_HEADS]


def _inproj(x2, m3, g_mix, w_cat, wgk_pad, bgk, bf_pad, w_t, bf_col, *, nb, rpb, tiles_per_batch, seq_minor):
    t, d = x2.shape
    tm = nb * rpb
    tpb = tiles_per_batch
    bsz = t // (tm * tpb) if seq_minor else None
    row = lambda w: pl.BlockSpec((tm, w), lambda i: (i, 0))
    mod = lambda c: pl.BlockSpec((nb, 1, d), lambda i, c=c: (i // tpb, 0, c))
    full = lambda a: pl.BlockSpec(a.shape, lambda i: (0,) * a.ndim)
    widths = [(GLA_KEY_WIDTH, f32), (GLA_KEY_WIDTH, f32), (GLA_VALUE_WIDTH, f32), (GLA_VALUE_WIDTH, f32),
              (GLA_KEY_WIDTH, f32), (FOX_WIDTH, bf16)]
    out_specs = [row(w) for w, _ in widths]
    out_shape = [jax.ShapeDtypeStruct((t, w), dt) for w, dt in widths]
    if seq_minor:
        length = tm * tpb
        hd = pl.BlockSpec((1, FOX_HEADS, FOX_HEAD_DIM, tm), lambda i: (i // tpb, 0, 0, i % tpb))
        out_specs += [hd, hd, hd, hd, pl.BlockSpec((1, FOX_HEADS, tm), lambda i: (i // tpb, 0, i % tpb))]
        out_shape += [jax.ShapeDtypeStruct((bsz, FOX_HEADS, FOX_HEAD_DIM, length), f32)] * 2
        out_shape += [jax.ShapeDtypeStruct((bsz, FOX_HEADS, FOX_HEAD_DIM, length), bf16)] * 2
        out_shape += [jax.ShapeDtypeStruct((bsz, FOX_HEADS, length), f32)]
    else:
        out_specs += [row(FOX_WIDTH), row(FOX_WIDTH), row(FOX_HEADS)]
        out_shape += [jax.ShapeDtypeStruct((t, w), f32) for w in (FOX_WIDTH, FOX_WIDTH, FOX_HEADS)]
    return pl.pallas_call(
        functools.partial(_inproj_body, nb=nb, rpb=rpb, seq_minor=seq_minor),
        grid=(t // tm,),
        in_specs=[row(d), mod(0), mod(1), full(g_mix), full(w_cat), full(wgk_pad), full(bgk), full(bf_pad),
                  full(w_t), full(bf_col)],
        out_specs=out_specs,
        out_shape=out_shape,
        compiler_params=_cparams(("arbitrary",)),
        name="inproj",
    )(x2, m3, m3, g_mix, w_cat, wgk_pad, bgk, bf_pad, w_t, bf_col)


def _cumsum_chunks(read, rows, n, *, chunk=128, suffix=False):
    r = lax.broadcasted_iota(i32, (chunk, chunk), 0)
    c = lax.broadcasted_iota(i32, (chunk, chunk), 1)
    tri = (r <= c).astype(bf16)
    local = []
    for j in range(n // chunk):
        x = read(slice(j * chunk, (j + 1) * chunk))
        x1 = x.astype(bf16)
        r1 = x - x1.astype(f32)
        x2 = r1.astype(bf16)
        x3 = (r1 - x2.astype(f32)).astype(bf16)
        local.append(jnp.dot(x1, tri, preferred_element_type=f32) + jnp.dot(x2, tri, preferred_element_type=f32)
                     + jnp.dot(x3, tri, preferred_element_type=f32))
    offsets = [jnp.zeros((rows, 1), f32)]
    for j in range(n // chunk):
        offsets.append(offsets[-1] + local[j][:, chunk - 1:chunk])
    total = offsets[-1] if suffix else 0.0
    return [local[j] + (offsets[j] - total) for j in range(n // chunk)]


def _cumsum_body(x_ref, o_ref, *, chunk, suffix):
    n = x_ref.shape[-1]
    out = _cumsum_chunks(lambda sl: x_ref[0, :, sl], x_ref.shape[1], n, chunk=chunk, suffix=suffix)
    for j in range(n // chunk):
        o_ref[0, :, j * chunk:(j + 1) * chunk] = out[j]


def _cumsum_lanes(x, *, suffix=False):
    b, h, n = x.shape
    return pl.pallas_call(
        functools.partial(_cumsum_body, chunk=128, suffix=suffix),
        grid=(b,),
        in_specs=[pl.BlockSpec((1, h, n), lambda i: (i, 0, 0))],
        out_specs=pl.BlockSpec((1, h, n), lambda i: (i, 0, 0)),
        out_shape=jax.ShapeDtypeStruct((b, h, n), f32),
        compiler_params=_cparams(("arbitrary",)),
        name="cumsum",
    )(x)


def _gla_body(q_ref, k_ref, v_ref, la_ref, s0_ref, o_ref, sout_ref, s_scr, *, chunk, cps, n_steps):
    si = pl.program_id(1)

    @pl.when(si == 0)
    def _():
        s_scr[...] = s0_ref[0].reshape(GLA_KEY_WIDTH, GLA_HEAD_V)

    r = lax.broadcasted_iota(i32, (chunk, chunk), 0)
    c = lax.broadcasted_iota(i32, (chunk, chunk), 1)
    causal = c <= r
    tril = causal.astype(bf16)
    ones = jnp.ones((chunk, GLA_HEAD_V), bf16)
    lane_head = lax.broadcasted_iota(i32, (1, GLA_KEY_WIDTH), 1) // GLA_HEAD_K
    zero = jnp.zeros((), bf16)
    mid = chunk // 2
    state = s_scr[...]
    for ch in range(cps):
        t = slice(ch * chunk, (ch + 1) * chunk)
        g = la_ref[t, :]
        g1 = g.astype(bf16)
        r1 = g - g1.astype(f32)
        g2 = r1.astype(bf16)
        g3 = (r1 - g2.astype(f32)).astype(bf16)
        b = (jnp.dot(tril, g1, preferred_element_type=f32) + jnp.dot(tril, g2, preferred_element_type=f32)
             + jnp.dot(tril, g3, preferred_element_type=f32))
        b_mid = b[mid:mid + 1, :]
        b_last = b[chunk - 1:chunk, :]
        q = q_ref[t, :]
        k = k_ref[t, :]
        qs = (q * jnp.exp(b - b_mid)).astype(bf16)
        ks = (k * jnp.exp(b_mid - b)).astype(bf16)
        qe = (q * jnp.exp(b)).astype(bf16)
        kd = (k * jnp.exp(b_last - b)).astype(bf16)
        decay = jnp.exp(lax.dot_general(g1, ones, _TN, preferred_element_type=f32)
                        + lax.dot_general(g2, ones, _TN, preferred_element_type=f32)
                        + lax.dot_general(g3, ones, _TN, preferred_element_type=f32))
        state_b = state.astype(bf16)
        head_rows = lambda x: jnp.concatenate([jnp.where(lane_head == h, x, zero) for h in range(GLA_HEADS)], axis=0)
        a_all = lax.dot_general(head_rows(qs), ks, _NT, preferred_element_type=f32)
        o_state = jnp.dot(head_rows(qe), state_b, preferred_element_type=f32)
        v = v_ref[t, :].astype(bf16)
        upd_all = lax.dot_general(kd, v, _TN, preferred_element_type=f32)
        upd = []
        for h in range(GLA_HEADS):
            trow = slice(h * chunk, (h + 1) * chunk)
            rows = slice(h * GLA_HEAD_K, (h + 1) * GLA_HEAD_K)
            cols = slice(h * GLA_HEAD_V, (h + 1) * GLA_HEAD_V)
            a = jnp.where(causal, a_all[trow, :], 0.0).astype(bf16)
            o_ref[t, cols] = o_state[trow, :] + jnp.dot(a, v[:, cols], preferred_element_type=f32)
            upd.append(upd_all[rows, cols])
        state = state * decay + jnp.concatenate(upd, axis=0)
    s_scr[...] = state

    @pl.when(si == n_steps - 1)
    def _():
        sout_ref[0] = state.reshape(GLA_HEADS, GLA_HEAD_K, GLA_HEAD_V)


def _gla(gq, gk, gv, la, s0, *, bsz, length, chunk, cps):
    n_steps = length // (chunk * cps)
    rows = chunk * cps
    row = lambda w: pl.BlockSpec((rows, w), lambda b, s: (b * n_steps + s, 0))
    st = pl.BlockSpec((1, GLA_HEADS, GLA_HEAD_K, GLA_HEAD_V), lambda b, s: (b, 0, 0, 0))
    return pl.pallas_call(
        functools.partial(_gla_body, chunk=chunk, cps=cps, n_steps=n_steps),
        grid=(bsz, n_steps),
        in_specs=[row(GLA_KEY_WIDTH), row(GLA_KEY_WIDTH), row(GLA_VALUE_WIDTH), row(GLA_KEY_WIDTH), st],
        out_specs=[row(GLA_VALUE_WIDTH), st],
        out_shape=[jax.ShapeDtypeStruct((bsz * length, GLA_VALUE_WIDTH), f32),
                   jax.ShapeDtypeStruct((bsz, GLA_HEADS, GLA_HEAD_K, GLA_HEAD_V), f32)],
        scratch_shapes=[pltpu.VMEM((GLA_KEY_WIDTH, GLA_HEAD_V), f32)],
        compiler_params=_cparams(("arbitrary", "arbitrary")),
        name="gla",
    )(gq, gk, gv, la, s0)


_HG = 4


_HGW = _HG * FOX_HEAD_DIM
_FOX_RB, _FOX_CB = 256, 256


def _foxp_body(q_ref, k_ref, v_ref, cq_ref, ck_ref, o_ref, qm_scr, vone_scr, cqb_scr, m_scr, sh_scr, acc_scr, *, t):
    qi = pl.program_id(2)
    n_tiles = k_ref.shape[1]
    lane_head = lax.broadcasted_iota(i32, (1, _HGW), 1) // FOX_HEAD_DIM

    @pl.when(qi == 0)
    def _():
        row_head = lax.broadcasted_iota(i32, (_HGW, 1), 0) // FOX_HEAD_DIM
        for kt in range(n_tiles):
            vt = v_ref[0, kt].reshape(_HGW, t)
            for j in range(_HG):
                vone_scr[j, kt] = jnp.where(row_head == j, vt, jnp.ones((), bf16))

    q = q_ref[...]
    for j in range(_HG):
        qm_scr[j] = jnp.where(lane_head == j, q, jnp.zeros((), bf16))
        cqb_scr[j] = jnp.broadcast_to(cq_ref[0, 0, :, j:j + 1], (t, 128))
    m_scr[...] = jnp.full(m_scr.shape, -jnp.inf, f32)
    acc_scr[...] = jnp.zeros(acc_scr.shape, f32)

    def logits(ki, j, rb, cb, diagonal):
        r0, c0 = rb * _FOX_RB, cb * _FOX_CB
        if diagonal and c0 > r0 + _FOX_RB - 1:
            return None
        kt = k_ref[0, ki, :, :, c0:c0 + _FOX_CB].reshape(_HGW, _FOX_CB)
        u = jnp.dot(qm_scr[j, r0:r0 + _FOX_RB, :], kt, preferred_element_type=f32) - ck_ref[0, 0, ki, j:j + 1, c0:c0 + _FOX_CB]
        if diagonal and c0 + _FOX_CB - 1 > r0:
            qpos = r0 + lax.broadcasted_iota(i32, (_FOX_RB, _FOX_CB), 0)
            kpos = c0 + lax.broadcasted_iota(i32, (_FOX_RB, _FOX_CB), 1)
            u = jnp.where(kpos <= qpos, u, -jnp.inf)
        return u

    def max_pass(ki, diagonal):
        for j in range(_HG):
            for cb in range(t // _FOX_CB):
                for rb in range(t // _FOX_RB):
                    u = logits(ki, j, rb, cb, diagonal)
                    if u is not None:
                        rows = slice(rb * _FOX_RB, (rb + 1) * _FOX_RB)
                        m = m_scr[j, rows, :]
                        for c in range(_FOX_CB // 128):
                            m = jnp.maximum(m, u[:, c * 128:(c + 1) * 128])
                        m_scr[j, rows, :] = m

    def sum_pass(ki, diagonal):
        for j in range(_HG):
            for cb in range(t // _FOX_CB):
                vt = vone_scr[j, ki, :, cb * _FOX_CB:(cb + 1) * _FOX_CB]
                for rb in range(t // _FOX_RB):
                    u = logits(ki, j, rb, cb, diagonal)
                    if u is not None:
                        rows = slice(rb * _FOX_RB, (rb + 1) * _FOX_RB)
                        p = jnp.exp(u - jnp.concatenate([sh_scr[j, rows, :]] * (_FOX_CB // 128), axis=-1))
                        acc_scr[j, rows, :] += lax.dot_general(p.astype(bf16), vt, _NT, preferred_element_type=f32)

    lax.fori_loop(0, qi, lambda ki, c: (max_pass(ki, False), c)[1], 0)
    for j in range(_HG):
        for rb in range(t // _FOX_RB):
            rows = slice(rb * _FOX_RB, (rb + 1) * _FOX_RB)
            m = m_scr[j, rows, :]
            for cb in range(t // _FOX_CB):
                u = logits(qi, j, rb, cb, True)
                if u is not None:
                    for c in range(_FOX_CB // 128):
                        m = jnp.maximum(m, u[:, c * 128:(c + 1) * 128])
            cq = cqb_scr[j, rows, :]
            m_full = jnp.max(m, axis=-1, keepdims=True) + cq
            sh_scr[j, rows, :] = m_full - cq
    lax.fori_loop(0, qi, lambda ki, c: (sum_pass(ki, False), c)[1], 0)
    sum_pass(qi, True)

    out = jnp.zeros((t, _HGW), f32)
    for j in range(_HG):
        acc = acc_scr[j]
        other = acc[:, 128:256] if j < _HG // 2 else acc[:, 0:128]
        inv = 1.0 / other
        out = jnp.where(lane_head == j, acc * jnp.concatenate([inv, inv], axis=-1), out)
    o_ref[...] = out


def _fox_prompt(fq, kt, vt, c_rows, c_cols, *, bsz, length, t=512):
    n = length // t
    kv = pl.BlockSpec((1, n, _HG, FOX_HEAD_DIM, t), lambda b, g, qi: (b, 0, g, 0, 0))
    return pl.pallas_call(
        functools.partial(_foxp_body, t=t),
        grid=(bsz, FOX_HEADS // _HG, n),
        in_specs=[pl.BlockSpec((t, _HGW), lambda b, g, qi: (b * n + qi, g)), kv, kv,
                  pl.BlockSpec((1, 1, t, _HG), lambda b, g, qi: (b, g, qi, 0)),
                  pl.BlockSpec((1, 1, n, _HG, t), lambda b, g, qi: (b, g, 0, 0, 0))],
        out_specs=pl.BlockSpec((t, _HGW), lambda b, g, qi: (b * n + qi, g)),
        out_shape=jax.ShapeDtypeStruct((bsz * length, FOX_WIDTH), f32),
        scratch_shapes=[pltpu.VMEM((_HG, t, _HGW), bf16), pltpu.VMEM((_HG, n, _HGW, t), bf16),
                        pltpu.VMEM((_HG, t, 128), f32), pltpu.VMEM((_HG, t, 128), f32),
                        pltpu.VMEM((_HG, t, 128), f32), pltpu.VMEM((_HG, t, _HGW), f32)],
        compiler_params=_cparams(("arbitrary",) * 3),
        name="fox_prompt",
    )(fq, kt, vt, c_rows, c_cols)


def _foxs_body(kc_ref, vc_ref, kn_ref, vn_ref, q_ref, lfp_ref, ckn_ref, cq_ref, o_ref, s_scr,
               *, past, new, kchunk):
    nrow = FOX_HEADS * new
    q = q_ref[0]
    lane_head = lax.broadcasted_iota(i32, (1, FOX_WIDTH), 1) // FOX_HEAD_DIM
    qbd = jnp.concatenate([jnp.where(lane_head == h, q, jnp.zeros((), bf16)) for h in range(FOX_HEADS)], axis=0)
    cq = cq_ref[0]

    def per_head_rows(c8):
        return jnp.concatenate([jnp.broadcast_to(c8[h:h + 1, :], (new, c8.shape[1])) for h in range(FOX_HEADS)], axis=0)

    ck_chunk = 128
    ckp = _cumsum_chunks(lambda sl: lfp_ref[0, :, sl], FOX_HEADS, past, chunk=ck_chunk, suffix=True)
    n_chunks = past // kchunk
    m = jnp.full((nrow, 1), -jnp.inf, f32)
    for c in range(n_chunks):
        cols = slice(c * kchunk, (c + 1) * kchunk)
        kt = kc_ref[0, :, :, cols].reshape(FOX_WIDTH, kchunk).astype(bf16)
        ck = jnp.concatenate(ckp[c * (kchunk // ck_chunk):(c + 1) * (kchunk // ck_chunk)], axis=-1)
        s = jnp.dot(qbd, kt, preferred_element_type=f32) + cq - per_head_rows(ck)
        s_scr[:, cols] = s
        m = jnp.maximum(m, jnp.max(s, axis=-1, keepdims=True))
    sn = lax.dot_general(qbd, kn_ref[0].astype(bf16), _NT, preferred_element_type=f32)
    sn = sn + cq - per_head_rows(ckn_ref[0])
    qry_i = lax.broadcasted_iota(i32, (nrow, new), 0) % new
    key_j = lax.broadcasted_iota(i32, (nrow, new), 1)
    sn = jnp.where(key_j <= qry_i, sn, -jnp.inf)
    m = jnp.maximum(m, jnp.max(sn, axis=-1, keepdims=True))
    pn = jnp.exp(sn - m)
    l = jnp.sum(pn, axis=-1, keepdims=True)
    for c in range(n_chunks):
        cols = slice(c * kchunk, (c + 1) * kchunk)
        p = jnp.exp(s_scr[:, cols] - m)
        s_scr[:, cols] = p
        l = l + jnp.sum(p, axis=-1, keepdims=True)
    inv = 1.0 / l
    acc = jnp.dot((pn * inv).astype(bf16), vn_ref[0].astype(bf16), preferred_element_type=f32)
    for c in range(n_chunks):
        cols = slice(c * kchunk, (c + 1) * kchunk)
        vt = vc_ref[0, :, :, cols].reshape(FOX_WIDTH, kchunk).astype(bf16)
        acc = acc + lax.dot_general((s_scr[:, cols] * inv).astype(bf16), vt, _NT, preferred_element_type=f32)
    for h in range(FOX_HEADS):
        o_ref[0, :, h * FOX_HEAD_DIM:(h + 1) * FOX_HEAD_DIM] = (
            acc[h * new:(h + 1) * new, h * FOX_HEAD_DIM:(h + 1) * FOX_HEAD_DIM])


def _fox_sample(kc_t, vc_t, kn, vn, q, lfp, ckn, cq, *, kchunk=512):
    bsz, _, _, past = kc_t.shape
    new, w = kn.shape[1], kn.shape[2]
    blk = lambda a: pl.BlockSpec((1,) + a.shape[1:], lambda b: (b,) + (0,) * (a.ndim - 1))
    return pl.pallas_call(
        functools.partial(_foxs_body, past=past, new=new, kchunk=kchunk),
        grid=(bsz,),
        in_specs=[blk(kc_t), blk(vc_t), blk(kn), blk(vn), blk(q), blk(lfp), blk(ckn), blk(cq)],
        out_specs=pl.BlockSpec((1, new, w), lambda b: (b, 0, 0)),
        out_shape=jax.ShapeDtypeStruct((bsz, new, w), f32),
        scratch_shapes=[pltpu.VMEM((FOX_HEADS * new, past), f32)],
        compiler_params=_cparams(("arbitrary",)),
        name="fox_sample",
    )(kc_t, vc_t, kn, vn, q, lfp, ckn, cq)


def _route(logits_t, b_router):
    ne, tm = logits_t.shape
    gsz = ne // N_GROUPS
    shape3 = (N_GROUPS, gsz, tm)
    scores = jax.nn.sigmoid(logits_t).reshape(shape3)
    choice = scores + b_router
    im = lax.broadcasted_iota(i32, shape3, 1)
    ig = lax.broadcasted_iota(i32, (N_GROUPS, 1, tm), 0)
    ie = lax.broadcasted_iota(i32, shape3, 0) * gsz + im
    neg = -jnp.inf
    m1 = jnp.max(choice, axis=1, keepdims=True)
    first = jnp.min(jnp.where(choice == m1, im, gsz), axis=1, keepdims=True)
    m2 = jnp.max(jnp.where(im == first, neg, choice), axis=1, keepdims=True)
    cur = m1 + m2
    gsel = jnp.zeros(cur.shape, i32)
    for _ in range(TOPK_GROUPS):
        mx = jnp.max(cur, axis=0, keepdims=True)
        hit = ig == jnp.min(jnp.where(cur == mx, ig, N_GROUPS), axis=0, keepdims=True)
        gsel = jnp.where(hit, 1, gsel)
        cur = jnp.where(hit, neg, cur)
    cur = jnp.where(jnp.broadcast_to(gsel, shape3) > 0, choice, neg)
    ids, ws = [], []
    for _ in range(TOP_K):
        mx = jnp.max(jnp.max(cur, axis=1, keepdims=True), axis=0, keepdims=True)
        cand = jnp.where(cur == mx, ie, ne)
        fi = jnp.min(jnp.min(cand, axis=1, keepdims=True), axis=0, keepdims=True)
        hit = ie == fi
        w = jnp.where(hit, scores, 0.0)
        ids.append(fi.reshape(1, tm))
        ws.append(jnp.sum(jnp.sum(w, axis=1, keepdims=True), axis=0, keepdims=True).reshape(1, tm))
        cur = jnp.where(hit, neg, cur)
    tot = ws[0]
    for w in ws[1:]:
        tot = tot + w
    scale = ROUTED_SCALE / (tot + 1e-20)
    return ids, [w * scale for w in ws]


def _outproj_body(og_ref, gg_ref, of_ref, x_ref, gt1_ref, sh2_ref, sc2_ref, ggla_ref, gfox_ref, gffn_ref,
                  wout_ref, wrh_ref, wrl_ref, br_ref, x1_ref, hp_ref, wts_ref, lpos_ref, lst_ref, seg_ref, *, nb, rpb):
    tm = nb * rpb
    og = og_ref[...]
    parts = []
    for h in range(GLA_HEADS):
        seg = og[:, h * GLA_HEAD_V:(h + 1) * GLA_HEAD_V]
        parts.append(seg * lax.rsqrt(jnp.mean(seg * seg, axis=-1, keepdims=True) + NORM_EPS))
    gla_n = jnp.concatenate(parts, axis=-1) * ggla_ref[...] * _silu(gg_ref[...])
    of = of_ref[...]
    fox_n = of * lax.rsqrt(jnp.mean(of * of, axis=-1, keepdims=True) + NORM_EPS) * gfox_ref[...]
    merged = jnp.concatenate([gla_n, fox_n], axis=-1).astype(bf16)
    mix = jnp.dot(merged, wout_ref[...], preferred_element_type=f32)
    x1 = (x_ref[...].reshape(nb, rpb, D_MODEL) + gt1_ref[...] * mix.reshape(nb, rpb, D_MODEL))
    x1_ref[...] = x1.reshape(tm, D_MODEL)
    x1f = x1.reshape(tm, D_MODEL)
    y = x1f * lax.rsqrt(jnp.mean(x1f * x1f, axis=-1, keepdims=True) + NORM_EPS) * gffn_ref[...]
    h2 = (y.reshape(nb, rpb, D_MODEL) * (1.0 + sc2_ref[...]) + sh2_ref[...]).reshape(tm, D_MODEL)
    h_hi = h2.astype(bf16)
    hp_ref[...] = h_hi
    h_lo = (h2 - h_hi.astype(f32)).astype(bf16)
    logits_t = (lax.dot_general(wrh_ref[...], h_hi, _NT, preferred_element_type=f32)
                + lax.dot_general(wrh_ref[...], h_lo, _NT, preferred_element_type=f32)
                + lax.dot_general(wrl_ref[...], h_hi, _NT, preferred_element_type=f32))
    ids, ws = _route(logits_t, br_ref[...])
    for j in range(TOP_K):
        wts_ref[j:j + 1, :] = ws[j]

    ie = lax.broadcasted_iota(i32, (N_EXPERTS, tm), 0)
    multihot = jnp.zeros((N_EXPERTS, tm), f32)
    for j in range(TOP_K):
        multihot = multihot + (ie == ids[j]).astype(f32)
    seg = jnp.maximum(jnp.ceil(jnp.sum(multihot, axis=1, keepdims=True) * (1.0 / SEG)), 1.0)
    r = lax.broadcasted_iota(i32, (N_EXPERTS, N_EXPERTS), 0)
    c = lax.broadcasted_iota(i32, (N_EXPERTS, N_EXPERTS), 1)
    lstart = jnp.dot((c < r).astype(bf16), jnp.broadcast_to(seg, (N_EXPERTS, 128)).astype(bf16),
                     preferred_element_type=f32)[:, :1]
    rr = lax.broadcasted_iota(i32, (tm, tm), 0)
    cc = lax.broadcasted_iota(i32, (tm, tm), 1)
    earlier = jnp.dot(multihot.astype(bf16), (rr < cc).astype(bf16), preferred_element_type=f32)
    pos = lstart * float(SEG) + earlier
    for j in range(TOP_K):
        lpos_ref[j:j + 1, :] = jnp.sum(jnp.where(ie == ids[j], pos, 0.0), axis=0, keepdims=True).astype(i32)
    lst_ref[0] = lstart.astype(i32)
    seg_ref[0] = seg.astype(i32)


def _outproj(og, gg, of, x2, m3, g_gla4, g_fox, g_ffn, w_out, w_rt_hi, w_rt_lo, b_r, *, nb, rpb, tiles_per_batch):
    t, d = x2.shape
    tm = nb * rpb
    assert tm == GROUP
    row = lambda w: pl.BlockSpec((tm, w), lambda i: (i, 0))
    col = pl.BlockSpec((TOP_K, tm), lambda i: (0, i))
    table = pl.BlockSpec((1, N_EXPERTS, 1), lambda i: (i, 0, 0))
    mod = lambda c: pl.BlockSpec((nb, 1, d), lambda i, c=c: (i // tiles_per_batch, 0, c))
    full = lambda a: pl.BlockSpec(a.shape, lambda i: (0,) * a.ndim)
    return pl.pallas_call(
        functools.partial(_outproj_body, nb=nb, rpb=rpb),
        grid=(t // tm,),
        in_specs=[row(GLA_VALUE_WIDTH), row(GLA_VALUE_WIDTH), row(FOX_WIDTH), row(d), mod(2), mod(3), mod(4),
                  full(g_gla4), full(g_fox), full(g_ffn), full(w_out), full(w_rt_hi), full(w_rt_lo), full(b_r)],
        out_specs=[row(d), row(d), col, col, table, table],
        out_shape=[jax.ShapeDtypeStruct((t, d), f32), jax.ShapeDtypeStruct((t, d), bf16),
                   jax.ShapeDtypeStruct((TOP_K, t), f32), jax.ShapeDtypeStruct((TOP_K, t), i32),
                   jax.ShapeDtypeStruct((t // tm, N_EXPERTS, 1), i32), jax.ShapeDtypeStruct((t // tm, N_EXPERTS, 1), i32)],
        compiler_params=_cparams(("arbitrary",)),
        name="outproj",
    )(og, gg, of, x2, m3, m3, m3, g_gla4, g_fox, g_ffn, w_out, w_rt_hi, w_rt_lo, b_r)


def _group_rows(lst_ref, seg_ref, g):
    k = g * N_EXPERTS + N_EXPERTS - 1
    return (lst_ref[k] + seg_ref[k]) * SEG


def _start_segment_copies(lst_ref, seg_ref, dst_ref, g, make_copy, experts=range(N_EXPERTS)):
    for e in experts:
        k = g * N_EXPERTS + e
        make_copy(pl.multiple_of(lst_ref[k] * SEG, SEG), pl.multiple_of(dst_ref[k] * SEG, SEG),
                  seg_ref[k] * SEG).start()


_CHUNKS_PER_TRIP = 2
_MIN_TRIPS = TOP_K * GROUP // (_CHUNKS_PER_TRIP * PCHUNK)
_EXPERTS_PER_TRIP = N_EXPERTS // _MIN_TRIPS
_SLOTS = 3


def _staged_trips(n_rows, pair, issue):
    for c in range(_MIN_TRIPS):
        pair(c)
        if issue is not None:
            issue(range(c * _EXPERTS_PER_TRIP, (c + 1) * _EXPERTS_PER_TRIP))
    lax.fori_loop(_MIN_TRIPS, _chunk_trips(n_rows), lambda c, carry: (pair(c), carry)[1], 0)


def _placement_rows(lpos, vals, r0):
    rid = lax.broadcasted_iota(i32, (PCHUNK, GROUP), 0).astype(f32).astype(bf16)
    rel = (lpos - r0).astype(f32).astype(bf16)
    p = jnp.zeros((PCHUNK, GROUP), bf16)
    for j in range(TOP_K):
        v = jnp.ones((), bf16) if vals is None else vals[j:j + 1, :]
        p = jnp.where(rel[j:j + 1, :] == rid, v, p)
    return p


def _chunk_trips(n_rows):
    return (n_rows + _CHUNKS_PER_TRIP * PCHUNK - 1) // (_CHUNKS_PER_TRIP * PCHUNK)


def _dispatch_body(lst_ref, seg_ref, dst_ref, tail0_ref, tailn_ref, misc_ref, lpos_ref, hp_ref, hs_ref, xs_hbm,
                   stage, zeros, sem, zsem, *, ngp, ng, n_blocks):
    s = pl.program_id(0)
    slot = s % _SLOTS
    prev = (s + _SLOTS - 1) % _SLOTS
    h = jnp.where(s < ngp, hp_ref[...], hs_ref[...])
    lpos = lpos_ref[...]

    def pair(c):
        r0 = pl.multiple_of(c * _CHUNKS_PER_TRIP * PCHUNK, _CHUNKS_PER_TRIP * PCHUNK)
        p = jnp.concatenate([_placement_rows(lpos, None, r0 + k * PCHUNK) for k in range(_CHUNKS_PER_TRIP)], axis=0)
        stage[slot, pl.ds(r0, _CHUNKS_PER_TRIP * PCHUNK), :] = jnp.dot(p, h, preferred_element_type=f32).astype(bf16)

    def issue_previous(experts):
        _start_segment_copies(lst_ref, seg_ref, dst_ref, s - 1, lambda a, d, n: pltpu.make_async_copy(
            stage.at[prev, pl.ds(a, n), :], xs_hbm.at[pl.ds(d, n), :], sem.at[prev]), experts)

    def wait_group(gg, sl):
        n = _group_rows(lst_ref, seg_ref, gg)
        pltpu.make_async_copy(stage.at[sl, pl.ds(0, n), :], xs_hbm.at[pl.ds(0, n), :], sem.at[sl]).wait()

    @pl.when(s >= _SLOTS)
    def _():
        wait_group(s - _SLOTS, slot)

    n_zero = misc_ref[1] * SEG + (n_blocks - misc_ref[0]) * EXPERT_ROWS

    @pl.when(s == 0)
    def _():
        zeros[...] = jnp.zeros(zeros.shape, bf16)

        def per_expert(e, carry):
            n = tailn_ref[e] * SEG

            @pl.when(n > 0)
            def _():
                pltpu.make_async_copy(zeros.at[pl.ds(0, n), :],
                                      xs_hbm.at[pl.ds(pl.multiple_of(tail0_ref[e] * SEG, SEG), n), :], zsem).start()

            return carry

        lax.fori_loop(0, N_EXPERTS, per_expert, 0)
        n_used = misc_ref[0]

        def per_block(b, carry):
            pltpu.make_async_copy(zeros, xs_hbm.at[pl.ds(pl.multiple_of(b * EXPERT_ROWS, EXPERT_ROWS), EXPERT_ROWS), :],
                                  zsem).start()
            return carry

        lax.fori_loop(n_used, n_blocks, per_block, 0)
        _staged_trips(_group_rows(lst_ref, seg_ref, s), pair, None)

    @pl.when(jnp.logical_and(s > 0, s < ng))
    def _():
        _staged_trips(_group_rows(lst_ref, seg_ref, s), pair, issue_previous)

    @pl.when(s == ng)
    def _():
        issue_previous(range(N_EXPERTS))
        for back in range(_SLOTS - 1, 0, -1):
            wait_group(s - back, (s + _SLOTS - back) % _SLOTS)

        @pl.when(n_zero > 0)
        def _():
            pltpu.make_async_copy(xs_hbm.at[pl.ds(0, n_zero), :], xs_hbm.at[pl.ds(0, n_zero), :], zsem).wait()


def _dispatch(tables, tail0, tailn, misc, lpos, hp, hs, *, n_blocks):
    ngp, ngs = hp.shape[0] // GROUP, hs.shape[0] // GROUP
    ng = ngp + ngs
    d = hp.shape[1]
    return pl.pallas_call(
        functools.partial(_dispatch_body, ngp=ngp, ng=ng, n_blocks=n_blocks),
        grid_spec=pltpu.PrefetchScalarGridSpec(
            num_scalar_prefetch=6,
            grid=(ng + 1,),
            in_specs=[pl.BlockSpec((TOP_K, GROUP), lambda g, *_: (0, jnp.minimum(g, ng - 1))),
                      pl.BlockSpec((GROUP, d), lambda g, *_: (jnp.minimum(g, ngp - 1), 0)),
                      pl.BlockSpec((GROUP, d), lambda g, *_: (jnp.clip(g - ngp, 0, ngs - 1), 0))],
            out_specs=pl.BlockSpec(memory_space=pl.ANY),
            scratch_shapes=[pltpu.VMEM((_SLOTS, STAGE_ROWS, d), bf16), pltpu.VMEM((EXPERT_ROWS, d), bf16),
                            pltpu.SemaphoreType.DMA((_SLOTS,)), pltpu.SemaphoreType.DMA]),
        out_shape=jax.ShapeDtypeStruct((n_blocks * EXPERT_ROWS, d), bf16),
        compiler_params=_cparams(("arbitrary",)),
        name="dispatch",
    )(*tables, tail0, tailn, misc, lpos, hp, hs)


def _experts_body(be_ref, nu_ref, xs_ref, wg_ref, wu_ref, wd_ref, y_ref, wg_s, wu_s, wd_s):
    i = pl.program_id(0)
    changed = jnp.logical_or(i == 0, be_ref[i] != be_ref[jnp.maximum(i - 1, 0)])

    @pl.when(changed)
    def _():
        wg_s[...] = wg_ref[0].astype(bf16)
        wu_s[...] = wu_ref[0].astype(bf16)
        wd_s[...] = wd_ref[0].astype(bf16)

    @pl.when(i < nu_ref[0])
    def _():
        x = xs_ref[...]
        g = jnp.dot(x, wg_s[...], preferred_element_type=f32)
        u = jnp.dot(x, wu_s[...], preferred_element_type=f32)
        a = (_silu(g) * u).astype(bf16)
        y_ref[...] = jnp.dot(a, wd_s[...], preferred_element_type=f32).astype(bf16)


def _experts(blk_e, n_used, xs, w_gate, w_up, w_down, *, block_rows):
    n_rows, d = xs.shape
    n_blocks = n_rows // block_rows
    de = w_gate.shape[2]
    used = lambda i, be, nu: (jnp.minimum(i, nu[0] - 1), 0)
    return pl.pallas_call(
        _experts_body,
        grid_spec=pltpu.PrefetchScalarGridSpec(
            num_scalar_prefetch=2,
            grid=(n_blocks,),
            in_specs=[pl.BlockSpec((block_rows, d), used),
                      pl.BlockSpec((1, d, de), lambda i, be, nu: (be[i], 0, 0)),
                      pl.BlockSpec((1, d, de), lambda i, be, nu: (be[i], 0, 0)),
                      pl.BlockSpec((1, de, d), lambda i, be, nu: (be[i], 0, 0))],
            out_specs=pl.BlockSpec((block_rows, d), used),
            scratch_shapes=[pltpu.VMEM((d, de), bf16), pltpu.VMEM((d, de), bf16), pltpu.VMEM((de, d), bf16)]),
        out_shape=jax.ShapeDtypeStruct((n_rows, d), bf16),
        input_output_aliases={2: 0},
        compiler_params=_cparams(("arbitrary",)),
        name="experts",
    )(blk_e, n_used, xs, w_gate, w_up, w_down)


def _combine_body(lst_ref, seg_ref, dst_ref, lpos_ref, wts_ref, h_ref, x1_ref, gt2_ref, wsg_ref, wsu_ref, wsd_ref,
                  gfin_ref, y_hbm, o_ref, ybuf, acc, sem, *, nb, rpb, g0, ng):
    i = pl.program_id(0)
    g = g0 + i
    n_slots = ybuf.shape[0]
    slot = i % n_slots
    ahead = n_slots - 1

    def start_gather(gg, sl, experts=range(N_EXPERTS)):
        _start_segment_copies(lst_ref, seg_ref, dst_ref, gg, lambda s, d, n: pltpu.make_async_copy(
            y_hbm.at[pl.ds(d, n), :], ybuf.at[sl, pl.ds(s, n), :], sem.at[sl]), experts)

    @pl.when(i == 0)
    def _():
        ybuf[...] = jnp.zeros(ybuf.shape, bf16)
        for k in range(min(ahead, ng)):
            start_gather(g + k, k)

    h = h_ref[...]
    gate = jnp.dot(h, wsg_ref[...], preferred_element_type=f32)
    up = jnp.dot(h, wsu_ref[...], preferred_element_type=f32)
    acc[...] = jnp.dot((_silu(gate) * up).astype(bf16), wsd_ref[...], preferred_element_type=f32)

    n_rows = _group_rows(lst_ref, seg_ref, g)
    pltpu.make_async_copy(y_hbm.at[pl.ds(0, n_rows), :], ybuf.at[slot, pl.ds(0, n_rows), :], sem.at[slot]).wait()

    lpos = lpos_ref[...]
    wts = wts_ref[...].astype(bf16)

    def pair(c):
        r0 = pl.multiple_of(c * _CHUNKS_PER_TRIP * PCHUNK, _CHUNKS_PER_TRIP * PCHUNK)
        wt = jnp.concatenate([_placement_rows(lpos, wts, r0 + k * PCHUNK) for k in range(_CHUNKS_PER_TRIP)], axis=0)
        acc[...] += lax.dot_general(wt, ybuf[slot, pl.ds(r0, _CHUNKS_PER_TRIP * PCHUNK), :], _TN,
                                    preferred_element_type=f32)

    @pl.when(i + ahead < ng)
    def _():
        _staged_trips(n_rows, pair, lambda experts: start_gather(g + ahead, (i + ahead) % n_slots, experts))

    @pl.when(i + ahead >= ng)
    def _():
        _staged_trips(n_rows, pair, None)

    out = (x1_ref[...].reshape(nb, rpb, D_MODEL) + gt2_ref[...] * acc[...].reshape(nb, rpb, D_MODEL))
    out = out.reshape(GROUP, D_MODEL)
    o_ref[...] = out * lax.rsqrt(jnp.mean(out * out, axis=-1, keepdims=True) + NORM_EPS) * gfin_ref[...]


def _combine(tables, lpos_rows, wts_rows, y, h2, x1, m3, w_sg, w_su, w_sd, g_fin, *, nb, rpb, tiles_per_batch, g0):
    t, d = x1.shape
    assert nb * rpb == GROUP
    ng = t // GROUP
    row = lambda w: pl.BlockSpec((GROUP, w), lambda i, *_: (i, 0))
    grow = lambda w: pl.BlockSpec((w, GROUP), lambda i, *_: (0, g0 + i))
    full = lambda a: pl.BlockSpec(a.shape, lambda i, *_: (0,) * a.ndim)
    return pl.pallas_call(
        functools.partial(_combine_body, nb=nb, rpb=rpb, g0=g0, ng=ng),
        grid_spec=pltpu.PrefetchScalarGridSpec(
            num_scalar_prefetch=3,
            grid=(ng,),
            in_specs=[grow(TOP_K), grow(TOP_K), row(d), row(d),
                      pl.BlockSpec((nb, 1, d), lambda i, *_: (i // tiles_per_batch, 0, 5)),
                      full(w_sg), full(w_su), full(w_sd), full(g_fin), pl.BlockSpec(memory_space=pl.ANY)],
            out_specs=row(d),
            scratch_shapes=[pltpu.VMEM((2, STAGE_ROWS, d), bf16), pltpu.VMEM((GROUP, d), f32),
                            pltpu.SemaphoreType.DMA((2,))]),
        out_shape=jax.ShapeDtypeStruct((t, d), f32),
        compiler_params=_cparams(("arbitrary",)),
        name="combine",
    )(*tables, lpos_rows, wts_rows, h2, x1, m3, w_sg, w_su, w_sd, g_fin, y)


_GLA_CHUNKS_PER_STEP = 16


def kernel(x_prompt, x_sample, state_gla, cache_fox_k, cache_fox_v, cache_fox_logf, c_prompt, c_sample, w_ada, b_ada, g_norm_mix, g_norm_ffn, w_in, w_gla_gk2, b_gla_gk, g_gla_out, b_fox_f, g_fox_out, w_out, w_router, b_router, w_exp_gate, w_exp_up, w_exp_down, w_sh_gate, w_sh_up, w_sh_down, g_final):
    assert w_ada.shape[0] == 1, "single-layer trunk"
    bp, lp, d = x_prompt.shape
    bs, ls, _ = x_sample.shape
    past = cache_fox_k.shape[2]
    tp, ts = bp * lp, bs * ls

    wi = w_in[0]
    o_glr = 2 * GLA_KEY_WIDTH + 2 * GLA_VALUE_WIDTH
    o_fq = o_glr + GLA_LOW_RANK
    o_ff = o_fq + 3 * FOX_WIDTH
    small = jnp.concatenate([wi[:, o_ff:o_ff + FOX_HEADS], wi[:, o_glr:o_fq],
                             jnp.zeros((d, 128 - FOX_HEADS - GLA_LOW_RANK), f32)], axis=1)
    w_cat = jnp.concatenate([wi[:, :o_glr], wi[:, o_fq:o_ff], small], axis=1).astype(bf16)
    w_t = jnp.concatenate([wi[:, o_fq + FOX_WIDTH:o_ff + FOX_HEADS], jnp.zeros((d, 16 - FOX_HEADS), f32)],
                          axis=1).T.astype(bf16)
    wgk_pad = jnp.zeros((128, GLA_KEY_WIDTH), f32).at[FOX_HEADS:FOX_HEADS + GLA_LOW_RANK].set(w_gla_gk2[0]).astype(bf16)
    bgk = b_gla_gk[0].reshape(1, GLA_KEY_WIDTH)
    bf_pad = jnp.zeros((1, 128), f32).at[0, :FOX_HEADS].set(b_fox_f[0])
    bf_col = jnp.zeros((16, 1), f32).at[:FOX_HEADS, 0].set(b_fox_f[0])
    g_mix = g_norm_mix[0].reshape(1, d)
    g_ffn = g_norm_ffn[0].reshape(1, d)
    g_gla4 = jnp.tile(g_gla_out[0], GLA_HEADS).reshape(1, GLA_VALUE_WIDTH)
    g_fox = g_fox_out[0].reshape(1, FOX_WIDTH)
    g_fin = g_final.reshape(1, d)
    w_out_b = w_out[0].astype(bf16)
    w_rt = w_router[0].T
    w_rt_hi = w_rt.astype(bf16)
    w_rt_lo = (w_rt - w_rt_hi.astype(f32)).astype(bf16)
    b_r = b_router[0].reshape(N_GROUPS, N_EXPERTS // N_GROUPS, 1)
    w_sg, w_su, w_sd = w_sh_gate[0].astype(bf16), w_sh_up[0].astype(bf16), w_sh_down[0].astype(bf16)

    m_all = _ada(jnp.concatenate([c_prompt, c_sample], axis=0), w_ada[0], b_ada[0])
    m3p = m_all[:bp].reshape(bp, 1, 6 * d)
    m3s = m_all[bp:].reshape(bs, 1, 6 * d)

    groups = []
    for (x, m3, bsz, length) in ((x_prompt, m3p, bp, lp), (x_sample, m3s, bs, ls)):
        if length >= GROUP:
            nb, rpb_in, tpb_in = 1, 1024, length // 1024
            rpb, tpb = GROUP, length // GROUP
        else:
            nb, rpb_in, tpb_in = GROUP // length, length, 1
            rpb, tpb = length, 1
        groups.append(dict(x2=x.reshape(bsz * length, d), m3=m3, bsz=bsz, length=length,
                           nb=nb, rpb_in=rpb_in, tpb_in=tpb_in, rpb=rpb, tpb=tpb))

    gp, gs = groups
    (gp["gq"], gp["gk"], gp["gv"], gp["gg"], gp["la"], gp["fq"], kt_p, vt_p, ktb_p, vtb_p, lft_p) = _inproj(
        gp["x2"], gp["m3"], g_mix, w_cat, wgk_pad, bgk, bf_pad, w_t, bf_col,
        nb=gp["nb"], rpb=gp["rpb_in"], tiles_per_batch=gp["tpb_in"], seq_minor=True)
    (gs["gq"], gs["gk"], gs["gv"], gs["gg"], gs["la"], gs["fq"], fk_s, fv_s, lf_s) = _inproj(
        gs["x2"], gs["m3"], g_mix, w_cat, wgk_pad, bgk, bf_pad, w_t, bf_col,
        nb=gs["nb"], rpb=gs["rpb_in"], tiles_per_batch=gs["tpb_in"], seq_minor=False)

    s0p = jnp.zeros((bp, GLA_HEADS, GLA_HEAD_K, GLA_HEAD_V), f32)
    gp["og"], sp_new = _gla(gp["gq"], gp["gk"], gp["gv"], gp["la"], s0p, bsz=bp, length=lp, chunk=GLA_CHUNK,
                            cps=_GLA_CHUNKS_PER_STEP)
    gs["og"], ss_new = _gla(gs["gq"], gs["gk"], gs["gv"], gs["la"], state_gla[0], bsz=bs, length=ls, chunk=ls, cps=1)

    c_p = _cumsum_lanes(lft_p)
    c_grp = c_p.reshape(bp, FOX_HEADS // _HG, _HG, lp)
    c_rows = c_grp.transpose(0, 1, 3, 2)
    c_cols = c_grp.reshape(bp, FOX_HEADS // _HG, 1, _HG, lp)
    tile_major = lambda a: a.reshape(bp, 1, FOX_HEADS, FOX_HEAD_DIM, lp)
    gp["of"] = _fox_prompt(gp["fq"], tile_major(ktb_p), tile_major(vtb_p), c_rows, c_cols, bsz=bp, length=lp, t=lp)

    lf_s = lf_s.reshape(bs, ls, FOX_HEADS)
    lpad = -(-ls // 128) * 128
    c_s = _cumsum_lanes(jnp.pad(lf_s.transpose(0, 2, 1), ((0, 0), (0, 0), (0, lpad - ls))))[:, :, :ls]
    of_s = _fox_sample(cache_fox_k[0].transpose(0, 2, 3, 1), cache_fox_v[0].transpose(0, 2, 3, 1),
                       fk_s.reshape(bs, ls, FOX_WIDTH), fv_s.reshape(bs, ls, FOX_WIDTH),
                       gs["fq"].reshape(bs, ls, FOX_WIDTH), cache_fox_logf[0].astype(f32).transpose(0, 2, 1),
                       c_s, c_s.reshape(bs, FOX_HEADS * ls, 1))
    gs["of"] = of_s.reshape(ts, FOX_WIDTH)

    for gr in groups:
        gr["x1"], gr["hp"], gr["wts"], gr["lpos"], gr["lst"], gr["seg"] = _outproj(
            gr["og"], gr["gg"], gr["of"], gr["x2"], gr["m3"], g_gla4, g_fox, g_ffn, w_out_b, w_rt_hi, w_rt_lo, b_r,
            nb=gr["nb"] if gr["length"] < GROUP else 1, rpb=gr["rpb"], tiles_per_batch=gr["tpb"])

    t_all = tp + ts
    n_groups = t_all // GROUP
    tiles_per_block = EXPERT_ROWS // SEG
    n_blocks = -(-(t_all * TOP_K + n_groups * N_EXPERTS * SEG) // EXPERT_ROWS) + N_EXPERTS
    lpos = jnp.concatenate([gp["lpos"], gs["lpos"]], axis=1)
    lst = jnp.concatenate([gp["lst"], gs["lst"]], axis=0)[:, :, 0]
    seg = jnp.concatenate([gp["seg"], gs["seg"]], axis=0)[:, :, 0]
    tot_e = jnp.sum(seg, axis=0)
    blocks_per_e = (tot_e + tiles_per_block - 1) // tiles_per_block
    blk_end = jnp.cumsum(blocks_per_e)
    dst = (blk_end - blocks_per_e)[None, :] * tiles_per_block + jnp.cumsum(seg, axis=0) - seg
    tables = tuple(a.reshape(n_groups * N_EXPERTS) for a in (lst, seg, dst))
    n_used = blk_end[-1:]
    blk = jnp.minimum(jnp.arange(n_blocks, dtype=i32), n_used[0] - 1)
    blk_e = jnp.minimum(jnp.sum((blk_end[None, :] <= blk[:, None]).astype(i32), axis=1), N_EXPERTS - 1)
    tail0 = (blk_end - blocks_per_e) * tiles_per_block + tot_e
    tailn = blocks_per_e * tiles_per_block - tot_e
    misc = jnp.stack([n_used[0], jnp.sum(tailn)]).astype(i32)
    xs = _dispatch(tables, tail0, tailn, misc, lpos, gp["hp"], gs["hp"], n_blocks=n_blocks)
    y = _experts(blk_e, n_used.astype(i32), xs, w_exp_gate[0], w_exp_up[0], w_exp_down[0], block_rows=EXPERT_ROWS)

    wts_all = jnp.concatenate([gp["wts"], gs["wts"]], axis=1)
    outs = []
    for gr, g0 in ((gp, 0), (gs, tp // GROUP)):
        outs.append(_combine(tables, lpos, wts_all, y, gr["hp"], gr["x1"], gr["m3"], w_sg, w_su, w_sd, g_fin,
                             nb=gr["nb"] if gr["length"] < GROUP else 1, rpb=gr["rpb"],
                             tiles_per_batch=gr["tpb"], g0=g0))
    y_prompt = outs[0].reshape(bp, lp, d)
    y_sample = outs[1].reshape(bs, ls, d)

    return (y_prompt, y_sample, sp_new[None], ss_new[None],
            kt_p.transpose(0, 3, 1, 2)[None], vt_p.transpose(0, 3, 1, 2)[None], lft_p.transpose(0, 2, 1)[None],
            fk_s.reshape(1, bs, ls, FOX_HEADS, FOX_HEAD_DIM), fv_s.reshape(1, bs, ls, FOX_HEADS, FOX_HEAD_DIM),
            lf_s[None])
```
